```python
import math
import jax
import jax.numpy as jnp
from jax import lax
import numpy as np

D_MODEL = 4096
BATCH = 2
SEQ = 8192
DEPTH = 1

BRANCH_WIDTH = D_MODEL // 2
N_BRANCHES = 3
ATTN_HEADS = 16
ATTN_HEAD_DIM = BRANCH_WIDTH // ATTN_HEADS
MOBA_BLOCK = 256
MOBA_TOP_K = 3
Q_CHUNK = 32
SSM_GROUP = 16
SSM_GROUPS = BRANCH_WIDTH // SSM_GROUP
SSM_STATE = 64
DT_MIN = 1e-3
DT_MAX = 1e-1
MEM_LEN = 256
MEM_HEADS = 4
MEM_HEAD_DIM = BRANCH_WIDTH // MEM_HEADS

EPS = 1e-6
NEG = -1e30

kernel_name = 'hybrid_moba_s5_memory_gated_block'


def _rms_norm(x, gain):
    xf = x.astype(jnp.float32)
    y = xf * lax.rsqrt(jnp.mean(xf * xf, axis=-1, keepdims=True) + EPS)
    return (y * gain.astype(jnp.float32)).astype(x.dtype)


def _alibi_slopes(n_heads):
    return jnp.asarray([2.0 ** (-8.0 * (i + 1) / n_heads) for i in range(n_heads)], dtype=jnp.float32)


def _moba_attention(q, k, v):
    bsz, n_heads, seq, hd = q.shape
    n_blocks = -(-seq // MOBA_BLOCK)
    pad = n_blocks * MOBA_BLOCK - seq
    k_pad = jnp.pad(k, ((0, 0), (0, 0), (0, pad), (0, 0)))
    v_pad = jnp.pad(v, ((0, 0), (0, 0), (0, pad), (0, 0)))
    k_blk = k_pad.reshape(bsz, n_heads, n_blocks, MOBA_BLOCK, hd)
    v_blk = v_pad.reshape(bsz, n_heads, n_blocks, MOBA_BLOCK, hd)
    k_mean = jnp.mean(k_blk.astype(jnp.float32), axis=3)
    n_sel = min(MOBA_TOP_K, n_blocks)
    n_g = n_sel * MOBA_BLOCK
    slopes = _alibi_slopes(n_heads)[None, :, None, None]
    scale = hd ** -0.5
    b_idx = jnp.arange(bsz)[:, None, None, None]
    h_idx = jnp.arange(n_heads)[None, :, None, None]
    blk_ids = jnp.arange(n_blocks)
    key_off = jnp.arange(MOBA_BLOCK)

    def one_chunk(c):
        start = c * Q_CHUNK
        own = start // MOBA_BLOCK
        q_c = lax.dynamic_slice_in_dim(q, start, Q_CHUNK, axis=2)
        q_pos = start + jnp.arange(Q_CHUNK)
        gate = jnp.einsum('bhqd,bhnd->bhqn', q_c.astype(jnp.float32), k_mean)
        gate = jnp.where(blk_ids < own, gate, NEG)
        _, sel = lax.top_k(gate, n_sel)
        valid = sel < own
        k_sel = k_blk[b_idx, h_idx, sel]
        v_sel = v_blk[b_idx, h_idx, sel]
        s_sel = jnp.einsum('bhqd,bhqnkd->bhqnk', q_c, k_sel).astype(jnp.float32) * scale
        dist_sel = (q_pos[None, None, :, None, None] - (sel[..., None] * MOBA_BLOCK + key_off)).astype(jnp.float32)
        s_sel = jnp.where(valid[..., None], s_sel - slopes[..., None] * dist_sel, NEG)
        s_sel = s_sel.reshape(bsz, n_heads, Q_CHUNK, n_g)
        k_own = lax.dynamic_slice_in_dim(k_pad, own * MOBA_BLOCK, MOBA_BLOCK, axis=2)
        v_own = lax.dynamic_slice_in_dim(v_pad, own * MOBA_BLOCK, MOBA_BLOCK, axis=2)
        s_own = jnp.einsum('bhqd,bhkd->bhqk', q_c, k_own).astype(jnp.float32) * scale
        dist_own = q_pos[:, None] - (own * MOBA_BLOCK + key_off)[None, :]
        s_own = jnp.where(dist_own >= 0, s_own - slopes * dist_own.astype(jnp.float32), NEG)
        p = jax.nn.softmax(jnp.concatenate([s_sel, s_own], axis=-1), axis=-1).astype(v.dtype)
        out = jnp.einsum('bhqk,bhqkd->bhqd', p[..., :n_g], v_sel.reshape(bsz, n_heads, Q_CHUNK, n_g, hd))
        out = out + jnp.einsum('bhqk,bhkd->bhqd', p[..., n_g:], v_own)
        return out

    chunks = lax.map(one_chunk, jnp.arange(seq // Q_CHUNK))
    return chunks.transpose(1, 2, 0, 3, 4).reshape(bsz, n_heads, seq, hd)


def _cplx_combine(e1, e2):
    a1r, a1i, b1r, b1i = e1
    a2r, a2i, b2r, b2i = e2
    return (a2r * a1r - a2i * a1i,
            a2r * a1i + a2i * a1r,
            a2r * b1r - a2i * b1i + b2r,
            a2r * b1i + a2i * b1r + b2i)


def _s5_branch(u, a_re, a_im, log_dt, b_re, b_im, c_re, c_im, d_skip):
    bsz, seq, width = u.shape
    ug = u.astype(jnp.float32).reshape(bsz, seq, SSM_GROUPS, SSM_GROUP)
    dt = jnp.exp(log_dt.astype(jnp.float32))[:, None]
    ar = a_re.astype(jnp.float32)
    ai = a_im.astype(jnp.float32)
    mag = jnp.exp(dt * ar)
    abar_re = mag * jnp.cos(dt * ai)
    abar_im = mag * jnp.sin(dt * ai)
    den = ar * ar + ai * ai
    nr = abar_re - 1.0
    f_re = (nr * ar + abar_im * ai) / den
    f_im = (abar_im * ar - nr * ai) / den
    br = b_re.astype(jnp.float32)
    bi = b_im.astype(jnp.float32)
    bbar_re = f_re[..., None] * br - f_im[..., None] * bi
    bbar_im = f_re[..., None] * bi + f_im[..., None] * br
    cr = c_re.astype(jnp.float32)
    ci = c_im.astype(jnp.float32)
    d = d_skip.astype(jnp.float32).reshape(SSM_GROUPS, SSM_GROUP)

    def per_example(u_e):
        bu_re = jnp.einsum('sgc,gnc->sgn', u_e, bbar_re)
        bu_im = jnp.einsum('sgc,gnc->sgn', u_e, bbar_im)
        a_r = jnp.broadcast_to(abar_re, bu_re.shape)
        a_i = jnp.broadcast_to(abar_im, bu_im.shape)
        _, _, x_re, x_im = lax.associative_scan(_cplx_combine, (a_r, a_i, bu_re, bu_im), axis=0)
        return (jnp.einsum('sgn,gcn->sgc', x_re, cr)
                - jnp.einsum('sgn,gcn->sgc', x_im, ci)
                + d * u_e)

    y = lax.map(per_example, ug)
    return y.reshape(bsz, seq, width).astype(u.dtype)


def _memory_attention(q, k, v):
    scale = q.shape[-1] ** -0.5
    s = jnp.einsum('bqhd,bkhd->bhqk', q, k).astype(jnp.float32) * scale
    p = jax.nn.softmax(s, axis=-1).astype(v.dtype)
    return jnp.einsum('bhqk,bkhd->bqhd', p, v)


def setup_inputs(seed: int = 0) -> dict:
    key = jax.random.key(seed)
    ks = jax.random.split(key, 24)
    W = BRANCH_WIDTH
    n_in = 8 * W + N_BRANCHES * D_MODEL

    def nrm(k, shape, scale):
        return jax.random.normal(k, shape, jnp.float32) * scale

    n_idx = jnp.arange(SSM_STATE, dtype=jnp.float32)
    return {
        'x': nrm(ks[0], (BATCH, SEQ, D_MODEL), 1.0),
        'mem': nrm(ks[1], (BATCH, MEM_LEN, D_MODEL), 1.0),
        'w_in': nrm(ks[2], (D_MODEL, n_in), D_MODEL ** -0.5),
        'g_norm': 1.0 + nrm(ks[3], (D_MODEL,), 0.02),
        'g_mem': 1.0 + nrm(ks[4], (D_MODEL,), 0.02),
        'w_mem_kv': nrm(ks[5], (D_MODEL, 2 * W), D_MODEL ** -0.5),
        'q_gain_a': 1.0 + nrm(ks[6], (ATTN_HEAD_DIM,), 0.02),
        'k_gain_a': 1.0 + nrm(ks[7], (ATTN_HEAD_DIM,), 0.02),
        'q_gain_c': 1.0 + nrm(ks[8], (MEM_HEAD_DIM,), 0.02),
        'k_gain_c': 1.0 + nrm(ks[9], (MEM_HEAD_DIM,), 0.02),
        'ssm_a_re': -0.5 + nrm(ks[10], (SSM_GROUPS, SSM_STATE), 0.01),
        'ssm_a_im': math.pi * n_idx[None, :] + nrm(ks[11], (SSM_GROUPS, SSM_STATE), 0.01),
        'ssm_log_dt': jax.random.uniform(ks[12], (SSM_GROUPS,), jnp.float32, math.log(DT_MIN), math.log(DT_MAX)),
        'ssm_b_re': nrm(ks[13], (SSM_GROUPS, SSM_STATE, SSM_GROUP), (2 * SSM_GROUP) ** -0.5),
        'ssm_b_im': nrm(ks[14], (SSM_GROUPS, SSM_STATE, SSM_GROUP), (2 * SSM_GROUP) ** -0.5),
        'ssm_c_re': nrm(ks[15], (SSM_GROUPS, SSM_GROUP, SSM_STATE), SSM_STATE ** -0.5),
        'ssm_c_im': nrm(ks[16], (SSM_GROUPS, SSM_GROUP, SSM_STATE), SSM_STATE ** -0.5),
        'ssm_d': nrm(ks[17], (W,), 1.0),
        'w_glu': nrm(ks[18], (W, W), W ** -0.5),
        'b_glu': nrm(ks[19], (W,), 0.01),
        'w_br_a': nrm(ks[20], (W, D_MODEL), W ** -0.5),
        'w_br_s': nrm(ks[21], (W, D_MODEL), W ** -0.5),
        'w_br_c': nrm(ks[22], (W, D_MODEL), W ** -0.5),
        'w_out': nrm(ks[23], (D_MODEL, D_MODEL), D_MODEL ** -0.5),
    }


def reference(x, mem, w_in, g_norm, g_mem, w_mem_kv, q_gain_a, k_gain_a, q_gain_c, k_gain_c,
              ssm_a_re, ssm_a_im, ssm_log_dt, ssm_b_re, ssm_b_im, ssm_c_re, ssm_c_im, ssm_d,
              w_glu, b_glu, w_br_a, w_br_s, w_br_c, w_out):
    bsz, seq, _ = x.shape
    W = BRANCH_WIDTH
    for _layer in range(DEPTH):
        h = _rms_norm(x, g_norm)
        proj = h @ w_in
        splits = [W * i for i in range(1, 9)] + [8 * W + D_MODEL, 8 * W + 2 * D_MODEL]
        q_a, k_a, v_a, z_a, u_s, z_s, q_c, z_c, g_a, g_s, g_c = jnp.split(proj, splits, axis=-1)

        qa = _rms_norm(q_a.reshape(bsz, seq, ATTN_HEADS, ATTN_HEAD_DIM), q_gain_a).transpose(0, 2, 1, 3)
        ka = _rms_norm(k_a.reshape(bsz, seq, ATTN_HEADS, ATTN_HEAD_DIM), k_gain_a).transpose(0, 2, 1, 3)
        va = v_a.reshape(bsz, seq, ATTN_HEADS, ATTN_HEAD_DIM).transpose(0, 2, 1, 3)
        o_a = _moba_attention(qa, ka, va).transpose(0, 2, 1, 3).reshape(bsz, seq, W)
        y_a = o_a * jax.nn.silu(z_a)

        y_s = _s5_branch(u_s, ssm_a_re, ssm_a_im, ssm_log_dt, ssm_b_re, ssm_b_im, ssm_c_re, ssm_c_im, ssm_d)
        y_s = jax.nn.gelu(y_s)
        y_s = y_s * jax.nn.sigmoid(y_s @ w_glu + b_glu)
        y_s = y_s * jax.nn.silu(z_s)

        m = _rms_norm(mem, g_mem)
        k_m, v_m = jnp.split(m @ w_mem_kv, 2, axis=-1)
        n_mem = mem.shape[1]
        qc = _rms_norm(q_c.reshape(bsz, seq, MEM_HEADS, MEM_HEAD_DIM), q_gain_c)
        kc = _rms_norm(k_m.reshape(bsz, n_mem, MEM_HEADS, MEM_HEAD_DIM), k_gain_c)
        vc = v_m.reshape(bsz, n_mem, MEM_HEADS, MEM_HEAD_DIM)
        o_c = _memory_attention(qc, kc, vc).reshape(bsz, seq, W)
        y_c = o_c * jax.nn.silu(z_c)

        merged = (jax.nn.sigmoid(g_a) * (y_a @ w_br_a)
                  + jax.nn.sigmoid(g_s) * (y_s @ w_br_s)
                  + jax.nn.sigmoid(g_c) * (y_c @ w_br_c))
        x = x + merged @ w_out
    return x
```

```python
import functools

import jax
import jax.numpy as jnp
from jax import lax
from jax.experimental import pallas as pl
from jax.experimental.pallas import tpu as pltpu

F32 = jnp.float32
BF16 = jnp.bfloat16

V7X_LANES = 128
V7X_VMEM_BYTES = 64 * 1024 * 1024
V7X_VMEM_RESERVE_BYTES = 6 * 1024 * 1024

ATTN_HEAD_DIM = 128
MOBA_BLOCK = 256
MOBA_TOP_K = 3
SSM_GROUP = 16
SSM_STATE = 64
MEM_HEADS = 4
EPS = 1e-6
NEG = -1e30

SSM_CHUNK = 16
GROUPS_PER_LANE_BLOCK = V7X_LANES // SSM_GROUP
STATE_LANES = GROUPS_PER_LANE_BLOCK * SSM_STATE


def _tile(n, pref):
    t = min(n, pref)
    while n % t:
        t -= V7X_LANES
    assert t > 0
    return t


def _params(block_bytes, scratch_bytes=0, temp_bytes=0, n_axes=1):
    need = 2 * sum(block_bytes) + scratch_bytes + temp_bytes
    limit = min(max(need, 16 * 1024 * 1024), V7X_VMEM_BYTES - V7X_VMEM_RESERVE_BYTES)
    return pltpu.CompilerParams(dimension_semantics=("arbitrary",) * n_axes, vmem_limit_bytes=int(limit))


def _rmsnorm_kernel(x_ref, g_ref, o_ref):
    x = x_ref[...].astype(F32)
    ms = jnp.mean(x * x, axis=-1, keepdims=True)
    o_ref[...] = (x * lax.rsqrt(ms + EPS) * g_ref[...]).astype(o_ref.dtype)


def _rmsnorm(x, gain, rows):
    n, d = x.shape
    tm = _tile(n, rows)
    return pl.pallas_call(
        _rmsnorm_kernel,
        grid=(n // tm,),
        in_specs=[pl.BlockSpec((tm, d), lambda i: (i, 0)), pl.BlockSpec((1, d), lambda i: (0, 0))],
        out_specs=pl.BlockSpec((tm, d), lambda i: (i, 0)),
        out_shape=jax.ShapeDtypeStruct((n, d), BF16),
        compiler_params=_params([tm * d * 4, tm * d * 2], temp_bytes=2 * tm * d * 4),
        name="rmsnorm",
    )(x, gain.reshape(1, d).astype(F32))


def _mm_kernel(a_ref, b_ref, o_ref):
    o_ref[...] = jnp.dot(a_ref[...], b_ref[...], preferred_element_type=F32).astype(o_ref.dtype)


def _matmul(a, b, tm, tn, name):
    m, k = a.shape
    _, n = b.shape
    tm, tn = _tile(m, tm), _tile(n, tn)
    return pl.pallas_call(
        _mm_kernel,
        grid=(m // tm, n // tn),
        in_specs=[pl.BlockSpec((tm, k), lambda i, j: (i, 0)), pl.BlockSpec((k, tn), lambda i, j: (0, j))],
        out_specs=pl.BlockSpec((tm, tn), lambda i, j: (i, j)),
        out_shape=jax.ShapeDtypeStruct((m, n), BF16),
        compiler_params=_params([tm * k * 2, k * tn * 2, tm * tn * 2], temp_bytes=tm * tn * (4 + 4 + 2), n_axes=2),
        name=name,
    )(a, b)


def _moba_kernel(q_ref, k_ref, v_ref, z_ref, qg_ref, kg_ref, slope_ref, o_ref,
                 kn_ref, vt_ref, kmean_ref, add_ref, *, n_blocks):
    i = pl.program_id(2)
    bs = MOBA_BLOCK
    slope = slope_ref[0][:, :1]

    @pl.when(i == 0)
    def _():
        def prep(c, carry):
            rows = pl.ds(pl.multiple_of(c * bs, bs), bs)
            kb = k_ref[rows, :].astype(F32)
            ms = jnp.mean(kb * kb, axis=-1, keepdims=True)
            kn = kb * lax.rsqrt(ms + EPS) * kg_ref[...]
            kn_ref[c] = kn.astype(BF16)
            kmean_ref[pl.ds(c, 1), :] = jnp.mean(kn, axis=0, keepdims=True)
            vt_ref[c] = v_ref[rows, :].astype(F32).T.astype(BF16)
            return carry

        lax.fori_loop(0, n_blocks, prep, 0)

    q = q_ref[...].astype(F32)
    ms = jnp.mean(q * q, axis=-1, keepdims=True)
    qn_t = (q * lax.rsqrt(ms + EPS) * qg_ref[...]).T
    qs_t = (qn_t * (ATTN_HEAD_DIM ** -0.5)).astype(BF16)

    gate = jnp.dot(kmean_ref[...], qn_t, preferred_element_type=F32, precision=lax.Precision.HIGHEST)
    blk = lax.broadcasted_iota(jnp.int32, gate.shape, 0)
    blk_f = blk.astype(F32)
    past = blk < i
    g = jnp.where(past, gate, NEG)
    sel = jnp.zeros(gate.shape, jnp.bool_)
    for _ in range(min(MOBA_TOP_K, n_blocks)):
        top = jnp.max(g, axis=0, keepdims=True)
        first = jnp.min(jnp.where(g == top, blk_f, float(n_blocks)), axis=0, keepdims=True)
        pick = blk_f == first
        sel = jnp.logical_or(sel, pick)
        g = jnp.where(pick, -jnp.inf, g)
    sel = jnp.logical_and(sel, past)
    add_ref[...] = jnp.where(sel, -slope * (bs * (i - blk)).astype(F32), NEG)

    key_off = lax.broadcasted_iota(jnp.int32, (bs, bs), 0)
    qry_off = lax.broadcasted_iota(jnp.int32, (bs, bs), 1)
    bias = -slope * (qry_off - key_off).astype(F32)

    s = jnp.dot(kn_ref[i], qs_t, preferred_element_type=F32)
    s = jnp.where(key_off <= qry_off, s + bias, NEG)
    m0 = jnp.max(s, axis=0, keepdims=True)
    p = jnp.exp(s - m0)
    l0 = jnp.sum(p, axis=0, keepdims=True)
    acc0 = jnp.dot(vt_ref[i], p.astype(BF16), preferred_element_type=F32)

    def past_block(j, carry):
        m, l, acc = carry
        s = jnp.dot(kn_ref[j], qs_t, preferred_element_type=F32) + bias + add_ref[pl.ds(j, 1), :]
        m_new = jnp.maximum(m, jnp.max(s, axis=0, keepdims=True))
        alpha = jnp.exp(m - m_new)
        p = jnp.exp(s - m_new)
        l = alpha * l + jnp.sum(p, axis=0, keepdims=True)
        acc = alpha * acc + jnp.dot(vt_ref[j], p.astype(BF16), preferred_element_type=F32)
        return m_new, l, acc

    _, l, acc = lax.fori_loop(0, i, past_block, (m0, l0, acc0))
    o = (acc / l).T
    o_ref[...] = (o * jax.nn.silu(z_ref[...].astype(F32))).astype(o_ref.dtype)


def _moba(proj, q_gain, k_gain, bsz, seq, width):
    n_heads = width // ATTN_HEAD_DIM
    n_blocks = seq // MOBA_BLOCK
    hd, bs = ATTN_HEAD_DIM, MOBA_BLOCK
    slopes = jnp.asarray([[[2.0 ** (-8.0 * (h + 1) / n_heads)] * V7X_LANES] for h in range(n_heads)], F32)
    scratch = [
        pltpu.VMEM((n_blocks, bs, hd), BF16),
        pltpu.VMEM((n_blocks, hd, bs), BF16),
        pltpu.VMEM((n_blocks, hd), F32),
        pltpu.VMEM((n_blocks, bs), F32),
    ]
    return pl.pallas_call(
        functools.partial(_moba_kernel, n_blocks=n_blocks),
        grid=(bsz, n_heads, n_blocks),
        in_specs=[
            pl.BlockSpec((bs, hd), lambda b, h, i: (b * n_blocks + i, h)),
            pl.BlockSpec((seq, hd), lambda b, h, i: (b, n_heads + h)),
            pl.BlockSpec((seq, hd), lambda b, h, i: (b, 2 * n_heads + h)),
            pl.BlockSpec((bs, hd), lambda b, h, i: (b * n_blocks + i, 3 * n_heads + h)),
            pl.BlockSpec((1, hd), lambda b, h, i: (0, 0)),
            pl.BlockSpec((1, hd), lambda b, h, i: (0, 0)),
            pl.BlockSpec((1, 1, V7X_LANES), lambda b, h, i: (h, 0, 0)),
        ],
        out_specs=pl.BlockSpec((bs, hd), lambda b, h, i: (b * n_blocks + i, h)),
        out_shape=jax.ShapeDtypeStruct((bsz * seq, width), BF16),
        scratch_shapes=scratch,
        compiler_params=_params([2 * seq * hd * 2, 3 * bs * hd * 2], scratch_bytes=2 * seq * hd * 2 + n_blocks * (hd + bs) * 4,
                                temp_bytes=16 * bs * bs * 4, n_axes=3),
        name="moba_attention",
    )(proj, proj, proj, proj, q_gain.reshape(1, hd).astype(F32), k_gain.reshape(1, hd).astype(F32), slopes)


def _s5_kernel(u_ref, ar_ref, ai_ref, ldt_ref, bbr_ref, bbi_ref, ccr_ref, cci_ref, d_ref, y_ref,
               p_ref, qt_ref, t_ref, dk_ref, sinc_ref, xs_ref, *, n_batch, rows_per_batch, row_tile):
    lb, gn, chunk = V7X_LANES, STATE_LANES, SSM_CHUNK
    ar, ai = ar_ref[0], ai_ref[0]
    dt = jnp.exp(ldt_ref[0])
    mag = jnp.exp(dt * ar)
    abr = mag * jnp.cos(dt * ai)
    abi = mag * jnp.sin(dt * ai)
    den = ar * ar + ai * ai
    nr = abr - 1.0
    f_re = (nr * ar + abi * ai) / den
    f_im = (abi * ar - nr * ai) / den

    row_g = lax.broadcasted_iota(jnp.int32, (lb, gn), 0) // SSM_GROUP
    col_g = lax.broadcasted_iota(jnp.int32, (lb, gn), 1) // SSM_STATE
    same = row_g == col_g
    bb_re, bb_im = bbr_ref[0], bbi_ref[0]
    bbar_re = jnp.where(same, f_re * bb_re - f_im * bb_im, 0.0)
    bbar_im = jnp.where(same, f_re * bb_im + f_im * bb_re, 0.0)
    cc_re = jnp.where(same, ccr_ref[0], 0.0)
    cc_im = jnp.where(same, cci_ref[0], 0.0)
    ccs = jnp.concatenate([cc_re, -cc_im], axis=1)

    pr = jnp.ones((1, gn), F32)
    pi = jnp.zeros((1, gn), F32)
    for tau in range(chunk + 1):
        if tau < chunk:
            ptau = jnp.concatenate([bbar_re * pr - bbar_im * pi, bbar_re * pi + bbar_im * pr], axis=1)
            s = chunk - 1 - tau
            p_ref[s * lb:(s + 1) * lb, :] = ptau.astype(BF16)
            dk_ref[tau] = lax.dot_general(ptau, ccs, (((1,), (1,)), ((), ())), preferred_element_type=F32,
                                          precision=lax.Precision.HIGHEST).astype(BF16)
        if tau >= 1:
            t = tau - 1
            qt_ref[t * lb:(t + 1) * lb, :] = jnp.concatenate(
                [cc_re * pr - cc_im * pi, -(cc_re * pi + cc_im * pr)], axis=1).astype(BF16)
        if tau < chunk:
            pr, pi = pr * abr - pi * abi, pr * abi + pi * abr
    al_re, al_im = pr, pi

    zero = jnp.zeros((lb, lb), BF16)
    for s in range(chunk):
        for t in range(chunk):
            t_ref[s * lb:(s + 1) * lb, t * lb:(t + 1) * lb] = dk_ref[t - s] if t >= s else zero

    n_rows = n_batch * rows_per_batch
    for r0 in range(0, n_rows, row_tile):
        sinc_ref[r0:r0 + row_tile, :] = jnp.dot(u_ref[0, r0:r0 + row_tile, :], p_ref[...], preferred_element_type=F32)

    def step(c, carry):
        out = []
        for b in range(n_batch):
            xr, xi = carry[b]
            row = pl.ds(b * rows_per_batch + c, 1)
            xs_ref[row, :gn] = xr
            xs_ref[row, gn:] = xi
            inc = sinc_ref[row, :]
            out.append((al_re * xr - al_im * xi + inc[:, :gn], al_re * xi + al_im * xr + inc[:, gn:]))
        return tuple(out)

    x0 = jnp.zeros((1, gn), F32)
    lax.fori_loop(0, rows_per_batch, step, tuple((x0, x0) for _ in range(n_batch)))

    for r0 in range(0, n_rows, row_tile):
        u = u_ref[0, r0:r0 + row_tile, :]
        y = jnp.dot(u, t_ref[...], preferred_element_type=F32)
        y = y + lax.dot_general(xs_ref[r0:r0 + row_tile, :].astype(BF16), qt_ref[...], (((1,), (1,)), ((), ())),
                                preferred_element_type=F32)
        y = y + d_ref[0] * u.astype(F32)
        y_ref[0, r0:r0 + row_tile, :] = jax.nn.gelu(y).astype(y_ref.dtype)


def _s5(u_blocks, a_re, a_im, log_dt, b_re, b_im, c_re, c_im, d_skip, n_batch):
    nb, n_rows, kdim = u_blocks.shape
    gpb, lb, gn, chunk = GROUPS_PER_LANE_BLOCK, V7X_LANES, STATE_LANES, SSM_CHUNK
    rows_per_batch = n_rows // n_batch
    row_tile = _tile(n_rows, 256)

    def lane_row(v):
        return v.astype(F32).reshape(nb, 1, gn)

    ldt = jnp.repeat(log_dt.astype(F32), SSM_STATE).reshape(nb, 1, gn)
    bt_re = jnp.tile(b_re.astype(F32).reshape(nb, gpb, SSM_STATE, SSM_GROUP).transpose(0, 3, 1, 2).reshape(nb, SSM_GROUP, gn), (1, gpb, 1))
    bt_im = jnp.tile(b_im.astype(F32).reshape(nb, gpb, SSM_STATE, SSM_GROUP).transpose(0, 3, 1, 2).reshape(nb, SSM_GROUP, gn), (1, gpb, 1))
    ct_re = jnp.tile(c_re.astype(F32).reshape(nb, lb, SSM_STATE), (1, 1, gpb))
    ct_im = jnp.tile(c_im.astype(F32).reshape(nb, lb, SSM_STATE), (1, 1, gpb))
    d_row = jnp.tile(d_skip.astype(F32).reshape(nb, 1, lb), (1, 1, chunk))

    vec = pl.BlockSpec((1, 1, gn), lambda j: (j, 0, 0))
    mat = pl.BlockSpec((1, lb, gn), lambda j: (j, 0, 0))
    blk = pl.BlockSpec((1, n_rows, kdim), lambda j: (j, 0, 0))
    scratch = [
        pltpu.VMEM((kdim, 2 * gn), BF16),
        pltpu.VMEM((kdim, 2 * gn), BF16),
        pltpu.VMEM((kdim, kdim), BF16),
        pltpu.VMEM((chunk, lb, lb), BF16),
        pltpu.VMEM((n_rows, 2 * gn), F32),
        pltpu.VMEM((n_rows, 2 * gn), F32),
    ]
    scratch_bytes = 2 * kdim * 2 * gn * 2 + kdim * kdim * 2 + chunk * lb * lb * 2 + 2 * n_rows * 2 * gn * 4
    return pl.pallas_call(
        functools.partial(_s5_kernel, n_batch=n_batch, rows_per_batch=rows_per_batch, row_tile=row_tile),
        grid=(nb,),
        in_specs=[blk, vec, vec, vec, mat, mat, mat, mat, pl.BlockSpec((1, 1, kdim), lambda j: (j, 0, 0))],
        out_specs=blk,
        out_shape=jax.ShapeDtypeStruct((nb, n_rows, kdim), BF16),
        scratch_shapes=scratch,
        compiler_params=_params([n_rows * kdim * 2, n_rows * kdim * 2, 4 * lb * gn * 4], scratch_bytes=scratch_bytes,
                                temp_bytes=4 * row_tile * kdim * 4),
        name="s5_scan",
    )(u_blocks, lane_row(a_re), lane_row(a_im), ldt, bt_re, bt_im, ct_re, ct_im, d_row)


def _glu_kernel(y_ref, ycol_ref, w_ref, b_ref, z_ref, o_ref):
    y = jnp.concatenate([y_ref[k] for k in range(y_ref.shape[0])], axis=1)
    ycol = jnp.concatenate([ycol_ref[k] for k in range(ycol_ref.shape[0])], axis=1).astype(F32)
    gate = jax.nn.sigmoid(jnp.dot(y, w_ref[...], preferred_element_type=F32) + b_ref[...])
    o_ref[...] = (ycol * gate * jax.nn.silu(z_ref[...].astype(F32))).astype(o_ref.dtype)


def _glu(y_blocks, w_glu, b_glu, proj, z_col0, tm, tn):
    nb, n, lb = y_blocks.shape
    width = nb * lb
    tm, tn = _tile(n, tm), _tile(width, tn)
    return pl.pallas_call(
        _glu_kernel,
        grid=(n // tm, width // tn),
        in_specs=[
            pl.BlockSpec((nb, tm, lb), lambda i, j: (0, i, 0)),
            pl.BlockSpec((tn // lb, tm, lb), lambda i, j: (j, i, 0)),
            pl.BlockSpec((width, tn), lambda i, j: (0, j)),
            pl.BlockSpec((1, tn), lambda i, j: (0, j)),
            pl.BlockSpec((tm, tn), lambda i, j: (i, z_col0 // tn + j)),
        ],
        out_specs=pl.BlockSpec((tm, tn), lambda i, j: (i, j)),
        out_shape=jax.ShapeDtypeStruct((n, width), BF16),
        compiler_params=_params([tm * width * 2, tm * tn * 2, width * tn * 2, tm * tn * 2, tm * tn * 2],
                                temp_bytes=tm * width * 2 + 3 * tm * tn * 4, n_axes=2),
        name="s5_glu",
    )(y_blocks, y_blocks, w_glu, b_glu.reshape(1, width).astype(F32), proj)


def _memattn_kernel(q_ref, z_ref, kv_ref, qg_ref, kg_ref, o_ref, *, width):
    dm = width // MEM_HEADS
    for hd in range(MEM_HEADS):
        cols = slice(hd * dm, (hd + 1) * dm)
        q = q_ref[:, cols].astype(F32)
        qn = q * lax.rsqrt(jnp.mean(q * q, axis=-1, keepdims=True) + EPS) * qg_ref[...]
        k = kv_ref[:, cols].astype(F32)
        kn = k * lax.rsqrt(jnp.mean(k * k, axis=-1, keepdims=True) + EPS) * kg_ref[...]
        v = kv_ref[:, width + hd * dm:width + (hd + 1) * dm]
        s = lax.dot_general(qn.astype(BF16), kn.astype(BF16), (((1,), (1,)), ((), ())),
                            preferred_element_type=F32) * (dm ** -0.5)
        p = jnp.exp(s - jnp.max(s, axis=-1, keepdims=True))
        l = jnp.sum(p, axis=-1, keepdims=True)
        o = jnp.dot(p.astype(BF16), v, preferred_element_type=F32) / l
        o_ref[:, cols] = (o * jax.nn.silu(z_ref[:, cols].astype(F32))).astype(o_ref.dtype)


def _memattn(proj, kv, q_gain, k_gain, bsz, seq, width, tq):
    dm = width // MEM_HEADS
    n_mem = kv.shape[0] // bsz
    tq = _tile(seq, tq)
    nq = seq // tq
    return pl.pallas_call(
        functools.partial(_memattn_kernel, width=width),
        grid=(bsz, nq),
        in_specs=[
            pl.BlockSpec((tq, width), lambda b, i: (b * nq + i, 6)),
            pl.BlockSpec((tq, width), lambda b, i: (b * nq + i, 7)),
            pl.BlockSpec((n_mem, 2 * width), lambda b, i: (b, 0)),
            pl.BlockSpec((1, dm), lambda b, i: (0, 0)),
            pl.BlockSpec((1, dm), lambda b, i: (0, 0)),
        ],
        out_specs=pl.BlockSpec((tq, width), lambda b, i: (b * nq + i, 0)),
        out_shape=jax.ShapeDtypeStruct((bsz * seq, width), BF16),
        compiler_params=_params([3 * tq * width * 2, n_mem * 2 * width * 2], temp_bytes=8 * tq * dm * 4, n_axes=2),
        name="memory_attention",
    )(proj, proj, kv, q_gain.reshape(1, dm).astype(F32), k_gain.reshape(1, dm).astype(F32))


def _merge_kernel(ya_ref, ys_ref, yc_ref, wa_ref, ws_ref, wc_ref, ga_ref, gs_ref, gc_ref, o_ref):
    def term(y_ref, w_ref, g_ref):
        return jax.nn.sigmoid(g_ref[...].astype(F32)) * jnp.dot(y_ref[...], w_ref[...], preferred_element_type=F32)

    o_ref[...] = (term(ya_ref, wa_ref, ga_ref) + term(ys_ref, ws_ref, gs_ref)
                  + term(yc_ref, wc_ref, gc_ref)).astype(o_ref.dtype)


def _merge(y_a, y_s, y_c, w_a, w_s, w_c, proj, g_col0, tm, tn):
    n, width = y_a.shape
    d = w_a.shape[1]
    tm, tn = _tile(n, tm), _tile(d, tn)
    y_spec = pl.BlockSpec((tm, width), lambda i, j: (i, 0))
    w_spec = pl.BlockSpec((width, tn), lambda i, j: (0, j))

    def g_spec(branch):
        return pl.BlockSpec((tm, tn), lambda i, j: (i, (g_col0 + branch * d) // tn + j))

    return pl.pallas_call(
        _merge_kernel,
        grid=(n // tm, d // tn),
        in_specs=[y_spec] * 3 + [w_spec] * 3 + [g_spec(0), g_spec(1), g_spec(2)],
        out_specs=pl.BlockSpec((tm, tn), lambda i, j: (i, j)),
        out_shape=jax.ShapeDtypeStruct((n, d), BF16),
        compiler_params=_params([3 * tm * width * 2, 3 * width * tn * 2, 4 * tm * tn * 2], temp_bytes=4 * tm * tn * 4, n_axes=2),
        name="branch_merge",
    )(y_a, y_s, y_c, w_a, w_s, w_c, proj, proj, proj)


def _outproj_kernel(m_ref, w_ref, x_ref, o_ref):
    o_ref[...] = x_ref[...] + jnp.dot(m_ref[...], w_ref[...], preferred_element_type=F32)


def _outproj(merged, w_out, x, tm, tn):
    n, d = merged.shape
    tm, tn = _tile(n, tm), _tile(d, tn)
    return pl.pallas_call(
        _outproj_kernel,
        grid=(n // tm, d // tn),
        in_specs=[
            pl.BlockSpec((tm, d), lambda i, j: (i, 0)),
            pl.BlockSpec((d, tn), lambda i, j: (0, j)),
            pl.BlockSpec((tm, tn), lambda i, j: (i, j)),
        ],
        out_specs=pl.BlockSpec((tm, tn), lambda i, j: (i, j)),
        out_shape=jax.ShapeDtypeStruct((n, d), F32),
        compiler_params=_params([tm * d * 2, d * tn * 2, 2 * tm * tn * 4], temp_bytes=tm * tn * 4, n_axes=2),
        name="out_projection",
    )(merged, w_out, x)


def kernel(x, mem, w_in, g_norm, g_mem, w_mem_kv, q_gain_a, k_gain_a, q_gain_c, k_gain_c, ssm_a_re, ssm_a_im, ssm_log_dt, ssm_b_re, ssm_b_im, ssm_c_re, ssm_c_im, ssm_d, w_glu, b_glu, w_br_a, w_br_s, w_br_c, w_out):
    bsz, seq, d_model = x.shape
    width = w_glu.shape[0]
    n_tok = bsz * seq
    n_mem = mem.shape[1]
    assert seq % MOBA_BLOCK == 0 and seq % SSM_CHUNK == 0 and width % V7X_LANES == 0
    assert w_in.shape == (d_model, 8 * width + 3 * d_model)

    x2 = x.reshape(n_tok, d_model)
    h = _rmsnorm(x2, g_norm, rows=256)
    proj = _matmul(h, w_in.astype(BF16), 1024, 1024, "in_projection")

    m = _rmsnorm(mem.reshape(bsz * n_mem, d_model), g_mem, rows=256)
    kv = _matmul(m, w_mem_kv.astype(BF16), 512, 1024, "memory_kv_projection")

    y_a = _moba(proj, q_gain_a, k_gain_a, bsz, seq, width)

    nb = width // V7X_LANES
    n_chunks = n_tok // SSM_CHUNK
    u = proj[:, 4 * width:5 * width].reshape(n_chunks, SSM_CHUNK, nb, V7X_LANES)
    u = u.transpose(2, 0, 1, 3).reshape(nb, n_chunks, SSM_CHUNK * V7X_LANES)
    y_g = _s5(u, ssm_a_re, ssm_a_im, ssm_log_dt, ssm_b_re, ssm_b_im, ssm_c_re, ssm_c_im, ssm_d, bsz)
    y_s = _glu(y_g.reshape(nb, n_tok, V7X_LANES), w_glu.astype(BF16), b_glu, proj, 5 * width, 512, 512)

    y_c = _memattn(proj, kv, q_gain_c, k_gain_c, bsz, seq, width, 512)

    merged = _merge(y_a, y_s, y_c, w_br_a.astype(BF16), w_br_s.astype(BF16), w_br_c.astype(BF16), proj,
                    8 * width, 512, 512)
    out = _outproj(merged, w_out.astype(BF16), x2, 1024, 512)
    return out.reshape(bsz, seq, d_model)
```

```python
import functools

import jax
import jax.numpy as jnp
from jax import lax
from jax.experimental import pallas as pl
from jax.experimental.pallas import tpu as pltpu

F32 = jnp.float32
BF16 = jnp.bfloat16

V7X_LANES = 128
V7X_VMEM_BYTES = 64 * 1024 * 1024
V7X_VMEM_RESERVE_BYTES = 6 * 1024 * 1024

ATTN_HEAD_DIM = 128
MOBA_BLOCK = 256
MOBA_TOP_K = 3
SSM_GROUP = 16
SSM_STATE = 64
MEM_HEADS = 4
EPS = 1e-6
NEG = -1e30
LOG2E = 1.4426950408889634

MOBA_GROUP = 2
MOBA_QUERY_STRIP = 256
SSM_CHUNK = 16
GROUPS_PER_LANE_BLOCK = V7X_LANES // SSM_GROUP
STATE_LANES = GROUPS_PER_LANE_BLOCK * SSM_STATE


def _tile(n, pref):
    t = min(n, pref)
    while n % t:
        t -= V7X_LANES
    assert t > 0
    return t


def _params(block_bytes, scratch_bytes=0, temp_bytes=0, n_axes=1):
    need = 2 * sum(block_bytes) + scratch_bytes + temp_bytes
    limit = min(max(need, 16 * 1024 * 1024), V7X_VMEM_BYTES - V7X_VMEM_RESERVE_BYTES)
    return pltpu.CompilerParams(dimension_semantics=("arbitrary",) * n_axes, vmem_limit_bytes=int(limit))


def _rmsnorm_kernel(x_ref, g_ref, o_ref):
    x = x_ref[...].astype(F32)
    ms = jnp.mean(x * x, axis=-1, keepdims=True)
    o_ref[...] = (x * lax.rsqrt(ms + EPS) * g_ref[...]).astype(o_ref.dtype)


def _rmsnorm(x, gain, rows):
    n, d = x.shape
    tm = _tile(n, rows)
    return pl.pallas_call(
        _rmsnorm_kernel,
        grid=(n // tm,),
        in_specs=[pl.BlockSpec((tm, d), lambda i: (i, 0)), pl.BlockSpec((1, d), lambda i: (0, 0))],
        out_specs=pl.BlockSpec((tm, d), lambda i: (i, 0)),
        out_shape=jax.ShapeDtypeStruct((n, d), BF16),
        compiler_params=_params([tm * d * 4, tm * d * 2], temp_bytes=2 * tm * d * 4),
        name="rmsnorm",
    )(x, gain.reshape(1, d).astype(F32))


def _mm_kernel(a_ref, b_ref, o_ref):
    o_ref[...] = jnp.dot(a_ref[...], b_ref[...], preferred_element_type=F32).astype(o_ref.dtype)


def _matmul(a, b, tm, tn, name):
    m, k = a.shape
    _, n = b.shape
    tm, tn = _tile(m, tm), _tile(n, tn)
    return pl.pallas_call(
        _mm_kernel,
        grid=(m // tm, n // tn),
        in_specs=[pl.BlockSpec((tm, k), lambda i, j: (i, 0)), pl.BlockSpec((k, tn), lambda i, j: (0, j))],
        out_specs=pl.BlockSpec((tm, tn), lambda i, j: (i, j)),
        out_shape=jax.ShapeDtypeStruct((m, n), BF16),
        compiler_params=_params([tm * k * 2, k * tn * 2, tm * tn * 2], temp_bytes=tm * tn * (4 + 4 + 2), n_axes=2),
        name=name,
    )(a, b)


def _moba_kernel(q_ref, k_ref, v_ref, z_ref, qg_ref, kg_ref, slope_ref, o_ref,
                 kn_ref, vt_ref, kmean_ref, add_ref, s_ref, acc_ref, *, n_blocks, group):
    it = pl.program_id(2)
    bs, hd = MOBA_BLOCK, ATTN_HEAD_DIM
    qt = group * bs
    strip = min(qt, MOBA_QUERY_STRIP)
    slope2 = slope_ref[0][:, :1] * LOG2E

    @pl.when(it == 0)
    def _():
        key_off = lax.broadcasted_iota(jnp.int32, (bs, hd), 0).astype(F32)
        lane = lax.broadcasted_iota(jnp.int32, (bs, hd), 1)
        k_aug = jnp.where(lane < 3, key_off, 0.0).astype(BF16)

        def prep(g, carry):
            for bi in range(group):
                c = g * group + bi
                rows = pl.ds(pl.multiple_of(c * bs, bs), bs)
                kb = k_ref[rows, :].astype(F32)
                ms = jnp.mean(kb * kb, axis=-1, keepdims=True)
                kn = kb * lax.rsqrt(ms + EPS) * kg_ref[...]
                kn_ref[g, bi * bs:(bi + 1) * bs, :] = jnp.concatenate([kn.astype(BF16), k_aug], axis=1)
                kmean_ref[pl.ds(c, 1), :] = jnp.mean(kn, axis=0, keepdims=True)
                vt_ref[g, :, bi * bs:(bi + 1) * bs] = v_ref[rows, :].astype(F32).T.astype(BF16)
            return carry

        lax.fori_loop(0, n_blocks // group, prep, 0)

    q = q_ref[...].astype(F32)
    ms = jnp.mean(q * q, axis=-1, keepdims=True)
    qn_t = (q * lax.rsqrt(ms + EPS) * qg_ref[...]).T
    s_hi = slope2.astype(BF16).astype(F32)
    s_mid = (slope2 - s_hi).astype(BF16).astype(F32)
    s_lo = slope2 - s_hi - s_mid
    row = lax.broadcasted_iota(jnp.int32, (hd, qt), 0)
    q_aug = jnp.where(row == 0, s_hi, jnp.where(row == 1, s_mid, jnp.where(row == 2, s_lo, 0.0)))
    q_aug = jnp.concatenate([(qn_t * (hd ** -0.5 * LOG2E)).astype(BF16), q_aug.astype(BF16)], axis=0)

    gate = jnp.dot(kmean_ref[...], qn_t, preferred_element_type=F32, precision=lax.Precision.HIGHEST)
    blk = lax.broadcasted_iota(jnp.int32, gate.shape, 0)
    own = it * group + lax.broadcasted_iota(jnp.int32, gate.shape, 1) // bs
    blk_f = blk.astype(F32)
    past = blk < own
    g = jnp.where(past, gate, NEG)
    sel = jnp.zeros(gate.shape, jnp.bool_)
    for _ in range(min(MOBA_TOP_K, n_blocks)):
        top = jnp.max(g, axis=0, keepdims=True)
        first = jnp.min(jnp.where(g == top, blk_f, float(n_blocks)), axis=0, keepdims=True)
        pick = blk_f == first
        sel = jnp.logical_or(sel, pick)
        g = jnp.where(pick, -jnp.inf, g)
    sel = jnp.logical_and(sel, past)
    add_ref[...] = jnp.where(sel, slope2 * (bs * (blk - own)).astype(F32), jnp.where(blk == own, 0.0, NEG))

    def scores_to(slot, gi, causal=False, penalty=None):
        s = jnp.dot(kn_ref[gi], q_aug, preferred_element_type=F32)
        top = None
        for bi in range(group):
            part = s[bi * bs:(bi + 1) * bs] + add_ref[pl.ds(gi * group + bi, 1), :]
            if penalty is not None:
                part = part + penalty
            if causal:
                key_rel = bi * bs + lax.broadcasted_iota(jnp.int32, (bs, qt), 0)
                part = jnp.where(key_rel <= lax.broadcasted_iota(jnp.int32, (bs, qt), 1), part, NEG)
            s_ref[slot, bi * bs:(bi + 1) * bs, :] = part
            top = part if top is None else jnp.maximum(top, part)
        return jnp.max(top, axis=0, keepdims=True)

    def accumulate(slot, gi, m, l, top):
        m_new = jnp.maximum(m, top)
        alpha = jnp.exp2(m - m_new)
        sums = []
        for c0 in range(0, qt, strip):
            cols = slice(c0, c0 + strip)
            p = jnp.exp2(s_ref[slot, :, cols] - m_new[:, cols])
            sums.append(jnp.sum(p, axis=0, keepdims=True))
            pv = jnp.dot(vt_ref[gi], p.astype(BF16), preferred_element_type=F32)
            acc_ref[:, cols] = alpha[:, cols] * acc_ref[:, cols] + pv
        return m_new, alpha * l + jnp.concatenate(sums, axis=1)

    def pair(k, carry):
        m, l = carry
        top0 = scores_to(0, 2 * k)
        top1 = scores_to(1, 2 * k + 1)
        m, l = accumulate(0, 2 * k, m, l, top0)
        return accumulate(1, 2 * k + 1, m, l, top1)

    acc_ref[...] = jnp.zeros(acc_ref.shape, F32)
    m, l = lax.fori_loop(0, it // 2, pair, (jnp.full((1, qt), 0.1 * NEG, F32), jnp.zeros((1, qt), F32)))
    odd = it % 2
    tail = it - odd
    top0 = scores_to(0, tail, penalty=jnp.where(odd == 1, 0.0, NEG).astype(F32))
    top1 = scores_to(1, it, causal=True)
    m, l = accumulate(0, tail, m, l, top0)
    _, l = accumulate(1, it, m, l, top1)
    o = (acc_ref[...] / l).T
    o_ref[...] = (o * jax.nn.silu(z_ref[...].astype(F32))).astype(o_ref.dtype)


def _moba(proj, q_gain, k_gain, bsz, seq, width):
    n_heads = width // ATTN_HEAD_DIM
    n_blocks = seq // MOBA_BLOCK
    hd, bs, group = ATTN_HEAD_DIM, MOBA_BLOCK, MOBA_GROUP
    assert n_blocks % group == 0
    n_groups, qt = n_blocks // group, group * bs
    slopes = jnp.asarray([[[2.0 ** (-8.0 * (h + 1) / n_heads)] * V7X_LANES] for h in range(n_heads)], F32)
    scratch = [
        pltpu.VMEM((n_blocks // group, group * bs, 2 * hd), BF16),
        pltpu.VMEM((n_blocks // group, hd, group * bs), BF16),
        pltpu.VMEM((n_blocks, hd), F32),
        pltpu.VMEM((n_blocks, qt), F32),
        pltpu.VMEM((2, qt, qt), F32),
        pltpu.VMEM((hd, qt), F32),
    ]
    return pl.pallas_call(
        functools.partial(_moba_kernel, n_blocks=n_blocks, group=group),
        grid=(bsz, n_heads, n_groups),
        in_specs=[
            pl.BlockSpec((qt, hd), lambda b, h, i: (b * n_groups + i, h)),
            pl.BlockSpec((seq, hd), lambda b, h, i: (b, n_heads + h)),
            pl.BlockSpec((seq, hd), lambda b, h, i: (b, 2 * n_heads + h)),
            pl.BlockSpec((qt, hd), lambda b, h, i: (b * n_groups + i, 3 * n_heads + h)),
            pl.BlockSpec((1, hd), lambda b, h, i: (0, 0)),
            pl.BlockSpec((1, hd), lambda b, h, i: (0, 0)),
            pl.BlockSpec((1, 1, V7X_LANES), lambda b, h, i: (h, 0, 0)),
        ],
        out_specs=pl.BlockSpec((qt, hd), lambda b, h, i: (b * n_groups + i, h)),
        out_shape=jax.ShapeDtypeStruct((bsz * seq, width), BF16),
        scratch_shapes=scratch,
        compiler_params=_params([2 * seq * hd * 2, 3 * qt * hd * 2], scratch_bytes=3 * seq * hd * 2 + (n_blocks * (hd + qt) + 2 * qt * qt + hd * qt) * 4,
                                temp_bytes=8 * qt * qt * 4, n_axes=3),
        name="moba_attention",
    )(proj, proj, proj, proj, q_gain.reshape(1, hd).astype(F32), k_gain.reshape(1, hd).astype(F32), slopes)


def _s5_kernel(u_ref, ar_ref, ai_ref, ldt_ref, bbr_ref, bbi_ref, ccr_ref, cci_ref, d_ref, y_ref,
               p_ref, qt_ref, t_ref, dk_ref, sinc_ref, xs_ref, *, n_batch, rows_per_batch, row_tile):
    lb, gn, chunk = V7X_LANES, STATE_LANES, SSM_CHUNK
    ar, ai = ar_ref[0], ai_ref[0]
    dt = jnp.exp(ldt_ref[0])
    mag = jnp.exp(dt * ar)
    abr = mag * jnp.cos(dt * ai)
    abi = mag * jnp.sin(dt * ai)
    den = ar * ar + ai * ai
    nr = abr - 1.0
    f_re = (nr * ar + abi * ai) / den
    f_im = (abi * ar - nr * ai) / den

    row_g = lax.broadcasted_iota(jnp.int32, (lb, gn), 0) // SSM_GROUP
    col_g = lax.broadcasted_iota(jnp.int32, (lb, gn), 1) // SSM_STATE
    same = row_g == col_g
    bb_re, bb_im = bbr_ref[0], bbi_ref[0]
    bbar_re = jnp.where(same, f_re * bb_re - f_im * bb_im, 0.0)
    bbar_im = jnp.where(same, f_re * bb_im + f_im * bb_re, 0.0)
    cc_re = jnp.where(same, ccr_ref[0], 0.0)
    cc_im = jnp.where(same, cci_ref[0], 0.0)
    ccs = jnp.concatenate([cc_re, -cc_im], axis=1)

    pr = jnp.ones((1, gn), F32)
    pi = jnp.zeros((1, gn), F32)
    for tau in range(chunk + 1):
        if tau < chunk:
            ptau = jnp.concatenate([bbar_re * pr - bbar_im * pi, bbar_re * pi + bbar_im * pr], axis=1)
            s = chunk - 1 - tau
            p_ref[s * lb:(s + 1) * lb, :] = ptau.astype(BF16)
            dk_ref[tau] = lax.dot_general(ptau, ccs, (((1,), (1,)), ((), ())), preferred_element_type=F32,
                                          precision=lax.Precision.HIGHEST).astype(BF16)
        if tau >= 1:
            t = tau - 1
            qt_ref[t * lb:(t + 1) * lb, :] = jnp.concatenate(
                [cc_re * pr - cc_im * pi, -(cc_re * pi + cc_im * pr)], axis=1).astype(BF16)
        if tau < chunk:
            pr, pi = pr * abr - pi * abi, pr * abi + pi * abr
    al_re, al_im = pr, pi

    zero = jnp.zeros((lb, lb), BF16)
    for s in range(chunk):
        for t in range(chunk):
            t_ref[s * lb:(s + 1) * lb, t * lb:(t + 1) * lb] = dk_ref[t - s] if t >= s else zero

    n_rows = n_batch * rows_per_batch
    for r0 in range(0, n_rows, row_tile):
        sinc_ref[r0:r0 + row_tile, :] = jnp.dot(u_ref[0, r0:r0 + row_tile, :], p_ref[...], preferred_element_type=F32)

    def step(c, carry):
        out = []
        for b in range(n_batch):
            xr, xi = carry[b]
            row = pl.ds(b * rows_per_batch + c, 1)
            xs_ref[row, :gn] = xr
            xs_ref[row, gn:] = xi
            inc = sinc_ref[row, :]
            out.append((al_re * xr - al_im * xi + inc[:, :gn], al_re * xi + al_im * xr + inc[:, gn:]))
        return tuple(out)

    x0 = jnp.zeros((1, gn), F32)
    lax.fori_loop(0, rows_per_batch, step, tuple((x0, x0) for _ in range(n_batch)))

    for r0 in range(0, n_rows, row_tile):
        u = u_ref[0, r0:r0 + row_tile, :]
        y = jnp.dot(u, t_ref[...], preferred_element_type=F32)
        y = y + lax.dot_general(xs_ref[r0:r0 + row_tile, :].astype(BF16), qt_ref[...], (((1,), (1,)), ((), ())),
                                preferred_element_type=F32)
        y = y + d_ref[0] * u.astype(F32)
        y_ref[0, r0:r0 + row_tile, :] = jax.nn.gelu(y).astype(y_ref.dtype)


def _s5(u_blocks, a_re, a_im, log_dt, b_re, b_im, c_re, c_im, d_skip, n_batch):
    nb, n_rows, kdim = u_blocks.shape
    gpb, lb, gn, chunk = GROUPS_PER_LANE_BLOCK, V7X_LANES, STATE_LANES, SSM_CHUNK
    rows_per_batch = n_rows // n_batch
    row_tile = _tile(n_rows, 256)

    def lane_row(v):
        return v.astype(F32).reshape(nb, 1, gn)

    ldt = jnp.repeat(log_dt.astype(F32), SSM_STATE).reshape(nb, 1, gn)
    bt_re = jnp.tile(b_re.astype(F32).reshape(nb, gpb, SSM_STATE, SSM_GROUP).transpose(0, 3, 1, 2).reshape(nb, SSM_GROUP, gn), (1, gpb, 1))
    bt_im = jnp.tile(b_im.astype(F32).reshape(nb, gpb, SSM_STATE, SSM_GROUP).transpose(0, 3, 1, 2).reshape(nb, SSM_GROUP, gn), (1, gpb, 1))
    ct_re = jnp.tile(c_re.astype(F32).reshape(nb, lb, SSM_STATE), (1, 1, gpb))
    ct_im = jnp.tile(c_im.astype(F32).reshape(nb, lb, SSM_STATE), (1, 1, gpb))
    d_row = jnp.tile(d_skip.astype(F32).reshape(nb, 1, lb), (1, 1, chunk))

    vec = pl.BlockSpec((1, 1, gn), lambda j: (j, 0, 0))
    mat = pl.BlockSpec((1, lb, gn), lambda j: (j, 0, 0))
    blk = pl.BlockSpec((1, n_rows, kdim), lambda j: (j, 0, 0))
    scratch = [
        pltpu.VMEM((kdim, 2 * gn), BF16),
        pltpu.VMEM((kdim, 2 * gn), BF16),
        pltpu.VMEM((kdim, kdim), BF16),
        pltpu.VMEM((chunk, lb, lb), BF16),
        pltpu.VMEM((n_rows, 2 * gn), F32),
        pltpu.VMEM((n_rows, 2 * gn), F32),
    ]
    scratch_bytes = 2 * kdim * 2 * gn * 2 + kdim * kdim * 2 + chunk * lb * lb * 2 + 2 * n_rows * 2 * gn * 4
    return pl.pallas_call(
        functools.partial(_s5_kernel, n_batch=n_batch, rows_per_batch=rows_per_batch, row_tile=row_tile),
        grid=(nb,),
        in_specs=[blk, vec, vec, vec, mat, mat, mat, mat, pl.BlockSpec((1, 1, kdim), lambda j: (j, 0, 0))],
        out_specs=blk,
        out_shape=jax.ShapeDtypeStruct((nb, n_rows, kdim), BF16),
        scratch_shapes=scratch,
        compiler_params=_params([n_rows * kdim * 2, n_rows * kdim * 2, 4 * lb * gn * 4], scratch_bytes=scratch_bytes,
                                temp_bytes=4 * row_tile * kdim * 4),
        name="s5_scan",
    )(u_blocks, lane_row(a_re), lane_row(a_im), ldt, bt_re, bt_im, ct_re, ct_im, d_row)


def _glu_kernel(y_ref, ycol_ref, w_ref, b_ref, z_ref, o_ref):
    y = jnp.concatenate([y_ref[k] for k in range(y_ref.shape[0])], axis=1)
    ycol = jnp.concatenate([ycol_ref[k] for k in range(ycol_ref.shape[0])], axis=1).astype(F32)
    gate = jax.nn.sigmoid(jnp.dot(y, w_ref[...], preferred_element_type=F32) + b_ref[...])
    o_ref[...] = (ycol * gate * jax.nn.silu(z_ref[...].astype(F32))).astype(o_ref.dtype)


def _glu(y_blocks, w_glu, b_glu, proj, z_col0, tm, tn):
    nb, n, lb = y_blocks.shape
    width = nb * lb
    tm, tn = _tile(n, tm), _tile(width, tn)
    return pl.pallas_call(
        _glu_kernel,
        grid=(n // tm, width // tn),
        in_specs=[
            pl.BlockSpec((nb, tm, lb), lambda i, j: (0, i, 0)),
            pl.BlockSpec((tn // lb, tm, lb), lambda i, j: (j, i, 0)),
            pl.BlockSpec((width, tn), lambda i, j: (0, j)),
            pl.BlockSpec((1, tn), lambda i, j: (0, j)),
            pl.BlockSpec((tm, tn), lambda i, j: (i, z_col0 // tn + j)),
        ],
        out_specs=pl.BlockSpec((tm, tn), lambda i, j: (i, j)),
        out_shape=jax.ShapeDtypeStruct((n, width), BF16),
        compiler_params=_params([tm * width * 2, tm * tn * 2, width * tn * 2, tm * tn * 2, tm * tn * 2],
                                temp_bytes=tm * width * 2 + 3 * tm * tn * 4, n_axes=2),
        name="s5_glu",
    )(y_blocks, y_blocks, w_glu, b_glu.reshape(1, width).astype(F32), proj)


def _memattn_kernel(q_ref, z_ref, kv_ref, qg_ref, kg_ref, o_ref, *, width):
    dm = width // MEM_HEADS
    for hd in range(MEM_HEADS):
        cols = slice(hd * dm, (hd + 1) * dm)
        q = q_ref[:, cols].astype(F32)
        qn = q * lax.rsqrt(jnp.mean(q * q, axis=-1, keepdims=True) + EPS) * qg_ref[...]
        k = kv_ref[:, cols].astype(F32)
        kn = k * lax.rsqrt(jnp.mean(k * k, axis=-1, keepdims=True) + EPS) * kg_ref[...]
        v = kv_ref[:, width + hd * dm:width + (hd + 1) * dm]
        s = lax.dot_general(qn.astype(BF16), kn.astype(BF16), (((1,), (1,)), ((), ())),
                            preferred_element_type=F32) * (dm ** -0.5)
        p = jnp.exp(s - jnp.max(s, axis=-1, keepdims=True))
        l = jnp.sum(p, axis=-1, keepdims=True)
        o = jnp.dot(p.astype(BF16), v, preferred_element_type=F32) / l
        o_ref[:, cols] = (o * jax.nn.silu(z_ref[:, cols].astype(F32))).astype(o_ref.dtype)


def _memattn(proj, kv, q_gain, k_gain, bsz, seq, width, tq):
    dm = width // MEM_HEADS
    n_mem = kv.shape[0] // bsz
    tq = _tile(seq, tq)
    nq = seq // tq
    return pl.pallas_call(
        functools.partial(_memattn_kernel, width=width),
        grid=(bsz, nq),
        in_specs=[
            pl.BlockSpec((tq, width), lambda b, i: (b * nq + i, 6)),
            pl.BlockSpec((tq, width), lambda b, i: (b * nq + i, 7)),
            pl.BlockSpec((n_mem, 2 * width), lambda b, i: (b, 0)),
            pl.BlockSpec((1, dm), lambda b, i: (0, 0)),
            pl.BlockSpec((1, dm), lambda b, i: (0, 0)),
        ],
        out_specs=pl.BlockSpec((tq, width), lambda b, i: (b * nq + i, 0)),
        out_shape=jax.ShapeDtypeStruct((bsz * seq, width), BF16),
        compiler_params=_params([3 * tq * width * 2, n_mem * 2 * width * 2], temp_bytes=8 * tq * dm * 4, n_axes=2),
        name="memory_attention",
    )(proj, proj, kv, q_gain.reshape(1, dm).astype(F32), k_gain.reshape(1, dm).astype(F32))


def _merge_kernel(ya_ref, ys_ref, yc_ref, wa_ref, ws_ref, wc_ref, ga_ref, gs_ref, gc_ref, o_ref):
    def term(y_ref, w_ref, g_ref):
        return jax.nn.sigmoid(g_ref[...].astype(F32)) * jnp.dot(y_ref[...], w_ref[...], preferred_element_type=F32)

    o_ref[...] = (term(ya_ref, wa_ref, ga_ref) + term(ys_ref, ws_ref, gs_ref)
                  + term(yc_ref, wc_ref, gc_ref)).astype(o_ref.dtype)


def _merge(y_a, y_s, y_c, w_a, w_s, w_c, proj, g_col0, tm, tn):
    n, width = y_a.shape
    d = w_a.shape[1]
    tm, tn = _tile(n, tm), _tile(d, tn)
    y_spec = pl.BlockSpec((tm, width), lambda i, j: (i, 0))
    w_spec = pl.BlockSpec((width, tn), lambda i, j: (0, j))

    def g_spec(branch):
        return pl.BlockSpec((tm, tn), lambda i, j: (i, (g_col0 + branch * d) // tn + j))

    return pl.pallas_call(
        _merge_kernel,
        grid=(n // tm, d // tn),
        in_specs=[y_spec] * 3 + [w_spec] * 3 + [g_spec(0), g_spec(1), g_spec(2)],
        out_specs=pl.BlockSpec((tm, tn), lambda i, j: (i, j)),
        out_shape=jax.ShapeDtypeStruct((n, d), BF16),
        compiler_params=_params([3 * tm * width * 2, 3 * width * tn * 2, 4 * tm * tn * 2], temp_bytes=4 * tm * tn * 4, n_axes=2),
        name="branch_merge",
    )(y_a, y_s, y_c, w_a, w_s, w_c, proj, proj, proj)


def _outproj_kernel(m_ref, w_ref, x_ref, o_ref):
    o_ref[...] = x_ref[...] + jnp.dot(m_ref[...], w_ref[...], preferred_element_type=F32)


def _outproj(merged, w_out, x, tm, tn):
    n, d = merged.shape
    tm, tn = _tile(n, tm), _tile(d, tn)
    return pl.pallas_call(
        _outproj_kernel,
        grid=(n // tm, d // tn),
        in_specs=[
            pl.BlockSpec((tm, d), lambda i, j: (i, 0)),
            pl.BlockSpec((d, tn), lambda i, j: (0, j)),
            pl.BlockSpec((tm, tn), lambda i, j: (i, j)),
        ],
        out_specs=pl.BlockSpec((tm, tn), lambda i, j: (i, j)),
        out_shape=jax.ShapeDtypeStruct((n, d), F32),
        compiler_params=_params([tm * d * 2, d * tn * 2, 2 * tm * tn * 4], temp_bytes=tm * tn * 4, n_axes=2),
        name="out_projection",
    )(merged, w_out, x)


def kernel(x, mem, w_in, g_norm, g_mem, w_mem_kv, q_gain_a, k_gain_a, q_gain_c, k_gain_c, ssm_a_re, ssm_a_im, ssm_log_dt, ssm_b_re, ssm_b_im, ssm_c_re, ssm_c_im, ssm_d, w_glu, b_glu, w_br_a, w_br_s, w_br_c, w_out):
    bsz, seq, d_model = x.shape
    width = w_glu.shape[0]
    n_tok = bsz * seq
    n_mem = mem.shape[1]
    assert seq % MOBA_BLOCK == 0 and seq % SSM_CHUNK == 0 and width % V7X_LANES == 0
    assert w_in.shape == (d_model, 8 * width + 3 * d_model)

    x2 = x.reshape(n_tok, d_model)
    h = _rmsnorm(x2, g_norm, rows=256)
    proj = _matmul(h, w_in.astype(BF16), 1024, 1024, "in_projection")

    m = _rmsnorm(mem.reshape(bsz * n_mem, d_model), g_mem, rows=256)
    kv = _matmul(m, w_mem_kv.astype(BF16), 512, 1024, "memory_kv_projection")

    y_a = _moba(proj, q_gain_a, k_gain_a, bsz, seq, width)

    nb = width // V7X_LANES
    n_chunks = n_tok // SSM_CHUNK
    u = proj[:, 4 * width:5 * width].reshape(n_chunks, SSM_CHUNK, nb, V7X_LANES)
    u = u.transpose(2, 0, 1, 3).reshape(nb, n_chunks, SSM_CHUNK * V7X_LANES)
    y_g = _s5(u, ssm_a_re, ssm_a_im, ssm_log_dt, ssm_b_re, ssm_b_im, ssm_c_re, ssm_c_im, ssm_d, bsz)
    y_s = _glu(y_g.reshape(nb, n_tok, V7X_LANES), w_glu.astype(BF16), b_glu, proj, 5 * width, 512, 512)

    y_c = _memattn(proj, kv, q_gain_c, k_gain_c, bsz, seq, width, 512)

    merged = _merge(y_a, y_s, y_c, w_br_a.astype(BF16), w_br_s.astype(BF16), w_br_c.astype(BF16), proj,
                    8 * width, 512, 512)
    out = _outproj(merged, w_out.astype(BF16), x2, 1024, 512)
    return out.reshape(bsz, seq, d_model)
```

```python
import functools

import jax
import jax.numpy as jnp
from jax import lax
from jax.experimental import pallas as pl
from jax.experimental.pallas import tpu as pltpu

F32 = jnp.float32
BF16 = jnp.bfloat16

V7X_LANES = 128
V7X_BF16_SUBLANES = 16
V7X_VMEM_BYTES = 64 * 1024 * 1024
V7X_VMEM_RESERVE_BYTES = 6 * 1024 * 1024

ATTN_HEAD_DIM = 128
MOBA_BLOCK = 256
MOBA_TOP_K = 3
SSM_GROUP = 16
SSM_STATE = 64
MEM_HEADS = 4
EPS = 1e-6
NEG = -1e30
LOG2E = 1.4426950408889634

MOBA_KEY_GROUP = 2
MOBA_QUERY_GROUP = 4
MOBA_QUERY_STRIP = 256
SSM_CHUNK = 16
GROUPS_PER_LANE_BLOCK = V7X_LANES // SSM_GROUP
STATE_LANES = GROUPS_PER_LANE_BLOCK * SSM_STATE


def _tile(n, pref):
    t = min(n, pref)
    while n % t:
        t -= V7X_LANES
    assert t > 0
    return t


def _params(block_bytes, scratch_bytes=0, temp_bytes=0, n_axes=1):
    need = 2 * sum(block_bytes) + scratch_bytes + temp_bytes
    limit = min(max(need, 16 * 1024 * 1024), V7X_VMEM_BYTES - V7X_VMEM_RESERVE_BYTES)
    return pltpu.CompilerParams(dimension_semantics=("arbitrary",) * n_axes, vmem_limit_bytes=int(limit))


def _rmsnorm_kernel(x_ref, g_ref, o_ref):
    x = x_ref[...].astype(F32)
    ms = jnp.mean(x * x, axis=-1, keepdims=True)
    o_ref[...] = (x * lax.rsqrt(ms + EPS) * g_ref[...]).astype(o_ref.dtype)


def _rmsnorm(x, gain, rows):
    n, d = x.shape
    tm = _tile(n, rows)
    return pl.pallas_call(
        _rmsnorm_kernel,
        grid=(n // tm,),
        in_specs=[pl.BlockSpec((tm, d), lambda i: (i, 0)), pl.BlockSpec((1, d), lambda i: (0, 0))],
        out_specs=pl.BlockSpec((tm, d), lambda i: (i, 0)),
        out_shape=jax.ShapeDtypeStruct((n, d), BF16),
        compiler_params=_params([tm * d * 4, tm * d * 2], temp_bytes=2 * tm * d * 4),
        name="rmsnorm",
    )(x, gain.reshape(1, d).astype(F32))


def _mm_kernel(a_ref, b_ref, o_ref):
    o_ref[...] = jnp.dot(a_ref[...], b_ref[...], preferred_element_type=F32).astype(o_ref.dtype)


def _mm_lane_blocks_kernel(a_ref, b_ref, o_ref, blk_ref, *, j0, nj):
    res = jnp.dot(a_ref[...], b_ref[...], preferred_element_type=F32).astype(o_ref.dtype)
    o_ref[...] = res
    j = pl.program_id(1)

    @pl.when(jnp.logical_and(j >= j0, j < j0 + nj))
    def _():
        for k in range(blk_ref.shape[0]):
            blk_ref[k] = res[:, k * V7X_LANES:(k + 1) * V7X_LANES]


def _matmul(a, b, tm, tn, name, lane_block_cols=None):
    m, k = a.shape
    _, n = b.shape
    tm, tn = _tile(m, tm), _tile(n, tn)
    in_specs = [pl.BlockSpec((tm, k), lambda i, j: (i, 0)), pl.BlockSpec((k, tn), lambda i, j: (0, j))]
    out_spec = pl.BlockSpec((tm, tn), lambda i, j: (i, j))
    out_shape = jax.ShapeDtypeStruct((m, n), BF16)
    params = _params([tm * k * 2, k * tn * 2, 2 * tm * tn * 2], temp_bytes=tm * tn * (4 + 4 + 2), n_axes=2)
    if lane_block_cols is None:
        return pl.pallas_call(_mm_kernel, grid=(m // tm, n // tn), in_specs=in_specs, out_specs=out_spec,
                              out_shape=out_shape, compiler_params=params, name=name)(a, b)
    col0, ncols = lane_block_cols
    assert col0 % tn == 0 and ncols % tn == 0
    j0, nj, per = col0 // tn, ncols // tn, tn // V7X_LANES
    blk_spec = pl.BlockSpec((per, tm, V7X_LANES), lambda i, j: (jnp.clip(j - j0, 0, nj - 1), i, 0))
    return pl.pallas_call(
        functools.partial(_mm_lane_blocks_kernel, j0=j0, nj=nj),
        grid=(m // tm, n // tn), in_specs=in_specs, out_specs=[out_spec, blk_spec],
        out_shape=[out_shape, jax.ShapeDtypeStruct((ncols // V7X_LANES, m, V7X_LANES), BF16)],
        compiler_params=params, name=name)(a, b)


MOBA_ADD_ROW = ATTN_HEAD_DIM + V7X_BF16_SUBLANES
MOBA_PIECES = 3


def _split_bf16(x):
    hi = x.astype(BF16).astype(F32)
    mid = (x - hi).astype(BF16).astype(F32)
    return hi, mid, x - hi - mid


def _moba_kernel(q_ref, k_ref, v_ref, z_ref, qg_ref, kg_ref, slope_ref, o_ref,
                 kn_ref, vt_ref, kmean_ref, pieces_ref, rhs_ref, s_ref, acc_ref, *, n_blocks, kgroup, qgroup):
    it = pl.program_id(2)
    bs, hd, sub = MOBA_BLOCK, ATTN_HEAD_DIM, V7X_BF16_SUBLANES
    kt, qt = kgroup * bs, qgroup * bs
    n_kgroups = n_blocks // kgroup
    strip = min(qt, MOBA_QUERY_STRIP)
    add_rows = slice(MOBA_ADD_ROW, MOBA_ADD_ROW + sub)
    slope2 = slope_ref[0][:, :1] * LOG2E

    @pl.when(it == 0)
    def _():
        key_off = lax.broadcasted_iota(jnp.int32, (bs, hd), 0).astype(F32)
        lane = lax.broadcasted_iota(jnp.int32, (bs, hd), 1)
        add_lane = lane - (MOBA_ADD_ROW - hd)
        in_add = jnp.logical_and(add_lane >= 0, add_lane < MOBA_PIECES * kgroup)
        k_aug = [jnp.where(lane < MOBA_PIECES, key_off,
                           jnp.where(jnp.logical_and(in_add, add_lane % kgroup == bi), 1.0, 0.0)).astype(BF16)
                 for bi in range(kgroup)]
        ones_rows = jnp.where(lax.broadcasted_iota(jnp.int32, (sub, kt), 0) == 0, 1.0, 0.0).astype(BF16)

        def prep(g, carry):
            for bi in range(kgroup):
                c = g * kgroup + bi
                rows = pl.ds(pl.multiple_of(c * bs, bs), bs)
                kb = k_ref[rows, :].astype(F32)
                ms = jnp.mean(kb * kb, axis=-1, keepdims=True)
                kn = kb * lax.rsqrt(ms + EPS) * kg_ref[...]
                kn_ref[g, bi * bs:(bi + 1) * bs, :] = jnp.concatenate([kn.astype(BF16), k_aug[bi]], axis=1)
                kmean_ref[pl.ds(c, 1), :] = jnp.mean(kn, axis=0, keepdims=True)
                vt_ref[g, :hd, bi * bs:(bi + 1) * bs] = v_ref[rows, :].astype(F32).T.astype(BF16)
            vt_ref[g, hd:, :] = ones_rows
            return carry

        lax.fori_loop(0, n_kgroups, prep, 0)

    q = q_ref[...].astype(F32)
    ms = jnp.mean(q * q, axis=-1, keepdims=True)
    qn_t = (q * lax.rsqrt(ms + EPS) * qg_ref[...]).T

    row = lax.broadcasted_iota(jnp.int32, (sub, qt), 0)
    s_hi, s_mid, s_lo = _split_bf16(slope2)
    slope_rows = jnp.where(row == 0, s_hi, jnp.where(row == 1, s_mid, jnp.where(row == 2, s_lo, 0.0)))
    rhs_top = jnp.concatenate([qn_t * (hd ** -0.5 * LOG2E), slope_rows], axis=0).astype(BF16)
    for slot in range(2):
        rhs_ref[slot, :MOBA_ADD_ROW, :] = rhs_top
        rhs_ref[slot, MOBA_ADD_ROW + sub:, :] = jnp.zeros((2 * hd - MOBA_ADD_ROW - sub, qt), BF16)

    gate = jnp.dot(kmean_ref[...], qn_t, preferred_element_type=F32, precision=lax.Precision.HIGHEST)
    blk = lax.broadcasted_iota(jnp.int32, gate.shape, 0)
    own = it * qgroup + lax.broadcasted_iota(jnp.int32, gate.shape, 1) // bs
    blk_f = blk.astype(F32)
    past = blk < own
    g = jnp.where(past, gate, NEG)
    sel = jnp.zeros(gate.shape, jnp.bool_)
    for _ in range(min(MOBA_TOP_K, n_blocks)):
        top = jnp.max(g, axis=0, keepdims=True)
        first = jnp.min(jnp.where(g == top, blk_f, float(n_blocks)), axis=0, keepdims=True)
        pick = blk_f == first
        sel = jnp.logical_or(sel, pick)
        g = jnp.where(pick, -jnp.inf, g)
    sel = jnp.logical_and(sel, past)
    add = jnp.where(sel, slope2 * (bs * (blk - own)).astype(F32), jnp.where(blk == own, 0.0, NEG))
    add_pieces = _split_bf16(add)
    pad = jnp.zeros((sub - MOBA_PIECES * kgroup, qt), F32)
    for kg in range(n_kgroups):
        rows = slice(kg * kgroup, (kg + 1) * kgroup)
        pieces_ref[kg] = jnp.concatenate([piece[rows] for piece in add_pieces] + [pad], axis=0).astype(BF16)

    def scores_to(slot, gi, causal_group=None):
        rhs_ref[slot, add_rows, :] = pieces_ref[gi]
        s = jnp.dot(kn_ref[gi], rhs_ref[slot], preferred_element_type=F32)
        top = None
        for bi in range(kgroup):
            part = s[bi * bs:(bi + 1) * bs]
            if causal_group is not None:
                key_rel = (causal_group * kgroup + bi) * bs + lax.broadcasted_iota(jnp.int32, (bs, qt), 0)
                part = jnp.where(key_rel <= lax.broadcasted_iota(jnp.int32, (bs, qt), 1), part, NEG)
            s_ref[slot, bi * bs:(bi + 1) * bs, :] = part
            top = part if top is None else jnp.maximum(top, part)
        return jnp.max(top, axis=0, keepdims=True)

    def accumulate(slot, gi, m, top):
        m_new = jnp.maximum(m, top)
        alpha = jnp.exp2(m - m_new)
        for c0 in range(0, qt, strip):
            cols = slice(c0, c0 + strip)
            p = jnp.exp2(s_ref[slot, :, cols] - m_new[:, cols]).astype(BF16)
            acc_ref[:, cols] = alpha[:, cols] * acc_ref[:, cols] + jnp.dot(vt_ref[gi], p, preferred_element_type=F32)
        return m_new

    def pair(k, m):
        top0 = scores_to(0, 2 * k)
        top1 = scores_to(1, 2 * k + 1)
        return accumulate(1, 2 * k + 1, accumulate(0, 2 * k, m, top0), top1)

    acc_ref[...] = jnp.zeros(acc_ref.shape, F32)
    m = lax.fori_loop(0, it, pair, jnp.full((1, qt), 0.1 * NEG, F32))
    top0 = scores_to(0, 2 * it, causal_group=0)
    top1 = scores_to(1, 2 * it + 1, causal_group=1)
    accumulate(1, 2 * it + 1, accumulate(0, 2 * it, m, top0), top1)
    acc = acc_ref[...]
    o = (acc[:hd] / acc[hd:hd + 1]).T
    o_ref[...] = (o * jax.nn.silu(z_ref[...].astype(F32))).astype(o_ref.dtype)


def _moba(proj, q_gain, k_gain, bsz, seq, width):
    n_heads = width // ATTN_HEAD_DIM
    n_blocks = seq // MOBA_BLOCK
    hd, bs, sub = ATTN_HEAD_DIM, MOBA_BLOCK, V7X_BF16_SUBLANES
    kgroup, qgroup = MOBA_KEY_GROUP, MOBA_QUERY_GROUP
    assert qgroup == 2 * kgroup and n_blocks % qgroup == 0 and MOBA_PIECES * kgroup <= sub
    n_kgroups, n_qtiles, kt, qt = n_blocks // kgroup, n_blocks // qgroup, kgroup * bs, qgroup * bs
    slopes = jnp.asarray([[[2.0 ** (-8.0 * (h + 1) / n_heads)] * V7X_LANES] for h in range(n_heads)], F32)
    scratch = [
        pltpu.VMEM((n_kgroups, kt, 2 * hd), BF16),
        pltpu.VMEM((n_kgroups, hd + sub, kt), BF16),
        pltpu.VMEM((n_blocks, hd), F32),
        pltpu.VMEM((n_kgroups, sub, qt), BF16),
        pltpu.VMEM((2, 2 * hd, qt), BF16),
        pltpu.VMEM((2, kt, qt), F32),
        pltpu.VMEM((hd + sub, qt), F32),
    ]
    scratch_bytes = (n_kgroups * (kt * 2 * hd + (hd + sub) * kt + sub * qt) * 2 + n_blocks * hd * 4
                     + 2 * 2 * hd * qt * 2 + 2 * kt * qt * 4 + (hd + sub) * qt * 4)
    return pl.pallas_call(
        functools.partial(_moba_kernel, n_blocks=n_blocks, kgroup=kgroup, qgroup=qgroup),
        grid=(bsz, n_heads, n_qtiles),
        in_specs=[
            pl.BlockSpec((qt, hd), lambda b, h, i: (b * n_qtiles + i, h)),
            pl.BlockSpec((seq, hd), lambda b, h, i: (b, n_heads + h)),
            pl.BlockSpec((seq, hd), lambda b, h, i: (b, 2 * n_heads + h)),
            pl.BlockSpec((qt, hd), lambda b, h, i: (b * n_qtiles + i, 3 * n_heads + h)),
            pl.BlockSpec((1, hd), lambda b, h, i: (0, 0)),
            pl.BlockSpec((1, hd), lambda b, h, i: (0, 0)),
            pl.BlockSpec((1, 1, V7X_LANES), lambda b, h, i: (h, 0, 0)),
        ],
        out_specs=pl.BlockSpec((qt, hd), lambda b, h, i: (b * n_qtiles + i, h)),
        out_shape=jax.ShapeDtypeStruct((bsz * seq, width), BF16),
        scratch_shapes=scratch,
        compiler_params=_params([2 * seq * hd * 2, 3 * qt * hd * 2], scratch_bytes=scratch_bytes,
                                temp_bytes=4 * kt * qt * 4, n_axes=3),
        name="moba_attention",
    )(proj, proj, proj, proj, q_gain.reshape(1, hd).astype(F32), k_gain.reshape(1, hd).astype(F32), slopes)


def _s5_kernel(u_ref, ar_ref, ai_ref, ldt_ref, bbr_ref, bbi_ref, ccr_ref, cci_ref, d_ref, y_ref,
               p_ref, qt_ref, t_ref, dk_ref, sinc_ref, xs_ref, *, n_batch, rows_per_batch, row_tile):
    lb, gn, chunk = V7X_LANES, STATE_LANES, SSM_CHUNK
    ar, ai = ar_ref[0], ai_ref[0]
    dt = jnp.exp(ldt_ref[0])
    mag = jnp.exp(dt * ar)
    abr = mag * jnp.cos(dt * ai)
    abi = mag * jnp.sin(dt * ai)
    den = ar * ar + ai * ai
    nr = abr - 1.0
    f_re = (nr * ar + abi * ai) / den
    f_im = (abi * ar - nr * ai) / den

    row_g = lax.broadcasted_iota(jnp.int32, (lb, gn), 0) // SSM_GROUP
    col_g = lax.broadcasted_iota(jnp.int32, (lb, gn), 1) // SSM_STATE
    same = row_g == col_g
    bb_re, bb_im = bbr_ref[0], bbi_ref[0]
    bbar_re = jnp.where(same, f_re * bb_re - f_im * bb_im, 0.0)
    bbar_im = jnp.where(same, f_re * bb_im + f_im * bb_re, 0.0)
    cc_re = jnp.where(same, ccr_ref[0], 0.0)
    cc_im = jnp.where(same, cci_ref[0], 0.0)
    ccs = jnp.concatenate([cc_re, -cc_im], axis=1)

    pr = jnp.ones((1, gn), F32)
    pi = jnp.zeros((1, gn), F32)
    for tau in range(chunk + 1):
        if tau < chunk:
            ptau = jnp.concatenate([bbar_re * pr - bbar_im * pi, bbar_re * pi + bbar_im * pr], axis=1)
            s = chunk - 1 - tau
            p_ref[s * lb:(s + 1) * lb, :] = ptau.astype(BF16)
            dk_ref[tau] = lax.dot_general(ptau, ccs, (((1,), (1,)), ((), ())), preferred_element_type=F32,
                                          precision=lax.Precision.HIGHEST).astype(BF16)
        if tau >= 1:
            t = tau - 1
            qt_ref[t * lb:(t + 1) * lb, :] = jnp.concatenate(
                [cc_re * pr - cc_im * pi, -(cc_re * pi + cc_im * pr)], axis=1).astype(BF16)
        if tau < chunk:
            pr, pi = pr * abr - pi * abi, pr * abi + pi * abr
    al_re, al_im = pr, pi

    zero = jnp.zeros((lb, lb), BF16)
    for s in range(chunk):
        for t in range(chunk):
            t_ref[s * lb:(s + 1) * lb, t * lb:(t + 1) * lb] = dk_ref[t - s] if t >= s else zero

    n_rows = n_batch * rows_per_batch
    for r0 in range(0, n_rows, row_tile):
        sinc_ref[r0:r0 + row_tile, :] = jnp.dot(u_ref[0, r0:r0 + row_tile, :], p_ref[...], preferred_element_type=F32)

    def step(c, carry):
        out = []
        for b in range(n_batch):
            xr, xi = carry[b]
            row = pl.ds(b * rows_per_batch + c, 1)
            xs_ref[row, :gn] = xr
            xs_ref[row, gn:] = xi
            inc = sinc_ref[row, :]
            out.append((al_re * xr - al_im * xi + inc[:, :gn], al_re * xi + al_im * xr + inc[:, gn:]))
        return tuple(out)

    x0 = jnp.zeros((1, gn), F32)
    lax.fori_loop(0, rows_per_batch, step, tuple((x0, x0) for _ in range(n_batch)))

    for r0 in range(0, n_rows, row_tile):
        u = u_ref[0, r0:r0 + row_tile, :]
        y = jnp.dot(u, t_ref[...], preferred_element_type=F32)
        y = y + lax.dot_general(xs_ref[r0:r0 + row_tile, :].astype(BF16), qt_ref[...], (((1,), (1,)), ((), ())),
                                preferred_element_type=F32)
        y = y + d_ref[0] * u.astype(F32)
        y_ref[0, r0:r0 + row_tile, :] = jax.nn.gelu(y).astype(y_ref.dtype)


def _s5(u_blocks, a_re, a_im, log_dt, b_re, b_im, c_re, c_im, d_skip, n_batch):
    nb, n_rows, kdim = u_blocks.shape
    gpb, lb, gn, chunk = GROUPS_PER_LANE_BLOCK, V7X_LANES, STATE_LANES, SSM_CHUNK
    rows_per_batch = n_rows // n_batch
    row_tile = _tile(n_rows, 256)

    def lane_row(v):
        return v.astype(F32).reshape(nb, 1, gn)

    ldt = jnp.repeat(log_dt.astype(F32), SSM_STATE).reshape(nb, 1, gn)
    bt_re = jnp.tile(b_re.astype(F32).reshape(nb, gpb, SSM_STATE, SSM_GROUP).transpose(0, 3, 1, 2).reshape(nb, SSM_GROUP, gn), (1, gpb, 1))
    bt_im = jnp.tile(b_im.astype(F32).reshape(nb, gpb, SSM_STATE, SSM_GROUP).transpose(0, 3, 1, 2).reshape(nb, SSM_GROUP, gn), (1, gpb, 1))
    ct_re = jnp.tile(c_re.astype(F32).reshape(nb, lb, SSM_STATE), (1, 1, gpb))
    ct_im = jnp.tile(c_im.astype(F32).reshape(nb, lb, SSM_STATE), (1, 1, gpb))
    d_row = jnp.tile(d_skip.astype(F32).reshape(nb, 1, lb), (1, 1, chunk))

    vec = pl.BlockSpec((1, 1, gn), lambda j: (j, 0, 0))
    mat = pl.BlockSpec((1, lb, gn), lambda j: (j, 0, 0))
    blk = pl.BlockSpec((1, n_rows, kdim), lambda j: (j, 0, 0))
    scratch = [
        pltpu.VMEM((kdim, 2 * gn), BF16),
        pltpu.VMEM((kdim, 2 * gn), BF16),
        pltpu.VMEM((kdim, kdim), BF16),
        pltpu.VMEM((chunk, lb, lb), BF16),
        pltpu.VMEM((n_rows, 2 * gn), F32),
        pltpu.VMEM((n_rows, 2 * gn), F32),
    ]
    scratch_bytes = 2 * kdim * 2 * gn * 2 + kdim * kdim * 2 + chunk * lb * lb * 2 + 2 * n_rows * 2 * gn * 4
    return pl.pallas_call(
        functools.partial(_s5_kernel, n_batch=n_batch, rows_per_batch=rows_per_batch, row_tile=row_tile),
        grid=(nb,),
        in_specs=[blk, vec, vec, vec, mat, mat, mat, mat, pl.BlockSpec((1, 1, kdim), lambda j: (j, 0, 0))],
        out_specs=blk,
        out_shape=jax.ShapeDtypeStruct((nb, n_rows, kdim), BF16),
        scratch_shapes=scratch,
        compiler_params=_params([n_rows * kdim * 2, n_rows * kdim * 2, 4 * lb * gn * 4], scratch_bytes=scratch_bytes,
                                temp_bytes=4 * row_tile * kdim * 4),
        name="s5_scan",
    )(u_blocks, lane_row(a_re), lane_row(a_im), ldt, bt_re, bt_im, ct_re, ct_im, d_row)


def _glu_kernel(y_ref, ycol_ref, w_ref, b_ref, z_ref, o_ref):
    y = jnp.concatenate([y_ref[k] for k in range(y_ref.shape[0])], axis=1)
    ycol = jnp.concatenate([ycol_ref[k] for k in range(ycol_ref.shape[0])], axis=1).astype(F32)
    gate = jax.nn.sigmoid(jnp.dot(y, w_ref[...], preferred_element_type=F32) + b_ref[...])
    o_ref[...] = (ycol * gate * jax.nn.silu(z_ref[...].astype(F32))).astype(o_ref.dtype)


def _glu(y_blocks, w_glu, b_glu, proj, z_col0, tm, tn):
    nb, n, lb = y_blocks.shape
    width = nb * lb
    tm, tn = _tile(n, tm), _tile(width, tn)
    return pl.pallas_call(
        _glu_kernel,
        grid=(n // tm, width // tn),
        in_specs=[
            pl.BlockSpec((nb, tm, lb), lambda i, j: (0, i, 0)),
            pl.BlockSpec((tn // lb, tm, lb), lambda i, j: (j, i, 0)),
            pl.BlockSpec((width, tn), lambda i, j: (0, j)),
            pl.BlockSpec((1, tn), lambda i, j: (0, j)),
            pl.BlockSpec((tm, tn), lambda i, j: (i, z_col0 // tn + j)),
        ],
        out_specs=pl.BlockSpec((tm, tn), lambda i, j: (i, j)),
        out_shape=jax.ShapeDtypeStruct((n, width), BF16),
        compiler_params=_params([tm * width * 2, tm * tn * 2, width * tn * 2, tm * tn * 2, tm * tn * 2],
                                temp_bytes=tm * width * 2 + 3 * tm * tn * 4, n_axes=2),
        name="s5_glu",
    )(y_blocks, y_blocks, w_glu, b_glu.reshape(1, width).astype(F32), proj)


def _memattn_kernel(q_ref, z_ref, kv_ref, qg_ref, kg_ref, o_ref, *, width):
    dm = width // MEM_HEADS
    for hd in range(MEM_HEADS):
        cols = slice(hd * dm, (hd + 1) * dm)
        q = q_ref[:, cols].astype(F32)
        qn = q * lax.rsqrt(jnp.mean(q * q, axis=-1, keepdims=True) + EPS) * qg_ref[...]
        k = kv_ref[:, cols].astype(F32)
        kn = k * lax.rsqrt(jnp.mean(k * k, axis=-1, keepdims=True) + EPS) * kg_ref[...]
        v = kv_ref[:, width + hd * dm:width + (hd + 1) * dm]
        s = lax.dot_general(qn.astype(BF16), kn.astype(BF16), (((1,), (1,)), ((), ())),
                            preferred_element_type=F32) * (dm ** -0.5)
        p = jnp.exp(s - jnp.max(s, axis=-1, keepdims=True))
        l = jnp.sum(p, axis=-1, keepdims=True)
        o = jnp.dot(p.astype(BF16), v, preferred_element_type=F32) / l
        o_ref[:, cols] = (o * jax.nn.silu(z_ref[:, cols].astype(F32))).astype(o_ref.dtype)


def _memattn(proj, kv, q_gain, k_gain, bsz, seq, width, tq):
    dm = width // MEM_HEADS
    n_mem = kv.shape[0] // bsz
    tq = _tile(seq, tq)
    nq = seq // tq
    return pl.pallas_call(
        functools.partial(_memattn_kernel, width=width),
        grid=(bsz, nq),
        in_specs=[
            pl.BlockSpec((tq, width), lambda b, i: (b * nq + i, 6)),
            pl.BlockSpec((tq, width), lambda b, i: (b * nq + i, 7)),
            pl.BlockSpec((n_mem, 2 * width), lambda b, i: (b, 0)),
            pl.BlockSpec((1, dm), lambda b, i: (0, 0)),
            pl.BlockSpec((1, dm), lambda b, i: (0, 0)),
        ],
        out_specs=pl.BlockSpec((tq, width), lambda b, i: (b * nq + i, 0)),
        out_shape=jax.ShapeDtypeStruct((bsz * seq, width), BF16),
        compiler_params=_params([3 * tq * width * 2, n_mem * 2 * width * 2], temp_bytes=8 * tq * dm * 4, n_axes=2),
        name="memory_attention",
    )(proj, proj, kv, q_gain.reshape(1, dm).astype(F32), k_gain.reshape(1, dm).astype(F32))


def _merge_kernel(ya_ref, ys_ref, yc_ref, wa_ref, ws_ref, wc_ref, ga_ref, gs_ref, gc_ref, o_ref):
    def term(y_ref, w_ref, g_ref):
        return jax.nn.sigmoid(g_ref[...].astype(F32)) * jnp.dot(y_ref[...], w_ref[...], preferred_element_type=F32)

    o_ref[...] = (term(ya_ref, wa_ref, ga_ref) + term(ys_ref, ws_ref, gs_ref)
                  + term(yc_ref, wc_ref, gc_ref)).astype(o_ref.dtype)


def _merge(y_a, y_s, y_c, w_a, w_s, w_c, proj, g_col0, tm, tn):
    n, width = y_a.shape
    d = w_a.shape[1]
    tm, tn = _tile(n, tm), _tile(d, tn)
    y_spec = pl.BlockSpec((tm, width), lambda i, j: (i, 0))
    w_spec = pl.BlockSpec((width, tn), lambda i, j: (0, j))

    def g_spec(branch):
        return pl.BlockSpec((tm, tn), lambda i, j: (i, (g_col0 + branch * d) // tn + j))

    return pl.pallas_call(
        _merge_kernel,
        grid=(n // tm, d // tn),
        in_specs=[y_spec] * 3 + [w_spec] * 3 + [g_spec(0), g_spec(1), g_spec(2)],
        out_specs=pl.BlockSpec((tm, tn), lambda i, j: (i, j)),
        out_shape=jax.ShapeDtypeStruct((n, d), BF16),
        compiler_params=_params([3 * tm * width * 2, 3 * width * tn * 2, 4 * tm * tn * 2], temp_bytes=4 * tm * tn * 4, n_axes=2),
        name="branch_merge",
    )(y_a, y_s, y_c, w_a, w_s, w_c, proj, proj, proj)


def _outproj_kernel(m_ref, w_ref, x_ref, o_ref):
    o_ref[...] = x_ref[...] + jnp.dot(m_ref[...], w_ref[...], preferred_element_type=F32)


def _outproj(merged, w_out, x, tm, tn):
    n, d = merged.shape
    tm, tn = _tile(n, tm), _tile(d, tn)
    return pl.pallas_call(
        _outproj_kernel,
        grid=(n // tm, d // tn),
        in_specs=[
            pl.BlockSpec((tm, d), lambda i, j: (i, 0)),
            pl.BlockSpec((d, tn), lambda i, j: (0, j)),
            pl.BlockSpec((tm, tn), lambda i, j: (i, j)),
        ],
        out_specs=pl.BlockSpec((tm, tn), lambda i, j: (i, j)),
        out_shape=jax.ShapeDtypeStruct((n, d), F32),
        compiler_params=_params([tm * d * 2, d * tn * 2, 2 * tm * tn * 4], temp_bytes=tm * tn * 4, n_axes=2),
        name="out_projection",
    )(merged, w_out, x)


def kernel(x, mem, w_in, g_norm, g_mem, w_mem_kv, q_gain_a, k_gain_a, q_gain_c, k_gain_c, ssm_a_re, ssm_a_im, ssm_log_dt, ssm_b_re, ssm_b_im, ssm_c_re, ssm_c_im, ssm_d, w_glu, b_glu, w_br_a, w_br_s, w_br_c, w_out):
    bsz, seq, d_model = x.shape
    width = w_glu.shape[0]
    n_tok = bsz * seq
    n_mem = mem.shape[1]
    assert seq % MOBA_BLOCK == 0 and seq % SSM_CHUNK == 0 and width % V7X_LANES == 0
    assert w_in.shape == (d_model, 8 * width + 3 * d_model)

    x2 = x.reshape(n_tok, d_model)
    h = _rmsnorm(x2, g_norm, rows=256)
    proj, u = _matmul(h, w_in.astype(BF16), 1024, _tile(width, 1024), "in_projection",
                      lane_block_cols=(4 * width, width))

    m = _rmsnorm(mem.reshape(bsz * n_mem, d_model), g_mem, rows=256)
    kv = _matmul(m, w_mem_kv.astype(BF16), 512, 1024, "memory_kv_projection")

    y_a = _moba(proj, q_gain_a, k_gain_a, bsz, seq, width)

    nb = width // V7X_LANES
    u = u.reshape(nb, n_tok // SSM_CHUNK, SSM_CHUNK * V7X_LANES)
    y_g = _s5(u, ssm_a_re, ssm_a_im, ssm_log_dt, ssm_b_re, ssm_b_im, ssm_c_re, ssm_c_im, ssm_d, bsz)
    y_s = _glu(y_g.reshape(nb, n_tok, V7X_LANES), w_glu.astype(BF16), b_glu, proj, 5 * width, 1024, 1024)

    y_c = _memattn(proj, kv, q_gain_c, k_gain_c, bsz, seq, width, 512)

    merged = _merge(y_a, y_s, y_c, w_br_a.astype(BF16), w_br_s.astype(BF16), w_br_c.astype(BF16), proj,
                    8 * width, 1024, 512)
    out = _outproj(merged, w_out.astype(BF16), x2, 1024, 512)
    return out.reshape(bsz, seq, d_model)
```

```python
import functools

import jax
import jax.numpy as jnp
from jax import lax
from jax.experimental import pallas as pl
from jax.experimental.pallas import tpu as pltpu

F32 = jnp.float32
BF16 = jnp.bfloat16

V7X_LANES = 128
V7X_BF16_SUBLANES = 16
V7X_VMEM_BYTES = 64 * 1024 * 1024
V7X_VMEM_RESERVE_BYTES = 6 * 1024 * 1024

ATTN_HEAD_DIM = 128
MOBA_BLOCK = 256
MOBA_TOP_K = 3
SSM_GROUP = 16
SSM_STATE = 64
MEM_HEADS = 4
EPS = 1e-6
NEG = -1e30
LOG2E = 1.4426950408889634

MOBA_KEY_GROUP = 2
MOBA_QUERY_GROUP = 4
MOBA_QUERY_STRIP = 256
SSM_CHUNK = 16
S5_OUT_COL_TILE = 512
GROUPS_PER_LANE_BLOCK = V7X_LANES // SSM_GROUP
STATE_LANES = GROUPS_PER_LANE_BLOCK * SSM_STATE


def _tile(n, pref):
    t = min(n, pref)
    while n % t:
        t -= V7X_LANES
    assert t > 0
    return t


def _params(block_bytes, scratch_bytes=0, temp_bytes=0, n_axes=1):
    need = 2 * sum(block_bytes) + scratch_bytes + temp_bytes
    limit = min(max(need, 16 * 1024 * 1024), V7X_VMEM_BYTES - V7X_VMEM_RESERVE_BYTES)
    return pltpu.CompilerParams(dimension_semantics=("arbitrary",) * n_axes, vmem_limit_bytes=int(limit))


def _rmsnorm_kernel(x_ref, g_ref, o_ref):
    x = x_ref[...].astype(F32)
    ms = jnp.mean(x * x, axis=-1, keepdims=True)
    o_ref[...] = (x * lax.rsqrt(ms + EPS) * g_ref[...]).astype(o_ref.dtype)


def _rmsnorm(x, gain, rows):
    n, d = x.shape
    tm = _tile(n, rows)
    return pl.pallas_call(
        _rmsnorm_kernel,
        grid=(n // tm,),
        in_specs=[pl.BlockSpec((tm, d), lambda i: (i, 0)), pl.BlockSpec((1, d), lambda i: (0, 0))],
        out_specs=pl.BlockSpec((tm, d), lambda i: (i, 0)),
        out_shape=jax.ShapeDtypeStruct((n, d), BF16),
        compiler_params=_params([tm * d * 4, tm * d * 2], temp_bytes=2 * tm * d * 4),
        name="rmsnorm",
    )(x, gain.reshape(1, d).astype(F32))


def _mm_kernel(a_ref, b_ref, o_ref):
    o_ref[...] = jnp.dot(a_ref[...], b_ref[...], preferred_element_type=F32).astype(o_ref.dtype)


def _mm_lane_blocks_kernel(a_ref, b_ref, o_ref, blk_ref, *, j0, nj):
    res = jnp.dot(a_ref[...], b_ref[...], preferred_element_type=F32).astype(o_ref.dtype)
    o_ref[...] = res
    j = pl.program_id(1)

    @pl.when(jnp.logical_and(j >= j0, j < j0 + nj))
    def _():
        for k in range(blk_ref.shape[0]):
            blk_ref[k] = res[:, k * V7X_LANES:(k + 1) * V7X_LANES]


def _matmul(a, b, tm, tn, name, lane_block_cols=None):
    m, k = a.shape
    _, n = b.shape
    tm, tn = _tile(m, tm), _tile(n, tn)
    in_specs = [pl.BlockSpec((tm, k), lambda i, j: (i, 0)), pl.BlockSpec((k, tn), lambda i, j: (0, j))]
    out_spec = pl.BlockSpec((tm, tn), lambda i, j: (i, j))
    out_shape = jax.ShapeDtypeStruct((m, n), BF16)
    params = _params([tm * k * 2, k * tn * 2, 2 * tm * tn * 2], temp_bytes=tm * tn * (4 + 4 + 2), n_axes=2)
    if lane_block_cols is None:
        return pl.pallas_call(_mm_kernel, grid=(m // tm, n // tn), in_specs=in_specs, out_specs=out_spec,
                              out_shape=out_shape, compiler_params=params, name=name)(a, b)
    col0, ncols = lane_block_cols
    assert col0 % tn == 0 and ncols % tn == 0
    j0, nj, per = col0 // tn, ncols // tn, tn // V7X_LANES
    blk_spec = pl.BlockSpec((per, tm, V7X_LANES), lambda i, j: (jnp.clip(j - j0, 0, nj - 1), i, 0))
    return pl.pallas_call(
        functools.partial(_mm_lane_blocks_kernel, j0=j0, nj=nj),
        grid=(m // tm, n // tn), in_specs=in_specs, out_specs=[out_spec, blk_spec],
        out_shape=[out_shape, jax.ShapeDtypeStruct((ncols // V7X_LANES, m, V7X_LANES), BF16)],
        compiler_params=params, name=name)(a, b)


MOBA_ADD_ROW = ATTN_HEAD_DIM + V7X_BF16_SUBLANES
MOBA_PIECES = 3


def _split_bf16(x):
    hi = x.astype(BF16).astype(F32)
    mid = (x - hi).astype(BF16).astype(F32)
    return hi, mid, x - hi - mid


def _moba_kernel(q_ref, k_ref, v_ref, z_ref, qg_ref, kg_ref, slope_ref, o_ref,
                 kn_ref, vt_ref, kmean_ref, rhs_ref, s_ref, acc_ref, *, n_blocks, kgroup, qgroup):
    it = pl.program_id(2)
    bs, hd, sub = MOBA_BLOCK, ATTN_HEAD_DIM, V7X_BF16_SUBLANES
    kt, qt = kgroup * bs, qgroup * bs
    n_kgroups = n_blocks // kgroup
    strip = min(qt, MOBA_QUERY_STRIP)
    slope2 = slope_ref[0][:, :1] * LOG2E

    @pl.when(it == 0)
    def _():
        key_off = lax.broadcasted_iota(jnp.int32, (bs, hd), 0).astype(F32)
        lane = lax.broadcasted_iota(jnp.int32, (bs, hd), 1)
        add_lane = lane - (MOBA_ADD_ROW - hd)
        in_add = jnp.logical_and(add_lane >= 0, add_lane < MOBA_PIECES * n_blocks)
        ones_rows = jnp.where(lax.broadcasted_iota(jnp.int32, (sub, kt), 0) == 0, 1.0, 0.0).astype(BF16)

        def prep(g, carry):
            for bi in range(kgroup):
                c = g * kgroup + bi
                rows = pl.ds(pl.multiple_of(c * bs, bs), bs)
                kb = k_ref[rows, :].astype(F32)
                ms = jnp.mean(kb * kb, axis=-1, keepdims=True)
                kn = kb * lax.rsqrt(ms + EPS) * kg_ref[...]
                k_aug = jnp.where(lane < MOBA_PIECES, key_off,
                                  jnp.where(jnp.logical_and(in_add, add_lane % n_blocks == c), 1.0, 0.0))
                kn_ref[g, bi * bs:(bi + 1) * bs, :] = jnp.concatenate([kn.astype(BF16), k_aug.astype(BF16)], axis=1)
                kmean_ref[pl.ds(c, 1), :] = jnp.mean(kn, axis=0, keepdims=True)
                vt_ref[g, :hd, bi * bs:(bi + 1) * bs] = v_ref[rows, :].astype(F32).T.astype(BF16)
            vt_ref[g, hd:, :] = ones_rows
            return carry

        lax.fori_loop(0, n_kgroups, prep, 0)

    q = q_ref[...].astype(F32)
    ms = jnp.mean(q * q, axis=-1, keepdims=True)
    qn_t = (q * lax.rsqrt(ms + EPS) * qg_ref[...]).T

    gate = jnp.dot(kmean_ref[...], qn_t, preferred_element_type=F32, precision=lax.Precision.HIGHEST)
    blk = lax.broadcasted_iota(jnp.int32, gate.shape, 0)
    own = it * qgroup + lax.broadcasted_iota(jnp.int32, gate.shape, 1) // bs
    blk_f = blk.astype(F32)
    past = blk < own
    g = jnp.where(past, gate, NEG)
    sel = jnp.zeros(gate.shape, jnp.bool_)
    for _ in range(min(MOBA_TOP_K, n_blocks)):
        top = jnp.max(g, axis=0, keepdims=True)
        first = jnp.min(jnp.where(g == top, blk_f, float(n_blocks)), axis=0, keepdims=True)
        pick = blk_f == first
        sel = jnp.logical_or(sel, pick)
        g = jnp.where(pick, -jnp.inf, g)
    sel = jnp.logical_and(sel, past)
    add = jnp.where(sel, slope2 * (bs * (blk - own)).astype(F32), jnp.where(blk == own, 0.0, NEG))

    row = lax.broadcasted_iota(jnp.int32, (sub, qt), 0)
    s_hi, s_mid, s_lo = _split_bf16(slope2)
    slope_rows = jnp.where(row == 0, s_hi, jnp.where(row == 1, s_mid, jnp.where(row == 2, s_lo, 0.0)))
    pad = jnp.zeros((2 * hd - MOBA_ADD_ROW - MOBA_PIECES * n_blocks, qt), F32)
    rhs_ref[...] = jnp.concatenate([qn_t * (hd ** -0.5 * LOG2E), slope_rows, *_split_bf16(add), pad],
                                   axis=0).astype(BF16)

    def scores_to(slot, gi, causal_group=None):
        s = jnp.dot(kn_ref[gi], rhs_ref[...], preferred_element_type=F32)
        top = None
        for bi in range(kgroup):
            part = s[bi * bs:(bi + 1) * bs]
            if causal_group is not None:
                key_rel = (causal_group * kgroup + bi) * bs + lax.broadcasted_iota(jnp.int32, (bs, qt), 0)
                part = jnp.where(key_rel <= lax.broadcasted_iota(jnp.int32, (bs, qt), 1), part, NEG)
            s_ref[slot, bi * bs:(bi + 1) * bs, :] = part
            top = part if top is None else jnp.maximum(top, part)
        return jnp.max(top, axis=0, keepdims=True)

    def accumulate(slot, gi, m, top):
        m_new = jnp.maximum(m, top)
        alpha = jnp.exp2(m - m_new)
        for c0 in range(0, qt, strip):
            cols = slice(c0, c0 + strip)
            p = jnp.exp2(s_ref[slot, :, cols] - m_new[:, cols]).astype(BF16)
            acc_ref[:, cols] = alpha[:, cols] * acc_ref[:, cols] + jnp.dot(vt_ref[gi], p, preferred_element_type=F32)
        return m_new

    def pair(k, m):
        top0 = scores_to(0, 2 * k)
        top1 = scores_to(1, 2 * k + 1)
        return accumulate(1, 2 * k + 1, accumulate(0, 2 * k, m, top0), top1)

    acc_ref[...] = jnp.zeros(acc_ref.shape, F32)
    m = lax.fori_loop(0, it, pair, jnp.full((1, qt), 0.1 * NEG, F32))
    top0 = scores_to(0, 2 * it, causal_group=0)
    top1 = scores_to(1, 2 * it + 1, causal_group=1)
    accumulate(1, 2 * it + 1, accumulate(0, 2 * it, m, top0), top1)
    acc = acc_ref[...]
    o = (acc[:hd] / acc[hd:hd + 1]).T
    o_ref[...] = (o * jax.nn.silu(z_ref[...].astype(F32))).astype(o_ref.dtype)


def _moba(proj, q_gain, k_gain, bsz, seq, width):
    n_heads = width // ATTN_HEAD_DIM
    n_blocks = seq // MOBA_BLOCK
    hd, bs, sub = ATTN_HEAD_DIM, MOBA_BLOCK, V7X_BF16_SUBLANES
    kgroup, qgroup = MOBA_KEY_GROUP, MOBA_QUERY_GROUP
    assert qgroup == 2 * kgroup and n_blocks % qgroup == 0
    assert MOBA_ADD_ROW + MOBA_PIECES * n_blocks <= 2 * hd
    n_kgroups, n_qtiles, kt, qt = n_blocks // kgroup, n_blocks // qgroup, kgroup * bs, qgroup * bs
    slopes = jnp.asarray([[[2.0 ** (-8.0 * (h + 1) / n_heads)] * V7X_LANES] for h in range(n_heads)], F32)
    scratch = [
        pltpu.VMEM((n_kgroups, kt, 2 * hd), BF16),
        pltpu.VMEM((n_kgroups, hd + sub, kt), BF16),
        pltpu.VMEM((n_blocks, hd), F32),
        pltpu.VMEM((2 * hd, qt), BF16),
        pltpu.VMEM((2, kt, qt), F32),
        pltpu.VMEM((hd + sub, qt), F32),
    ]
    scratch_bytes = (n_kgroups * (kt * 2 * hd + (hd + sub) * kt) * 2 + n_blocks * hd * 4
                     + 2 * hd * qt * 2 + 2 * kt * qt * 4 + (hd + sub) * qt * 4)
    return pl.pallas_call(
        functools.partial(_moba_kernel, n_blocks=n_blocks, kgroup=kgroup, qgroup=qgroup),
        grid=(bsz, n_heads, n_qtiles),
        in_specs=[
            pl.BlockSpec((qt, hd), lambda b, h, i: (b * n_qtiles + i, h)),
            pl.BlockSpec((seq, hd), lambda b, h, i: (b, n_heads + h)),
            pl.BlockSpec((seq, hd), lambda b, h, i: (b, 2 * n_heads + h)),
            pl.BlockSpec((qt, hd), lambda b, h, i: (b * n_qtiles + i, 3 * n_heads + h)),
            pl.BlockSpec((1, hd), lambda b, h, i: (0, 0)),
            pl.BlockSpec((1, hd), lambda b, h, i: (0, 0)),
            pl.BlockSpec((1, 1, V7X_LANES), lambda b, h, i: (h, 0, 0)),
        ],
        out_specs=pl.BlockSpec((qt, hd), lambda b, h, i: (b * n_qtiles + i, h)),
        out_shape=jax.ShapeDtypeStruct((bsz * seq, width), BF16),
        scratch_shapes=scratch,
        compiler_params=_params([2 * seq * hd * 2, 3 * qt * hd * 2], scratch_bytes=scratch_bytes,
                                temp_bytes=4 * kt * qt * 4, n_axes=3),
        name="moba_attention",
    )(proj, proj, proj, proj, q_gain.reshape(1, hd).astype(F32), k_gain.reshape(1, hd).astype(F32), slopes)


def _s5_kernel(u_ref, ar_ref, ai_ref, ldt_ref, bbr_ref, bbi_ref, ccr_ref, cci_ref, d_ref, y_ref,
               p_ref, qt_ref, t_ref, dk_ref, stage_ref, uc_ref, xs_ref, *, n_batch, rows_per_batch, row_tile):
    lb, gn, chunk = V7X_LANES, STATE_LANES, SSM_CHUNK
    kdim = chunk * lb
    col_tile = min(kdim, S5_OUT_COL_TILE)
    ar, ai = ar_ref[0], ai_ref[0]
    dt = jnp.exp(ldt_ref[0])
    mag = jnp.exp(dt * ar)
    abr = mag * jnp.cos(dt * ai)
    abi = mag * jnp.sin(dt * ai)
    den = ar * ar + ai * ai
    nr = abr - 1.0
    f_re = (nr * ar + abi * ai) / den
    f_im = (abi * ar - nr * ai) / den

    row_g = lax.broadcasted_iota(jnp.int32, (lb, gn), 0) // SSM_GROUP
    col_g = lax.broadcasted_iota(jnp.int32, (lb, gn), 1) // SSM_STATE
    same = row_g == col_g
    bb_re, bb_im = bbr_ref[0], bbi_ref[0]
    bbar_re = jnp.where(same, f_re * bb_re - f_im * bb_im, 0.0)
    bbar_im = jnp.where(same, f_re * bb_im + f_im * bb_re, 0.0)
    cc_re = jnp.where(same, ccr_ref[0], 0.0)
    cc_im = jnp.where(same, cci_ref[0], 0.0)
    ccs = jnp.concatenate([cc_re, -cc_im], axis=1)

    pr = jnp.ones((1, gn), F32)
    pi = jnp.zeros((1, gn), F32)
    for tau in range(chunk + 1):
        if tau < chunk:
            ptau = jnp.concatenate([bbar_re * pr - bbar_im * pi, bbar_re * pi + bbar_im * pr], axis=1)
            s = chunk - 1 - tau
            p_ref[s * lb:(s + 1) * lb, :] = ptau.astype(BF16)
            dk_ref[tau] = lax.dot_general(ptau, ccs, (((1,), (1,)), ((), ())), preferred_element_type=F32,
                                          precision=lax.Precision.HIGHEST).astype(BF16)
        if tau >= 1:
            t = tau - 1
            qt_ref[t * lb:(t + 1) * lb, :] = jnp.concatenate(
                [cc_re * pr - cc_im * pi, -(cc_re * pi + cc_im * pr)], axis=1).astype(BF16)
        if tau < chunk:
            pr, pi = pr * abr - pi * abi, pr * abi + pi * abr
    al_re, al_im = pr, pi

    zero = jnp.zeros((lb, lb), BF16)
    for s in range(chunk):
        for t in range(chunk):
            t_ref[s * lb:(s + 1) * lb, t * lb:(t + 1) * lb] = dk_ref[t - s] if t >= s else zero

    n_rows = n_batch * rows_per_batch
    steps_per_batch = rows_per_batch * chunk
    for b in range(n_batch):
        stage_ref[...] = u_ref[0, b * steps_per_batch:(b + 1) * steps_per_batch, :].astype(F32)
        for s in range(chunk):
            uc_ref[b * rows_per_batch:(b + 1) * rows_per_batch, s * lb:(s + 1) * lb] = (
                stage_ref[pl.ds(s, rows_per_batch, stride=chunk), :].astype(BF16))

    for r0 in range(0, n_rows, row_tile):
        xs_ref[r0:r0 + row_tile, :] = jnp.dot(uc_ref[r0:r0 + row_tile, :], p_ref[...], preferred_element_type=F32)

    def step(c, carry):
        out = []
        for b in range(n_batch):
            xr, xi = carry[b]
            row = pl.ds(b * rows_per_batch + c, 1)
            inc = xs_ref[row, :]
            xs_ref[row, :gn] = xr
            xs_ref[row, gn:] = xi
            out.append((al_re * xr - al_im * xi + inc[:, :gn], al_re * xi + al_im * xr + inc[:, gn:]))
        return tuple(out)

    x0 = jnp.zeros((1, gn), F32)
    lax.fori_loop(0, rows_per_batch, step, tuple((x0, x0) for _ in range(n_batch)))

    for b in range(n_batch):
        for r0 in range(0, rows_per_batch, row_tile):
            rows = slice(b * rows_per_batch + r0, b * rows_per_batch + r0 + row_tile)
            x_start = xs_ref[rows, :].astype(BF16)
            for c0 in range(0, kdim, col_tile):
                cols = slice(c0, c0 + col_tile)
                y = jnp.dot(uc_ref[rows, :c0 + col_tile], t_ref[:c0 + col_tile, cols], preferred_element_type=F32)
                y = y + lax.dot_general(x_start, qt_ref[cols, :], (((1,), (1,)), ((), ())),
                                        preferred_element_type=F32)
                y = jax.nn.gelu(y + d_ref[0][:, cols] * uc_ref[rows, cols].astype(F32))
                for s in range(c0 // lb, (c0 + col_tile) // lb):
                    stage_ref[pl.ds(r0 * chunk + s, row_tile, stride=chunk), :] = y[:, s * lb - c0:(s + 1) * lb - c0]
        y_ref[0, b * steps_per_batch:(b + 1) * steps_per_batch, :] = stage_ref[...].astype(y_ref.dtype)


def _s5(u_blocks, a_re, a_im, log_dt, b_re, b_im, c_re, c_im, d_skip, n_batch):
    nb, n_steps, lb = u_blocks.shape
    gpb, gn, chunk = GROUPS_PER_LANE_BLOCK, STATE_LANES, SSM_CHUNK
    n_rows, kdim = n_steps // chunk, chunk * lb
    rows_per_batch = n_rows // n_batch
    row_tile = _tile(rows_per_batch, 256)

    def lane_row(v):
        return v.astype(F32).reshape(nb, 1, gn)

    ldt = jnp.repeat(log_dt.astype(F32), SSM_STATE).reshape(nb, 1, gn)
    bt_re = jnp.tile(b_re.astype(F32).reshape(nb, gpb, SSM_STATE, SSM_GROUP).transpose(0, 3, 1, 2).reshape(nb, SSM_GROUP, gn), (1, gpb, 1))
    bt_im = jnp.tile(b_im.astype(F32).reshape(nb, gpb, SSM_STATE, SSM_GROUP).transpose(0, 3, 1, 2).reshape(nb, SSM_GROUP, gn), (1, gpb, 1))
    ct_re = jnp.tile(c_re.astype(F32).reshape(nb, lb, SSM_STATE), (1, 1, gpb))
    ct_im = jnp.tile(c_im.astype(F32).reshape(nb, lb, SSM_STATE), (1, 1, gpb))
    d_row = jnp.tile(d_skip.astype(F32).reshape(nb, 1, lb), (1, 1, chunk))

    vec = pl.BlockSpec((1, 1, gn), lambda j: (j, 0, 0))
    mat = pl.BlockSpec((1, lb, gn), lambda j: (j, 0, 0))
    blk = pl.BlockSpec((1, n_steps, lb), lambda j: (j, 0, 0))
    scratch = [
        pltpu.VMEM((kdim, 2 * gn), BF16),
        pltpu.VMEM((kdim, 2 * gn), BF16),
        pltpu.VMEM((kdim, kdim), BF16),
        pltpu.VMEM((chunk, lb, lb), BF16),
        pltpu.VMEM((n_steps // n_batch, lb), F32),
        pltpu.VMEM((n_rows, kdim), BF16),
        pltpu.VMEM((n_rows, 2 * gn), F32),
    ]
    scratch_bytes = (2 * kdim * 2 * gn * 2 + kdim * kdim * 2 + chunk * lb * lb * 2 + n_steps // n_batch * lb * 4
                     + n_rows * kdim * 2 + n_rows * 2 * gn * 4)
    return pl.pallas_call(
        functools.partial(_s5_kernel, n_batch=n_batch, rows_per_batch=rows_per_batch, row_tile=row_tile),
        grid=(nb,),
        in_specs=[blk, vec, vec, vec, mat, mat, mat, mat, pl.BlockSpec((1, 1, kdim), lambda j: (j, 0, 0))],
        out_specs=blk,
        out_shape=jax.ShapeDtypeStruct((nb, n_steps, lb), BF16),
        scratch_shapes=scratch,
        compiler_params=_params([n_steps * lb * 2, n_steps * lb * 2, 4 * lb * gn * 4], scratch_bytes=scratch_bytes,
                                temp_bytes=4 * row_tile * kdim * 4),
        name="s5_scan",
    )(u_blocks, lane_row(a_re), lane_row(a_im), ldt, bt_re, bt_im, ct_re, ct_im, d_row)


def _glu_kernel(y_ref, ycol_ref, w_ref, b_ref, z_ref, o_ref):
    y = jnp.concatenate([y_ref[k] for k in range(y_ref.shape[0])], axis=1)
    ycol = jnp.concatenate([ycol_ref[k] for k in range(ycol_ref.shape[0])], axis=1).astype(F32)
    gate = jax.nn.sigmoid(jnp.dot(y, w_ref[...], preferred_element_type=F32) + b_ref[...])
    o_ref[...] = (ycol * gate * jax.nn.silu(z_ref[...].astype(F32))).astype(o_ref.dtype)


def _glu(y_blocks, w_glu, b_glu, proj, z_col0, tm, tn):
    nb, n, lb = y_blocks.shape
    width = nb * lb
    tm, tn = _tile(n, tm), _tile(width, tn)
    return pl.pallas_call(
        _glu_kernel,
        grid=(n // tm, width // tn),
        in_specs=[
            pl.BlockSpec((nb, tm, lb), lambda i, j: (0, i, 0)),
            pl.BlockSpec((tn // lb, tm, lb), lambda i, j: (j, i, 0)),
            pl.BlockSpec((width, tn), lambda i, j: (0, j)),
            pl.BlockSpec((1, tn), lambda i, j: (0, j)),
            pl.BlockSpec((tm, tn), lambda i, j: (i, z_col0 // tn + j)),
        ],
        out_specs=pl.BlockSpec((tm, tn), lambda i, j: (i, j)),
        out_shape=jax.ShapeDtypeStruct((n, width), BF16),
        compiler_params=_params([tm * width * 2, tm * tn * 2, width * tn * 2, tm * tn * 2, tm * tn * 2],
                                temp_bytes=tm * width * 2 + 3 * tm * tn * 4, n_axes=2),
        name="s5_glu",
    )(y_blocks, y_blocks, w_glu, b_glu.reshape(1, width).astype(F32), proj)


def _memattn_kernel(q_ref, z_ref, kv_ref, qg_ref, kg_ref, o_ref, *, width):
    dm = width // MEM_HEADS
    for hd in range(MEM_HEADS):
        cols = slice(hd * dm, (hd + 1) * dm)
        q = q_ref[:, cols].astype(F32)
        qn = q * lax.rsqrt(jnp.mean(q * q, axis=-1, keepdims=True) + EPS) * qg_ref[...]
        k = kv_ref[:, cols].astype(F32)
        kn = k * lax.rsqrt(jnp.mean(k * k, axis=-1, keepdims=True) + EPS) * kg_ref[...]
        v = kv_ref[:, width + hd * dm:width + (hd + 1) * dm]
        s = lax.dot_general(qn.astype(BF16), kn.astype(BF16), (((1,), (1,)), ((), ())),
                            preferred_element_type=F32) * (dm ** -0.5)
        p = jnp.exp(s - jnp.max(s, axis=-1, keepdims=True))
        l = jnp.sum(p, axis=-1, keepdims=True)
        o = jnp.dot(p.astype(BF16), v, preferred_element_type=F32) / l
        o_ref[:, cols] = (o * jax.nn.silu(z_ref[:, cols].astype(F32))).astype(o_ref.dtype)


def _memattn(proj, kv, q_gain, k_gain, bsz, seq, width, tq):
    dm = width // MEM_HEADS
    n_mem = kv.shape[0] // bsz
    tq = _tile(seq, tq)
    nq = seq // tq
    return pl.pallas_call(
        functools.partial(_memattn_kernel, width=width),
        grid=(bsz, nq),
        in_specs=[
            pl.BlockSpec((tq, width), lambda b, i: (b * nq + i, 6)),
            pl.BlockSpec((tq, width), lambda b, i: (b * nq + i, 7)),
            pl.BlockSpec((n_mem, 2 * width), lambda b, i: (b, 0)),
            pl.BlockSpec((1, dm), lambda b, i: (0, 0)),
            pl.BlockSpec((1, dm), lambda b, i: (0, 0)),
        ],
        out_specs=pl.BlockSpec((tq, width), lambda b, i: (b * nq + i, 0)),
        out_shape=jax.ShapeDtypeStruct((bsz * seq, width), BF16),
        compiler_params=_params([3 * tq * width * 2, n_mem * 2 * width * 2], temp_bytes=8 * tq * dm * 4, n_axes=2),
        name="memory_attention",
    )(proj, proj, kv, q_gain.reshape(1, dm).astype(F32), k_gain.reshape(1, dm).astype(F32))


def _merge_kernel(ya_ref, ys_ref, yc_ref, wa_ref, ws_ref, wc_ref, ga_ref, gs_ref, gc_ref, o_ref):
    def term(y_ref, w_ref, g_ref):
        return jax.nn.sigmoid(g_ref[...].astype(F32)) * jnp.dot(y_ref[...], w_ref[...], preferred_element_type=F32)

    o_ref[...] = (term(ya_ref, wa_ref, ga_ref) + term(ys_ref, ws_ref, gs_ref)
                  + term(yc_ref, wc_ref, gc_ref)).astype(o_ref.dtype)


def _merge(y_a, y_s, y_c, w_a, w_s, w_c, proj, g_col0, tm, tn):
    n, width = y_a.shape
    d = w_a.shape[1]
    tm, tn = _tile(n, tm), _tile(d, tn)
    y_spec = pl.BlockSpec((tm, width), lambda i, j: (i, 0))
    w_spec = pl.BlockSpec((width, tn), lambda i, j: (0, j))

    def g_spec(branch):
        return pl.BlockSpec((tm, tn), lambda i, j: (i, (g_col0 + branch * d) // tn + j))

    return pl.pallas_call(
        _merge_kernel,
        grid=(n // tm, d // tn),
        in_specs=[y_spec] * 3 + [w_spec] * 3 + [g_spec(0), g_spec(1), g_spec(2)],
        out_specs=pl.BlockSpec((tm, tn), lambda i, j: (i, j)),
        out_shape=jax.ShapeDtypeStruct((n, d), BF16),
        compiler_params=_params([3 * tm * width * 2, 3 * width * tn * 2, 4 * tm * tn * 2], temp_bytes=4 * tm * tn * 4, n_axes=2),
        name="branch_merge",
    )(y_a, y_s, y_c, w_a, w_s, w_c, proj, proj, proj)


def _outproj_kernel(m_ref, w_ref, x_ref, o_ref):
    o_ref[...] = x_ref[...] + jnp.dot(m_ref[...], w_ref[...], preferred_element_type=F32)


def _outproj(merged, w_out, x, tm, tn):
    n, d = merged.shape
    tm, tn = _tile(n, tm), _tile(d, tn)
    return pl.pallas_call(
        _outproj_kernel,
        grid=(n // tm, d // tn),
        in_specs=[
            pl.BlockSpec((tm, d), lambda i, j: (i, 0)),
            pl.BlockSpec((d, tn), lambda i, j: (0, j)),
            pl.BlockSpec((tm, tn), lambda i, j: (i, j)),
        ],
        out_specs=pl.BlockSpec((tm, tn), lambda i, j: (i, j)),
        out_shape=jax.ShapeDtypeStruct((n, d), F32),
        compiler_params=_params([tm * d * 2, d * tn * 2, 2 * tm * tn * 4], temp_bytes=tm * tn * 4, n_axes=2),
        name="out_projection",
    )(merged, w_out, x)


def kernel(x, mem, w_in, g_norm, g_mem, w_mem_kv, q_gain_a, k_gain_a, q_gain_c, k_gain_c, ssm_a_re, ssm_a_im, ssm_log_dt, ssm_b_re, ssm_b_im, ssm_c_re, ssm_c_im, ssm_d, w_glu, b_glu, w_br_a, w_br_s, w_br_c, w_out):
    bsz, seq, d_model = x.shape
    width = w_glu.shape[0]
    n_tok = bsz * seq
    n_mem = mem.shape[1]
    assert seq % MOBA_BLOCK == 0 and seq % SSM_CHUNK == 0 and width % V7X_LANES == 0
    assert w_in.shape == (d_model, 8 * width + 3 * d_model)

    x2 = x.reshape(n_tok, d_model)
    h = _rmsnorm(x2, g_norm, rows=256)
    proj, u = _matmul(h, w_in.astype(BF16), 1024, _tile(width, 1024), "in_projection",
                      lane_block_cols=(4 * width, width))

    m = _rmsnorm(mem.reshape(bsz * n_mem, d_model), g_mem, rows=256)
    kv = _matmul(m, w_mem_kv.astype(BF16), 512, 1024, "memory_kv_projection")

    y_a = _moba(proj, q_gain_a, k_gain_a, bsz, seq, width)

    y_g = _s5(u, ssm_a_re, ssm_a_im, ssm_log_dt, ssm_b_re, ssm_b_im, ssm_c_re, ssm_c_im, ssm_d, bsz)
    y_s = _glu(y_g, w_glu.astype(BF16), b_glu, proj, 5 * width, 1024, 1024)

    y_c = _memattn(proj, kv, q_gain_c, k_gain_c, bsz, seq, width, 512)

    merged = _merge(y_a, y_s, y_c, w_br_a.astype(BF16), w_br_s.astype(BF16), w_br_c.astype(BF16), proj,
                    8 * width, 1024, 512)
    out = _outproj(merged, w_out.astype(BF16), x2, 1024, 512)
    return out.reshape(bsz, seq, d_model)
```

```python
import functools

import jax
import jax.numpy as jnp
from jax import lax
from jax.experimental import pallas as pl
from jax.experimental.pallas import tpu as pltpu

F32 = jnp.float32
BF16 = jnp.bfloat16

V7X_LANES = 128
V7X_BF16_SUBLANES = 16
V7X_VMEM_BYTES = 64 * 1024 * 1024
V7X_VMEM_RESERVE_BYTES = 6 * 1024 * 1024

ATTN_HEAD_DIM = 128
MOBA_BLOCK = 256
MOBA_TOP_K = 3
SSM_GROUP = 16
SSM_STATE = 64
MEM_HEADS = 4
EPS = 1e-6
NEG = -1e30
LOG2E = 1.4426950408889634

MOBA_KEY_GROUP = 2
MOBA_QUERY_GROUP = 4
MOBA_QUERY_STRIP = 256
SSM_CHUNK = 16
S5_OUT_COL_TILE = 512
GROUPS_PER_LANE_BLOCK = V7X_LANES // SSM_GROUP
STATE_LANES = GROUPS_PER_LANE_BLOCK * SSM_STATE


def _tile(n, pref):
    t = min(n, pref)
    while n % t:
        t -= V7X_LANES
    assert t > 0
    return t


def _params(block_bytes, scratch_bytes=0, temp_bytes=0, n_axes=1):
    need = 2 * sum(block_bytes) + scratch_bytes + temp_bytes
    limit = min(max(need, 16 * 1024 * 1024), V7X_VMEM_BYTES - V7X_VMEM_RESERVE_BYTES)
    return pltpu.CompilerParams(dimension_semantics=("arbitrary",) * n_axes, vmem_limit_bytes=int(limit))


def _rmsnorm_kernel(x_ref, g_ref, o_ref):
    x = x_ref[...].astype(F32)
    ms = jnp.mean(x * x, axis=-1, keepdims=True)
    o_ref[...] = (x * lax.rsqrt(ms + EPS) * g_ref[...]).astype(o_ref.dtype)


def _rmsnorm(x, gain, rows):
    n, d = x.shape
    tm = _tile(n, rows)
    return pl.pallas_call(
        _rmsnorm_kernel,
        grid=(n // tm,),
        in_specs=[pl.BlockSpec((tm, d), lambda i: (i, 0)), pl.BlockSpec((1, d), lambda i: (0, 0))],
        out_specs=pl.BlockSpec((tm, d), lambda i: (i, 0)),
        out_shape=jax.ShapeDtypeStruct((n, d), BF16),
        compiler_params=_params([tm * d * 4, tm * d * 2], temp_bytes=2 * tm * d * 4),
        name="rmsnorm",
    )(x, gain.reshape(1, d).astype(F32))


def _mm_kernel(a_ref, b_ref, o_ref):
    o_ref[...] = jnp.dot(a_ref[...], b_ref[...], preferred_element_type=F32).astype(o_ref.dtype)


def _mm_lane_blocks_kernel(a_ref, b_ref, o_ref, blk_ref, *, j0, nj):
    res = jnp.dot(a_ref[...], b_ref[...], preferred_element_type=F32).astype(o_ref.dtype)
    o_ref[...] = res
    j = pl.program_id(1)

    @pl.when(jnp.logical_and(j >= j0, j < j0 + nj))
    def _():
        for k in range(blk_ref.shape[0]):
            blk_ref[k] = res[:, k * V7X_LANES:(k + 1) * V7X_LANES]


def _matmul(a, b, tm, tn, name, lane_block_cols=None):
    m, k = a.shape
    _, n = b.shape
    tm, tn = _tile(m, tm), _tile(n, tn)
    in_specs = [pl.BlockSpec((tm, k), lambda i, j: (i, 0)), pl.BlockSpec((k, tn), lambda i, j: (0, j))]
    out_spec = pl.BlockSpec((tm, tn), lambda i, j: (i, j))
    out_shape = jax.ShapeDtypeStruct((m, n), BF16)
    params = _params([tm * k * 2, k * tn * 2, 2 * tm * tn * 2], temp_bytes=tm * tn * (4 + 4 + 2), n_axes=2)
    if lane_block_cols is None:
        return pl.pallas_call(_mm_kernel, grid=(m // tm, n // tn), in_specs=in_specs, out_specs=out_spec,
                              out_shape=out_shape, compiler_params=params, name=name)(a, b)
    col0, ncols = lane_block_cols
    assert col0 % tn == 0 and ncols % tn == 0
    j0, nj, per = col0 // tn, ncols // tn, tn // V7X_LANES
    blk_spec = pl.BlockSpec((per, tm, V7X_LANES), lambda i, j: (jnp.clip(j - j0, 0, nj - 1), i, 0))
    return pl.pallas_call(
        functools.partial(_mm_lane_blocks_kernel, j0=j0, nj=nj),
        grid=(m // tm, n // tn), in_specs=in_specs, out_specs=[out_spec, blk_spec],
        out_shape=[out_shape, jax.ShapeDtypeStruct((ncols // V7X_LANES, m, V7X_LANES), BF16)],
        compiler_params=params, name=name)(a, b)


MOBA_ADD_ROW = ATTN_HEAD_DIM + V7X_BF16_SUBLANES
MOBA_PIECES = 3


def _split_bf16(x):
    hi = x.astype(BF16).astype(F32)
    mid = (x - hi).astype(BF16).astype(F32)
    return hi, mid, x - hi - mid


def _moba_kernel(q_ref, k_ref, v_ref, z_ref, qg_ref, kg_ref, slope_ref, o_ref,
                 kn_ref, vt_ref, kmean_ref, rhs_ref, s_ref, acc_ref, *, n_blocks, kgroup, qgroup):
    it = pl.program_id(2)
    bs, hd, sub = MOBA_BLOCK, ATTN_HEAD_DIM, V7X_BF16_SUBLANES
    kt, qt = kgroup * bs, qgroup * bs
    n_kgroups = n_blocks // kgroup
    strip = min(qt, MOBA_QUERY_STRIP)
    slope2 = slope_ref[0][:, :1] * LOG2E

    @pl.when(it == 0)
    def _():
        key_off = lax.broadcasted_iota(jnp.int32, (bs, hd), 0).astype(F32)
        lane = lax.broadcasted_iota(jnp.int32, (bs, hd), 1)
        add_lane = lane - (MOBA_ADD_ROW - hd)
        in_add = jnp.logical_and(add_lane >= 0, add_lane < MOBA_PIECES * n_blocks)
        ones_rows = jnp.where(lax.broadcasted_iota(jnp.int32, (sub, kt), 0) == 0, 1.0, 0.0).astype(BF16)

        def prep(g, carry):
            for bi in range(kgroup):
                c = g * kgroup + bi
                rows = pl.ds(pl.multiple_of(c * bs, bs), bs)
                kb = k_ref[rows, :].astype(F32)
                ms = jnp.mean(kb * kb, axis=-1, keepdims=True)
                kn = kb * lax.rsqrt(ms + EPS) * kg_ref[...]
                k_aug = jnp.where(lane < MOBA_PIECES, key_off,
                                  jnp.where(jnp.logical_and(in_add, add_lane % n_blocks == c), 1.0, 0.0))
                kn_ref[g, bi * bs:(bi + 1) * bs, :] = jnp.concatenate([kn.astype(BF16), k_aug.astype(BF16)], axis=1)
                kmean_ref[pl.ds(c, 1), :] = jnp.mean(kn, axis=0, keepdims=True)
                vt_ref[g, :hd, bi * bs:(bi + 1) * bs] = v_ref[rows, :].astype(F32).T.astype(BF16)
            vt_ref[g, hd:, :] = ones_rows
            return carry

        lax.fori_loop(0, n_kgroups, prep, 0)

    q = q_ref[...].astype(F32)
    ms = jnp.mean(q * q, axis=-1, keepdims=True)
    qn_t = (q * lax.rsqrt(ms + EPS) * qg_ref[...]).T

    gate = jnp.dot(kmean_ref[...], qn_t, preferred_element_type=F32, precision=lax.Precision.HIGHEST)
    blk = lax.broadcasted_iota(jnp.int32, gate.shape, 0)
    own = it * qgroup + lax.broadcasted_iota(jnp.int32, gate.shape, 1) // bs
    blk_f = blk.astype(F32)
    past = blk < own
    g = jnp.where(past, gate, NEG)
    sel = jnp.zeros(gate.shape, jnp.bool_)
    for _ in range(min(MOBA_TOP_K, n_blocks)):
        top = jnp.max(g, axis=0, keepdims=True)
        first = jnp.min(jnp.where(g == top, blk_f, float(n_blocks)), axis=0, keepdims=True)
        pick = blk_f == first
        sel = jnp.logical_or(sel, pick)
        g = jnp.where(pick, -jnp.inf, g)
    sel = jnp.logical_and(sel, past)
    add = jnp.where(sel, slope2 * (bs * (blk - own)).astype(F32), jnp.where(blk == own, 0.0, NEG))

    row = lax.broadcasted_iota(jnp.int32, (sub, qt), 0)
    s_hi, s_mid, s_lo = _split_bf16(slope2)
    slope_rows = jnp.where(row == 0, s_hi, jnp.where(row == 1, s_mid, jnp.where(row == 2, s_lo, 0.0)))
    pad = jnp.zeros((2 * hd - MOBA_ADD_ROW - MOBA_PIECES * n_blocks, qt), F32)
    rhs_ref[...] = jnp.concatenate([qn_t * (hd ** -0.5 * LOG2E), slope_rows, *_split_bf16(add), pad],
                                   axis=0).astype(BF16)

    def scores_to(slot, gi, causal_group=None):
        s = jnp.dot(kn_ref[gi], rhs_ref[...], preferred_element_type=F32)
        top = None
        for bi in range(kgroup):
            part = s[bi * bs:(bi + 1) * bs]
            if causal_group is not None:
                d0 = (causal_group * kgroup + bi) * bs
                tri = lax.broadcasted_iota(jnp.int32, (bs, bs), 0) <= lax.broadcasted_iota(jnp.int32, (bs, bs), 1)
                pieces = [part[:, :d0], jnp.where(tri, part[:, d0:d0 + bs], NEG), part[:, d0 + bs:]]
                part = jnp.concatenate([piece for piece in pieces if piece.shape[1]], axis=1)
            s_ref[slot, bi * bs:(bi + 1) * bs, :] = part
            top = part if top is None else jnp.maximum(top, part)
        return jnp.max(top, axis=0, keepdims=True)

    def accumulate(slot, gi, m, top, causal_group=None):
        m_new = jnp.maximum(m, top)
        alpha = jnp.exp2(m - m_new)
        for c0 in range(0, qt, strip):
            cols = slice(c0, c0 + strip)
            live = kgroup
            if causal_group is not None:
                live = min(max(c0 // bs - causal_group * kgroup + 1, 0), kgroup)
            if live == 0:
                continue
            p = jnp.exp2(s_ref[slot, :live * bs, cols] - m_new[:, cols]).astype(BF16)
            pv = jnp.dot(vt_ref[gi, :, :live * bs], p, preferred_element_type=F32)
            acc_ref[:, cols] = alpha[:, cols] * acc_ref[:, cols] + pv
        return m_new

    def pair(k, m):
        top0 = scores_to(0, 2 * k)
        top1 = scores_to(1, 2 * k + 1)
        return accumulate(1, 2 * k + 1, accumulate(0, 2 * k, m, top0), top1)

    acc_ref[...] = jnp.zeros(acc_ref.shape, F32)
    m = lax.fori_loop(0, it, pair, jnp.full((1, qt), 0.1 * NEG, F32))
    top0 = scores_to(0, 2 * it, causal_group=0)
    top1 = scores_to(1, 2 * it + 1, causal_group=1)
    accumulate(1, 2 * it + 1, accumulate(0, 2 * it, m, top0, causal_group=0), top1, causal_group=1)
    acc = acc_ref[...]
    o = (acc[:hd] / acc[hd:hd + 1]).T
    o_ref[...] = (o * jax.nn.silu(z_ref[...].astype(F32))).astype(o_ref.dtype)


def _moba(proj, q_gain, k_gain, bsz, seq, width):
    n_heads = width // ATTN_HEAD_DIM
    n_blocks = seq // MOBA_BLOCK
    hd, bs, sub = ATTN_HEAD_DIM, MOBA_BLOCK, V7X_BF16_SUBLANES
    kgroup, qgroup = MOBA_KEY_GROUP, MOBA_QUERY_GROUP
    assert qgroup == 2 * kgroup and n_blocks % qgroup == 0
    assert MOBA_ADD_ROW + MOBA_PIECES * n_blocks <= 2 * hd
    n_kgroups, n_qtiles, kt, qt = n_blocks // kgroup, n_blocks // qgroup, kgroup * bs, qgroup * bs
    slopes = jnp.asarray([[[2.0 ** (-8.0 * (h + 1) / n_heads)] * V7X_LANES] for h in range(n_heads)], F32)
    scratch = [
        pltpu.VMEM((n_kgroups, kt, 2 * hd), BF16),
        pltpu.VMEM((n_kgroups, hd + sub, kt), BF16),
        pltpu.VMEM((n_blocks, hd), F32),
        pltpu.VMEM((2 * hd, qt), BF16),
        pltpu.VMEM((2, kt, qt), F32),
        pltpu.VMEM((hd + sub, qt), F32),
    ]
    scratch_bytes = (n_kgroups * (kt * 2 * hd + (hd + sub) * kt) * 2 + n_blocks * hd * 4
                     + 2 * hd * qt * 2 + 2 * kt * qt * 4 + (hd + sub) * qt * 4)
    return pl.pallas_call(
        functools.partial(_moba_kernel, n_blocks=n_blocks, kgroup=kgroup, qgroup=qgroup),
        grid=(bsz, n_heads, n_qtiles),
        in_specs=[
            pl.BlockSpec((qt, hd), lambda b, h, i: (b * n_qtiles + i, h)),
            pl.BlockSpec((seq, hd), lambda b, h, i: (b, n_heads + h)),
            pl.BlockSpec((seq, hd), lambda b, h, i: (b, 2 * n_heads + h)),
            pl.BlockSpec((qt, hd), lambda b, h, i: (b * n_qtiles + i, 3 * n_heads + h)),
            pl.BlockSpec((1, hd), lambda b, h, i: (0, 0)),
            pl.BlockSpec((1, hd), lambda b, h, i: (0, 0)),
            pl.BlockSpec((1, 1, V7X_LANES), lambda b, h, i: (h, 0, 0)),
        ],
        out_specs=pl.BlockSpec((qt, hd), lambda b, h, i: (b * n_qtiles + i, h)),
        out_shape=jax.ShapeDtypeStruct((bsz * seq, width), BF16),
        scratch_shapes=scratch,
        compiler_params=_params([2 * seq * hd * 2, 3 * qt * hd * 2], scratch_bytes=scratch_bytes,
                                temp_bytes=4 * kt * qt * 4, n_axes=3),
        name="moba_attention",
    )(proj, proj, proj, proj, q_gain.reshape(1, hd).astype(F32), k_gain.reshape(1, hd).astype(F32), slopes)


def _zoh(ar, ai, log_dt):
    dt = jnp.exp(log_dt)
    mag = jnp.exp(dt * ar)
    abr = mag * jnp.cos(dt * ai)
    abi = mag * jnp.sin(dt * ai)
    den = ar * ar + ai * ai
    nr = abr - 1.0
    return abr, abi, (nr * ar + abi * ai) / den, (abi * ar - nr * ai) / den


def _s5_kernel(u_ref, ar_ref, ai_ref, ldt_ref, bbr_ref, bbi_ref, ccr_ref, cci_ref, d_ref,
               ard_ref, aid_ref, ldtd_ref, bdr_ref, bdi_ref, cdr_ref, cdi_ref, y_ref,
               p_ref, qt_ref, t_ref, dk_ref, stage_ref, uc_ref, xs_ref, *, n_batch, rows_per_batch, row_tile):
    lb, gn, chunk = V7X_LANES, STATE_LANES, SSM_CHUNK
    kdim = chunk * lb
    col_tile = min(kdim, S5_OUT_COL_TILE)
    abr, abi, f_re, f_im = _zoh(ar_ref[0], ai_ref[0], ldt_ref[0])

    abr_d, abi_d, f_re_d, f_im_d = _zoh(ard_ref[0], aid_ref[0], ldtd_ref[0])
    bd_re = f_re_d * bdr_ref[0] - f_im_d * bdi_ref[0]
    bd_im = f_re_d * bdi_ref[0] + f_im_d * bdr_ref[0]
    cd = jnp.concatenate([cdr_ref[0], -cdi_ref[0]], axis=1)
    same_group = (lax.broadcasted_iota(jnp.int32, (lb, lb), 0) // SSM_GROUP
                  == lax.broadcasted_iota(jnp.int32, (lb, lb), 1) // SSM_GROUP)
    pr_d = jnp.ones(abr_d.shape, F32)
    pi_d = jnp.zeros(abr_d.shape, F32)
    for tau in range(chunk):
        lag = jnp.concatenate([bd_re * pr_d - bd_im * pi_d, bd_re * pi_d + bd_im * pr_d], axis=1)
        blocks = lax.dot_general(lag, cd, (((1,), (1,)), ((), ())), preferred_element_type=F32,
                                 precision=lax.Precision.HIGHEST)
        dk_ref[tau] = jnp.where(same_group, blocks, 0.0).astype(BF16)
        pr_d, pi_d = pr_d * abr_d - pi_d * abi_d, pr_d * abi_d + pi_d * abr_d

    row_g = lax.broadcasted_iota(jnp.int32, (lb, gn), 0) // SSM_GROUP
    col_g = lax.broadcasted_iota(jnp.int32, (lb, gn), 1) // SSM_STATE
    same = row_g == col_g
    bb_re, bb_im = bbr_ref[0], bbi_ref[0]
    bbar_re = jnp.where(same, f_re * bb_re - f_im * bb_im, 0.0)
    bbar_im = jnp.where(same, f_re * bb_im + f_im * bb_re, 0.0)
    cc_re = jnp.where(same, ccr_ref[0], 0.0)
    cc_im = jnp.where(same, cci_ref[0], 0.0)

    pr = jnp.ones((1, gn), F32)
    pi = jnp.zeros((1, gn), F32)
    for tau in range(chunk + 1):
        if tau < chunk:
            s = chunk - 1 - tau
            p_ref[s * lb:(s + 1) * lb, :] = jnp.concatenate(
                [bbar_re * pr - bbar_im * pi, bbar_re * pi + bbar_im * pr], axis=1).astype(BF16)
        if tau >= 1:
            t = tau - 1
            qt_ref[t * lb:(t + 1) * lb, :] = jnp.concatenate(
                [cc_re * pr - cc_im * pi, -(cc_re * pi + cc_im * pr)], axis=1).astype(BF16)
        if tau < chunk:
            pr, pi = pr * abr - pi * abi, pr * abi + pi * abr
    al_re, al_im = pr, pi

    zero = jnp.zeros((lb, lb), BF16)
    for s in range(chunk):
        for t in range(chunk):
            t_ref[s * lb:(s + 1) * lb, t * lb:(t + 1) * lb] = dk_ref[t - s] if t >= s else zero

    n_rows = n_batch * rows_per_batch
    steps_per_batch = rows_per_batch * chunk
    for b in range(n_batch):
        stage_ref[...] = u_ref[0, b * steps_per_batch:(b + 1) * steps_per_batch, :].astype(F32)
        for s in range(chunk):
            uc_ref[b * rows_per_batch:(b + 1) * rows_per_batch, s * lb:(s + 1) * lb] = (
                stage_ref[pl.ds(s, rows_per_batch, stride=chunk), :].astype(BF16))

    for r0 in range(0, n_rows, row_tile):
        xs_ref[r0:r0 + row_tile, :] = jnp.dot(uc_ref[r0:r0 + row_tile, :], p_ref[...], preferred_element_type=F32)

    def step(c, carry):
        out = []
        for b in range(n_batch):
            xr, xi = carry[b]
            row = pl.ds(b * rows_per_batch + c, 1)
            inc = xs_ref[row, :]
            xs_ref[row, :gn] = xr
            xs_ref[row, gn:] = xi
            out.append((al_re * xr - al_im * xi + inc[:, :gn], al_re * xi + al_im * xr + inc[:, gn:]))
        return tuple(out)

    x0 = jnp.zeros((1, gn), F32)
    lax.fori_loop(0, rows_per_batch, step, tuple((x0, x0) for _ in range(n_batch)))

    for b in range(n_batch):
        for r0 in range(0, rows_per_batch, row_tile):
            rows = slice(b * rows_per_batch + r0, b * rows_per_batch + r0 + row_tile)
            x_start = xs_ref[rows, :].astype(BF16)
            for c0 in range(0, kdim, col_tile):
                cols = slice(c0, c0 + col_tile)
                y = jnp.dot(uc_ref[rows, :c0 + col_tile], t_ref[:c0 + col_tile, cols], preferred_element_type=F32)
                y = y + lax.dot_general(x_start, qt_ref[cols, :], (((1,), (1,)), ((), ())),
                                        preferred_element_type=F32)
                y = jax.nn.gelu(y + d_ref[0][:, cols] * uc_ref[rows, cols].astype(F32))
                for s in range(c0 // lb, (c0 + col_tile) // lb):
                    stage_ref[pl.ds(r0 * chunk + s, row_tile, stride=chunk), :] = y[:, s * lb - c0:(s + 1) * lb - c0]
        y_ref[0, b * steps_per_batch:(b + 1) * steps_per_batch, :] = stage_ref[...].astype(y_ref.dtype)


def _s5(u_blocks, a_re, a_im, log_dt, b_re, b_im, c_re, c_im, d_skip, n_batch):
    nb, n_steps, lb = u_blocks.shape
    gpb, gn, chunk = GROUPS_PER_LANE_BLOCK, STATE_LANES, SSM_CHUNK
    n_rows, kdim = n_steps // chunk, chunk * lb
    rows_per_batch = n_rows // n_batch
    row_tile = _tile(rows_per_batch, 256)

    def lane_row(v):
        return v.astype(F32).reshape(nb, 1, gn)

    ldt = jnp.repeat(log_dt.astype(F32), SSM_STATE).reshape(nb, 1, gn)
    bt_re = jnp.tile(b_re.astype(F32).reshape(nb, gpb, SSM_STATE, SSM_GROUP).transpose(0, 3, 1, 2).reshape(nb, SSM_GROUP, gn), (1, gpb, 1))
    bt_im = jnp.tile(b_im.astype(F32).reshape(nb, gpb, SSM_STATE, SSM_GROUP).transpose(0, 3, 1, 2).reshape(nb, SSM_GROUP, gn), (1, gpb, 1))
    ct_re = jnp.tile(c_re.astype(F32).reshape(nb, lb, SSM_STATE), (1, 1, gpb))
    ct_im = jnp.tile(c_im.astype(F32).reshape(nb, lb, SSM_STATE), (1, 1, gpb))
    d_row = jnp.tile(d_skip.astype(F32).reshape(nb, 1, lb), (1, 1, chunk))

    def per_channel_rows(v):
        return jnp.repeat(v.astype(F32).reshape(nb, gpb, SSM_STATE), SSM_GROUP, axis=1)

    ldt_d = per_channel_rows(jnp.broadcast_to(log_dt[:, None], a_re.shape))
    bd_re = b_re.astype(F32).reshape(nb, gpb, SSM_STATE, SSM_GROUP).transpose(0, 1, 3, 2).reshape(nb, lb, SSM_STATE)
    bd_im = b_im.astype(F32).reshape(nb, gpb, SSM_STATE, SSM_GROUP).transpose(0, 1, 3, 2).reshape(nb, lb, SSM_STATE)
    cd_re = c_re.astype(F32).reshape(nb, lb, SSM_STATE)
    cd_im = c_im.astype(F32).reshape(nb, lb, SSM_STATE)
    dia = pl.BlockSpec((1, lb, SSM_STATE), lambda j: (j, 0, 0))

    vec = pl.BlockSpec((1, 1, gn), lambda j: (j, 0, 0))
    mat = pl.BlockSpec((1, lb, gn), lambda j: (j, 0, 0))
    blk = pl.BlockSpec((1, n_steps, lb), lambda j: (j, 0, 0))
    scratch = [
        pltpu.VMEM((kdim, 2 * gn), BF16),
        pltpu.VMEM((kdim, 2 * gn), BF16),
        pltpu.VMEM((kdim, kdim), BF16),
        pltpu.VMEM((chunk, lb, lb), BF16),
        pltpu.VMEM((n_steps // n_batch, lb), F32),
        pltpu.VMEM((n_rows, kdim), BF16),
        pltpu.VMEM((n_rows, 2 * gn), F32),
    ]
    scratch_bytes = (2 * kdim * 2 * gn * 2 + kdim * kdim * 2 + chunk * lb * lb * 2 + n_steps // n_batch * lb * 4
                     + n_rows * kdim * 2 + n_rows * 2 * gn * 4)
    return pl.pallas_call(
        functools.partial(_s5_kernel, n_batch=n_batch, rows_per_batch=rows_per_batch, row_tile=row_tile),
        grid=(nb,),
        in_specs=[blk, vec, vec, vec, mat, mat, mat, mat, pl.BlockSpec((1, 1, kdim), lambda j: (j, 0, 0))] + [dia] * 7,
        out_specs=blk,
        out_shape=jax.ShapeDtypeStruct((nb, n_steps, lb), BF16),
        scratch_shapes=scratch,
        compiler_params=_params([n_steps * lb * 2, n_steps * lb * 2, 4 * lb * gn * 4], scratch_bytes=scratch_bytes,
                                temp_bytes=4 * row_tile * kdim * 4),
        name="s5_scan",
    )(u_blocks, lane_row(a_re), lane_row(a_im), ldt, bt_re, bt_im, ct_re, ct_im, d_row,
      per_channel_rows(a_re), per_channel_rows(a_im), ldt_d, bd_re, bd_im, cd_re, cd_im)


def _glu_kernel(y_ref, ycol_ref, w_ref, b_ref, z_ref, o_ref):
    y = jnp.concatenate([y_ref[k] for k in range(y_ref.shape[0])], axis=1)
    ycol = jnp.concatenate([ycol_ref[k] for k in range(ycol_ref.shape[0])], axis=1).astype(F32)
    gate = jax.nn.sigmoid(jnp.dot(y, w_ref[...], preferred_element_type=F32) + b_ref[...])
    o_ref[...] = (ycol * gate * jax.nn.silu(z_ref[...].astype(F32))).astype(o_ref.dtype)


def _glu(y_blocks, w_glu, b_glu, proj, z_col0, tm, tn):
    nb, n, lb = y_blocks.shape
    width = nb * lb
    tm, tn = _tile(n, tm), _tile(width, tn)
    return pl.pallas_call(
        _glu_kernel,
        grid=(n // tm, width // tn),
        in_specs=[
            pl.BlockSpec((nb, tm, lb), lambda i, j: (0, i, 0)),
            pl.BlockSpec((tn // lb, tm, lb), lambda i, j: (j, i, 0)),
            pl.BlockSpec((width, tn), lambda i, j: (0, j)),
            pl.BlockSpec((1, tn), lambda i, j: (0, j)),
            pl.BlockSpec((tm, tn), lambda i, j: (i, z_col0 // tn + j)),
        ],
        out_specs=pl.BlockSpec((tm, tn), lambda i, j: (i, j)),
        out_shape=jax.ShapeDtypeStruct((n, width), BF16),
        compiler_params=_params([tm * width * 2, tm * tn * 2, width * tn * 2, tm * tn * 2, tm * tn * 2],
                                temp_bytes=tm * width * 2 + 3 * tm * tn * 4, n_axes=2),
        name="s5_glu",
    )(y_blocks, y_blocks, w_glu, b_glu.reshape(1, width).astype(F32), proj)


def _memattn_kernel(q_ref, z_ref, kv_ref, qg_ref, kg_ref, o_ref, *, width):
    dm = width // MEM_HEADS
    for hd in range(MEM_HEADS):
        cols = slice(hd * dm, (hd + 1) * dm)
        q = q_ref[:, cols].astype(F32)
        qn = q * lax.rsqrt(jnp.mean(q * q, axis=-1, keepdims=True) + EPS) * qg_ref[...]
        k = kv_ref[:, cols].astype(F32)
        kn = k * lax.rsqrt(jnp.mean(k * k, axis=-1, keepdims=True) + EPS) * kg_ref[...]
        v = kv_ref[:, width + hd * dm:width + (hd + 1) * dm]
        s = lax.dot_general(qn.astype(BF16), kn.astype(BF16), (((1,), (1,)), ((), ())),
                            preferred_element_type=F32) * (dm ** -0.5)
        p = jnp.exp(s - jnp.max(s, axis=-1, keepdims=True))
        l = jnp.sum(p, axis=-1, keepdims=True)
        o = jnp.dot(p.astype(BF16), v, preferred_element_type=F32) / l
        o_ref[:, cols] = (o * jax.nn.silu(z_ref[:, cols].astype(F32))).astype(o_ref.dtype)


def _memattn(proj, kv, q_gain, k_gain, bsz, seq, width, tq):
    dm = width // MEM_HEADS
    n_mem = kv.shape[0] // bsz
    tq = _tile(seq, tq)
    nq = seq // tq
    return pl.pallas_call(
        functools.partial(_memattn_kernel, width=width),
        grid=(bsz, nq),
        in_specs=[
            pl.BlockSpec((tq, width), lambda b, i: (b * nq + i, 6)),
            pl.BlockSpec((tq, width), lambda b, i: (b * nq + i, 7)),
            pl.BlockSpec((n_mem, 2 * width), lambda b, i: (b, 0)),
            pl.BlockSpec((1, dm), lambda b, i: (0, 0)),
            pl.BlockSpec((1, dm), lambda b, i: (0, 0)),
        ],
        out_specs=pl.BlockSpec((tq, width), lambda b, i: (b * nq + i, 0)),
        out_shape=jax.ShapeDtypeStruct((bsz * seq, width), BF16),
        compiler_params=_params([3 * tq * width * 2, n_mem * 2 * width * 2], temp_bytes=8 * tq * dm * 4, n_axes=2),
        name="memory_attention",
    )(proj, proj, kv, q_gain.reshape(1, dm).astype(F32), k_gain.reshape(1, dm).astype(F32))


def _merge_kernel(ya_ref, ys_ref, yc_ref, wa_ref, ws_ref, wc_ref, ga_ref, gs_ref, gc_ref, o_ref):
    def term(y_ref, w_ref, g_ref):
        return jax.nn.sigmoid(g_ref[...].astype(F32)) * jnp.dot(y_ref[...], w_ref[...], preferred_element_type=F32)

    o_ref[...] = (term(ya_ref, wa_ref, ga_ref) + term(ys_ref, ws_ref, gs_ref)
                  + term(yc_ref, wc_ref, gc_ref)).astype(o_ref.dtype)


def _merge(y_a, y_s, y_c, w_a, w_s, w_c, proj, g_col0, tm, tn):
    n, width = y_a.shape
    d = w_a.shape[1]
    tm, tn = _tile(n, tm), _tile(d, tn)
    y_spec = pl.BlockSpec((tm, width), lambda i, j: (i, 0))
    w_spec = pl.BlockSpec((width, tn), lambda i, j: (0, j))

    def g_spec(branch):
        return pl.BlockSpec((tm, tn), lambda i, j: (i, (g_col0 + branch * d) // tn + j))

    return pl.pallas_call(
        _merge_kernel,
        grid=(n // tm, d // tn),
        in_specs=[y_spec] * 3 + [w_spec] * 3 + [g_spec(0), g_spec(1), g_spec(2)],
        out_specs=pl.BlockSpec((tm, tn), lambda i, j: (i, j)),
        out_shape=jax.ShapeDtypeStruct((n, d), BF16),
        compiler_params=_params([3 * tm * width * 2, 3 * width * tn * 2, 4 * tm * tn * 2], temp_bytes=4 * tm * tn * 4, n_axes=2),
        name="branch_merge",
    )(y_a, y_s, y_c, w_a, w_s, w_c, proj, proj, proj)


def _outproj_kernel(m_ref, w_ref, x_ref, o_ref):
    o_ref[...] = x_ref[...] + jnp.dot(m_ref[...], w_ref[...], preferred_element_type=F32)


def _outproj(merged, w_out, x, tm, tn):
    n, d = merged.shape
    tm, tn = _tile(n, tm), _tile(d, tn)
    return pl.pallas_call(
        _outproj_kernel,
        grid=(n // tm, d // tn),
        in_specs=[
            pl.BlockSpec((tm, d), lambda i, j: (i, 0)),
            pl.BlockSpec((d, tn), lambda i, j: (0, j)),
            pl.BlockSpec((tm, tn), lambda i, j: (i, j)),
        ],
        out_specs=pl.BlockSpec((tm, tn), lambda i, j: (i, j)),
        out_shape=jax.ShapeDtypeStruct((n, d), F32),
        compiler_params=_params([tm * d * 2, d * tn * 2, 2 * tm * tn * 4], temp_bytes=tm * tn * 4, n_axes=2),
        name="out_projection",
    )(merged, w_out, x)


def kernel(x, mem, w_in, g_norm, g_mem, w_mem_kv, q_gain_a, k_gain_a, q_gain_c, k_gain_c, ssm_a_re, ssm_a_im, ssm_log_dt, ssm_b_re, ssm_b_im, ssm_c_re, ssm_c_im, ssm_d, w_glu, b_glu, w_br_a, w_br_s, w_br_c, w_out):
    bsz, seq, d_model = x.shape
    width = w_glu.shape[0]
    n_tok = bsz * seq
    n_mem = mem.shape[1]
    assert seq % MOBA_BLOCK == 0 and seq % SSM_CHUNK == 0 and width % V7X_LANES == 0
    assert w_in.shape == (d_model, 8 * width + 3 * d_model)

    x2 = x.reshape(n_tok, d_model)
    h = _rmsnorm(x2, g_norm, rows=256)
    proj, u = _matmul(h, w_in.astype(BF16), 1024, _tile(width, 1024), "in_projection",
                      lane_block_cols=(4 * width, width))

    m = _rmsnorm(mem.reshape(bsz * n_mem, d_model), g_mem, rows=256)
    kv = _matmul(m, w_mem_kv.astype(BF16), 512, 1024, "memory_kv_projection")

    y_a = _moba(proj, q_gain_a, k_gain_a, bsz, seq, width)

    y_g = _s5(u, ssm_a_re, ssm_a_im, ssm_log_dt, ssm_b_re, ssm_b_im, ssm_c_re, ssm_c_im, ssm_d, bsz)
    y_s = _glu(y_g, w_glu.astype(BF16), b_glu, proj, 5 * width, 512, width)

    y_c = _memattn(proj, kv, q_gain_c, k_gain_c, bsz, seq, width, 512)

    merged = _merge(y_a, y_s, y_c, w_br_a.astype(BF16), w_br_s.astype(BF16), w_br_c.astype(BF16), proj,
                    8 * width, 1024, 512)
    out = _outproj(merged, w_out.astype(BF16), x2, 1024, 512)
    return out.reshape(bsz, seq, d_model)
```

```python
import functools

import jax
import jax.numpy as jnp
from jax import lax
from jax.experimental import pallas as pl
from jax.experimental.pallas import tpu as pltpu

F32 = jnp.float32
BF16 = jnp.bfloat16

V7X_LANES = 128
V7X_F32_SUBLANES = 8
V7X_BF16_SUBLANES = 16
V7X_VMEM_BYTES = 64 * 1024 * 1024
V7X_VMEM_RESERVE_BYTES = 6 * 1024 * 1024

ATTN_HEAD_DIM = 128
MOBA_BLOCK = 256
MOBA_TOP_K = 3
SSM_GROUP = 16
SSM_STATE = 64
MEM_HEADS = 4
EPS = 1e-6
NEG = -1e30
LOG2E = 1.4426950408889634

MOBA_KEY_GROUP = 2
MOBA_QUERY_GROUP = 4
MOBA_QUERY_STRIP = 256
SSM_CHUNK = 16
S5_OUT_COL_TILE = 512
GROUPS_PER_LANE_BLOCK = V7X_LANES // SSM_GROUP
STATE_LANES = GROUPS_PER_LANE_BLOCK * SSM_STATE


def _tile(n, pref):
    t = min(n, pref)
    while n % t:
        t -= V7X_LANES
    assert t > 0
    return t


def _params(block_bytes, scratch_bytes=0, temp_bytes=0, n_axes=1):
    need = 2 * sum(block_bytes) + scratch_bytes + temp_bytes
    limit = min(max(need, 16 * 1024 * 1024), V7X_VMEM_BYTES - V7X_VMEM_RESERVE_BYTES)
    return pltpu.CompilerParams(dimension_semantics=("arbitrary",) * n_axes, vmem_limit_bytes=int(limit))


def _rmsnorm_kernel(x_ref, g_ref, o_ref):
    x = x_ref[...].astype(F32)
    ms = jnp.mean(x * x, axis=-1, keepdims=True)
    o_ref[...] = (x * lax.rsqrt(ms + EPS) * g_ref[...]).astype(o_ref.dtype)


def _rmsnorm(x, gain, rows):
    n, d = x.shape
    tm = _tile(n, rows)
    return pl.pallas_call(
        _rmsnorm_kernel,
        grid=(n // tm,),
        in_specs=[pl.BlockSpec((tm, d), lambda i: (i, 0)), pl.BlockSpec((1, d), lambda i: (0, 0))],
        out_specs=pl.BlockSpec((tm, d), lambda i: (i, 0)),
        out_shape=jax.ShapeDtypeStruct((n, d), BF16),
        compiler_params=_params([tm * d * 4, tm * d * 2], temp_bytes=2 * tm * d * 4),
        name="rmsnorm",
    )(x, gain.reshape(1, d).astype(F32))


def _mm_kernel(a_ref, b_ref, o_ref):
    o_ref[...] = jnp.dot(a_ref[...], b_ref[...].astype(BF16), preferred_element_type=F32).astype(o_ref.dtype)


def _mm_lane_blocks_kernel(a_ref, b_ref, o_ref, blk_ref, *, j0, nj):
    res = jnp.dot(a_ref[...], b_ref[...], preferred_element_type=F32).astype(o_ref.dtype)
    o_ref[...] = res
    j = pl.program_id(1)

    @pl.when(jnp.logical_and(j >= j0, j < j0 + nj))
    def _():
        for k in range(blk_ref.shape[0]):
            blk_ref[k] = res[:, k * V7X_LANES:(k + 1) * V7X_LANES]


def _matmul(a, b, tm, tn, name, lane_block_cols=None):
    m, k = a.shape
    _, n = b.shape
    tm, tn = _tile(m, tm), _tile(n, tn)
    in_specs = [pl.BlockSpec((tm, k), lambda i, j: (i, 0)), pl.BlockSpec((k, tn), lambda i, j: (0, j))]
    out_spec = pl.BlockSpec((tm, tn), lambda i, j: (i, j))
    out_shape = jax.ShapeDtypeStruct((m, n), BF16)
    b_cast_bytes = k * tn * 2 if b.dtype != BF16 else 0
    params = _params([tm * k * 2, k * tn * b.dtype.itemsize, 2 * tm * tn * 2],
                     temp_bytes=tm * tn * (4 + 4 + 2) + b_cast_bytes, n_axes=2)
    if lane_block_cols is None:
        return pl.pallas_call(_mm_kernel, grid=(m // tm, n // tn), in_specs=in_specs, out_specs=out_spec,
                              out_shape=out_shape, compiler_params=params, name=name)(a, b)
    col0, ncols = lane_block_cols
    assert col0 % tn == 0 and ncols % tn == 0
    j0, nj, per = col0 // tn, ncols // tn, tn // V7X_LANES
    blk_spec = pl.BlockSpec((per, tm, V7X_LANES), lambda i, j: (jnp.clip(j - j0, 0, nj - 1), i, 0))
    return pl.pallas_call(
        functools.partial(_mm_lane_blocks_kernel, j0=j0, nj=nj),
        grid=(m // tm, n // tn), in_specs=in_specs, out_specs=[out_spec, blk_spec],
        out_shape=[out_shape, jax.ShapeDtypeStruct((ncols // V7X_LANES, m, V7X_LANES), BF16)],
        compiler_params=params, name=name)(a, b)


MOBA_ADD_ROW = ATTN_HEAD_DIM + V7X_BF16_SUBLANES
MOBA_PIECES = 3


def _split_bf16(x):
    hi = x.astype(BF16).astype(F32)
    mid = (x - hi).astype(BF16).astype(F32)
    return hi, mid, x - hi - mid


def _moba_kernel(q_ref, k_ref, v_ref, z_ref, qg_ref, kg_ref, slope_ref, o_ref,
                 kn_ref, vt_ref, kmean_ref, rhs_ref, s_ref, acc_ref, *, n_blocks, kgroup, qgroup):
    it = pl.program_id(2)
    bs, hd, sub = MOBA_BLOCK, ATTN_HEAD_DIM, V7X_BF16_SUBLANES
    kt, qt = kgroup * bs, qgroup * bs
    n_kgroups = n_blocks // kgroup
    strip = min(qt, MOBA_QUERY_STRIP)
    slope2 = slope_ref[0][:, :1] * LOG2E

    @pl.when(it == 0)
    def _():
        key_off = lax.broadcasted_iota(jnp.int32, (bs, hd), 0).astype(F32)
        lane = lax.broadcasted_iota(jnp.int32, (bs, hd), 1)
        add_lane = lane - (MOBA_ADD_ROW - hd)
        in_add = jnp.logical_and(add_lane >= 0, add_lane < MOBA_PIECES * n_blocks)
        ones_rows = jnp.where(lax.broadcasted_iota(jnp.int32, (sub, kt), 0) == 0, 1.0, 0.0).astype(BF16)

        def prep(g, carry):
            for bi in range(kgroup):
                c = g * kgroup + bi
                rows = pl.ds(pl.multiple_of(c * bs, bs), bs)
                kb = k_ref[rows, :].astype(F32)
                ms = jnp.mean(kb * kb, axis=-1, keepdims=True)
                kn = kb * lax.rsqrt(ms + EPS) * kg_ref[...]
                k_aug = jnp.where(lane < MOBA_PIECES, key_off,
                                  jnp.where(jnp.logical_and(in_add, add_lane % n_blocks == c), 1.0, 0.0))
                kn_ref[g, bi * bs:(bi + 1) * bs, :] = jnp.concatenate([kn.astype(BF16), k_aug.astype(BF16)], axis=1)
                kmean_ref[pl.ds(c, 1), :] = jnp.mean(kn, axis=0, keepdims=True)
                vt_ref[g, :hd, bi * bs:(bi + 1) * bs] = v_ref[rows, :].astype(F32).T.astype(BF16)
            vt_ref[g, hd:, :] = ones_rows
            return carry

        lax.fori_loop(0, n_kgroups, prep, 0)

    q = q_ref[...].astype(F32)
    ms = jnp.mean(q * q, axis=-1, keepdims=True)
    qn_t = (q * lax.rsqrt(ms + EPS) * qg_ref[...]).T

    gate = jnp.dot(kmean_ref[...], qn_t, preferred_element_type=F32, precision=lax.Precision.HIGHEST)
    blk = lax.broadcasted_iota(jnp.int32, gate.shape, 0)
    own = it * qgroup + lax.broadcasted_iota(jnp.int32, gate.shape, 1) // bs
    blk_f = blk.astype(F32)
    past = blk < own
    g = jnp.where(past, gate, NEG)
    sel = jnp.zeros(gate.shape, jnp.bool_)
    for _ in range(min(MOBA_TOP_K, n_blocks)):
        top = jnp.max(g, axis=0, keepdims=True)
        first = jnp.min(jnp.where(g == top, blk_f, float(n_blocks)), axis=0, keepdims=True)
        pick = blk_f == first
        sel = jnp.logical_or(sel, pick)
        g = jnp.where(pick, -jnp.inf, g)
    sel = jnp.logical_and(sel, past)
    add = jnp.where(sel, slope2 * (bs * (blk - own)).astype(F32), jnp.where(blk == own, 0.0, NEG))

    row = lax.broadcasted_iota(jnp.int32, (sub, qt), 0)
    s_hi, s_mid, s_lo = _split_bf16(slope2)
    slope_rows = jnp.where(row == 0, s_hi, jnp.where(row == 1, s_mid, jnp.where(row == 2, s_lo, 0.0)))
    pad = jnp.zeros((2 * hd - MOBA_ADD_ROW - MOBA_PIECES * n_blocks, qt), F32)
    rhs_ref[...] = jnp.concatenate([qn_t * (hd ** -0.5 * LOG2E), slope_rows, *_split_bf16(add), pad],
                                   axis=0).astype(BF16)

    def scores_to(slot, gi, causal_group=None):
        s = jnp.dot(kn_ref[gi], rhs_ref[...], preferred_element_type=F32)
        top = None
        for bi in range(kgroup):
            part = s[bi * bs:(bi + 1) * bs]
            if causal_group is not None:
                d0 = (causal_group * kgroup + bi) * bs
                tri = lax.broadcasted_iota(jnp.int32, (bs, bs), 0) <= lax.broadcasted_iota(jnp.int32, (bs, bs), 1)
                pieces = [part[:, :d0], jnp.where(tri, part[:, d0:d0 + bs], NEG), part[:, d0 + bs:]]
                part = jnp.concatenate([piece for piece in pieces if piece.shape[1]], axis=1)
            s_ref[slot, bi * bs:(bi + 1) * bs, :] = part
            top = part if top is None else jnp.maximum(top, part)
        return jnp.max(top, axis=0, keepdims=True)

    def accumulate(slot, gi, m, top, causal_group=None):
        m_new = jnp.maximum(m, top)
        alpha = jnp.exp2(m - m_new)
        for c0 in range(0, qt, strip):
            cols = slice(c0, c0 + strip)
            live = kgroup
            if causal_group is not None:
                live = min(max(c0 // bs - causal_group * kgroup + 1, 0), kgroup)
            if live == 0:
                continue
            p = jnp.exp2(s_ref[slot, :live * bs, cols] - m_new[:, cols]).astype(BF16)
            pv = jnp.dot(vt_ref[gi, :, :live * bs], p, preferred_element_type=F32)
            acc_ref[:, cols] = alpha[:, cols] * acc_ref[:, cols] + pv
        return m_new

    def pair(k, m):
        top0 = scores_to(0, 2 * k)
        top1 = scores_to(1, 2 * k + 1)
        return accumulate(1, 2 * k + 1, accumulate(0, 2 * k, m, top0), top1)

    acc_ref[...] = jnp.zeros(acc_ref.shape, F32)
    m = lax.fori_loop(0, it, pair, jnp.full((1, qt), 0.1 * NEG, F32))
    top0 = scores_to(0, 2 * it, causal_group=0)
    top1 = scores_to(1, 2 * it + 1, causal_group=1)
    accumulate(1, 2 * it + 1, accumulate(0, 2 * it, m, top0, causal_group=0), top1, causal_group=1)
    acc = acc_ref[...]
    o = (acc[:hd] / acc[hd:hd + 1]).T
    o_ref[...] = (o * jax.nn.silu(z_ref[...].astype(F32))).astype(o_ref.dtype)


def _moba(proj, q_gain, k_gain, bsz, seq, width):
    n_heads = width // ATTN_HEAD_DIM
    n_blocks = seq // MOBA_BLOCK
    hd, bs, sub = ATTN_HEAD_DIM, MOBA_BLOCK, V7X_BF16_SUBLANES
    kgroup, qgroup = MOBA_KEY_GROUP, MOBA_QUERY_GROUP
    assert qgroup == 2 * kgroup and n_blocks % qgroup == 0
    assert MOBA_ADD_ROW + MOBA_PIECES * n_blocks <= 2 * hd
    n_kgroups, n_qtiles, kt, qt = n_blocks // kgroup, n_blocks // qgroup, kgroup * bs, qgroup * bs
    slopes = jnp.asarray([[[2.0 ** (-8.0 * (h + 1) / n_heads)] * V7X_LANES] for h in range(n_heads)], F32)
    scratch = [
        pltpu.VMEM((n_kgroups, kt, 2 * hd), BF16),
        pltpu.VMEM((n_kgroups, hd + sub, kt), BF16),
        pltpu.VMEM((n_blocks, hd), F32),
        pltpu.VMEM((2 * hd, qt), BF16),
        pltpu.VMEM((2, kt, qt), F32),
        pltpu.VMEM((hd + sub, qt), F32),
    ]
    scratch_bytes = (n_kgroups * (kt * 2 * hd + (hd + sub) * kt) * 2 + n_blocks * hd * 4
                     + 2 * hd * qt * 2 + 2 * kt * qt * 4 + (hd + sub) * qt * 4)
    return pl.pallas_call(
        functools.partial(_moba_kernel, n_blocks=n_blocks, kgroup=kgroup, qgroup=qgroup),
        grid=(bsz, n_heads, n_qtiles),
        in_specs=[
            pl.BlockSpec((qt, hd), lambda b, h, i: (b * n_qtiles + i, h)),
            pl.BlockSpec((seq, hd), lambda b, h, i: (b, n_heads + h)),
            pl.BlockSpec((seq, hd), lambda b, h, i: (b, 2 * n_heads + h)),
            pl.BlockSpec((qt, hd), lambda b, h, i: (b * n_qtiles + i, 3 * n_heads + h)),
            pl.BlockSpec((1, hd), lambda b, h, i: (0, 0)),
            pl.BlockSpec((1, hd), lambda b, h, i: (0, 0)),
            pl.BlockSpec((1, 1, V7X_LANES), lambda b, h, i: (h, 0, 0)),
        ],
        out_specs=pl.BlockSpec((qt, hd), lambda b, h, i: (b * n_qtiles + i, h)),
        out_shape=jax.ShapeDtypeStruct((bsz * seq, width), BF16),
        scratch_shapes=scratch,
        compiler_params=_params([2 * seq * hd * 2, 3 * qt * hd * 2], scratch_bytes=scratch_bytes,
                                temp_bytes=4 * kt * qt * 4, n_axes=3),
        name="moba_attention",
    )(proj, proj, proj, proj, q_gain.reshape(1, hd).astype(F32), k_gain.reshape(1, hd).astype(F32), slopes)


def _zoh(ar, ai, log_dt):
    dt = jnp.exp(log_dt)
    mag = jnp.exp(dt * ar)
    abr = mag * jnp.cos(dt * ai)
    abi = mag * jnp.sin(dt * ai)
    den = ar * ar + ai * ai
    nr = abr - 1.0
    return abr, abi, (nr * ar + abi * ai) / den, (abi * ar - nr * ai) / den


def _s5_kernel(u_ref, ar_ref, ai_ref, ldt_ref, bbr_ref, bbi_ref, ccr_ref, cci_ref, d_ref,
               ard_ref, aid_ref, ldtd_ref, bdr_ref, bdi_ref, cdr_ref, cdi_ref, y_ref,
               p_ref, qt_ref, t_ref, dk_ref, stage_ref, uc_ref, xs_ref, *, n_batch, rows_per_batch, row_tile):
    lb, gn, chunk = V7X_LANES, STATE_LANES, SSM_CHUNK
    kdim = chunk * lb
    col_tile = min(kdim, S5_OUT_COL_TILE)
    scan_rows = V7X_F32_SUBLANES
    assert rows_per_batch % scan_rows == 0
    abr, abi, f_re, f_im = _zoh(ar_ref[0], ai_ref[0], ldt_ref[0])

    abr_d, abi_d, f_re_d, f_im_d = _zoh(ard_ref[0], aid_ref[0], ldtd_ref[0])
    bd_re = f_re_d * bdr_ref[0] - f_im_d * bdi_ref[0]
    bd_im = f_re_d * bdi_ref[0] + f_im_d * bdr_ref[0]
    cd = jnp.concatenate([cdr_ref[0], -cdi_ref[0]], axis=1)
    same_group = (lax.broadcasted_iota(jnp.int32, (lb, lb), 0) // SSM_GROUP
                  == lax.broadcasted_iota(jnp.int32, (lb, lb), 1) // SSM_GROUP)
    pr_d = jnp.ones(abr_d.shape, F32)
    pi_d = jnp.zeros(abr_d.shape, F32)
    for tau in range(chunk):
        lag = jnp.concatenate([bd_re * pr_d - bd_im * pi_d, bd_re * pi_d + bd_im * pr_d], axis=1)
        blocks = lax.dot_general(lag, cd, (((1,), (1,)), ((), ())), preferred_element_type=F32,
                                 precision=lax.Precision.HIGHEST)
        dk_ref[tau] = jnp.where(same_group, blocks, 0.0).astype(BF16)
        pr_d, pi_d = pr_d * abr_d - pi_d * abi_d, pr_d * abi_d + pi_d * abr_d

    row_g = lax.broadcasted_iota(jnp.int32, (lb, gn), 0) // SSM_GROUP
    col_g = lax.broadcasted_iota(jnp.int32, (lb, gn), 1) // SSM_STATE
    same = row_g == col_g
    bb_re, bb_im = bbr_ref[0], bbi_ref[0]
    bbar_re = jnp.where(same, f_re * bb_re - f_im * bb_im, 0.0)
    bbar_im = jnp.where(same, f_re * bb_im + f_im * bb_re, 0.0)
    cc_re = jnp.where(same, ccr_ref[0], 0.0)
    cc_im = jnp.where(same, cci_ref[0], 0.0)

    pr = jnp.ones((1, gn), F32)
    pi = jnp.zeros((1, gn), F32)
    for tau in range(chunk + 1):
        if tau < chunk:
            s = chunk - 1 - tau
            p_ref[s * lb:(s + 1) * lb, :] = jnp.concatenate(
                [bbar_re * pr - bbar_im * pi, bbar_re * pi + bbar_im * pr], axis=1).astype(BF16)
        if tau >= 1:
            t = tau - 1
            qt_ref[t * lb:(t + 1) * lb, :] = jnp.concatenate(
                [cc_re * pr - cc_im * pi, -(cc_re * pi + cc_im * pr)], axis=1).astype(BF16)
        if tau < chunk:
            pr, pi = pr * abr - pi * abi, pr * abi + pi * abr
    al_re, al_im = pr, pi

    zero = jnp.zeros((lb, lb), BF16)
    for s in range(chunk):
        for t in range(chunk):
            t_ref[s * lb:(s + 1) * lb, t * lb:(t + 1) * lb] = dk_ref[t - s] if t >= s else zero

    n_rows = n_batch * rows_per_batch
    steps_per_batch = rows_per_batch * chunk
    for b in range(n_batch):
        stage_ref[...] = u_ref[0, b * steps_per_batch:(b + 1) * steps_per_batch, :].astype(F32)
        for s in range(chunk):
            uc_ref[b * rows_per_batch:(b + 1) * rows_per_batch, s * lb:(s + 1) * lb] = (
                stage_ref[pl.ds(s, rows_per_batch, stride=chunk), :].astype(BF16))

    for r0 in range(0, n_rows, row_tile):
        xs_ref[r0:r0 + row_tile, :] = jnp.dot(uc_ref[r0:r0 + row_tile, :], p_ref[...], preferred_element_type=F32)

    def step(c, carry):
        out = []
        for b in range(n_batch):
            xr, xi = carry[b]
            rows = pl.ds(pl.multiple_of(b * rows_per_batch + c * scan_rows, scan_rows), scan_rows)
            inc = xs_ref[rows, :]
            starts_re, starts_im = [], []
            for k in range(scan_rows):
                starts_re.append(xr)
                starts_im.append(xi)
                xr, xi = (al_re * xr - al_im * xi + inc[k:k + 1, :gn], al_re * xi + al_im * xr + inc[k:k + 1, gn:])
            xs_ref[rows, :gn] = jnp.concatenate(starts_re, axis=0)
            xs_ref[rows, gn:] = jnp.concatenate(starts_im, axis=0)
            out.append((xr, xi))
        return tuple(out)

    x0 = jnp.zeros((1, gn), F32)
    lax.fori_loop(0, rows_per_batch // scan_rows, step, tuple((x0, x0) for _ in range(n_batch)))

    for b in range(n_batch):
        for r0 in range(0, rows_per_batch, row_tile):
            rows = slice(b * rows_per_batch + r0, b * rows_per_batch + r0 + row_tile)
            x_start = xs_ref[rows, :].astype(BF16)
            for c0 in range(0, kdim, col_tile):
                cols = slice(c0, c0 + col_tile)
                y = jnp.dot(uc_ref[rows, :c0 + col_tile], t_ref[:c0 + col_tile, cols], preferred_element_type=F32)
                y = y + lax.dot_general(x_start, qt_ref[cols, :], (((1,), (1,)), ((), ())),
                                        preferred_element_type=F32)
                y = jax.nn.gelu(y + d_ref[0][:, cols] * uc_ref[rows, cols].astype(F32))
                for s in range(c0 // lb, (c0 + col_tile) // lb):
                    stage_ref[pl.ds(r0 * chunk + s, row_tile, stride=chunk), :] = y[:, s * lb - c0:(s + 1) * lb - c0]
        y_ref[0, b * steps_per_batch:(b + 1) * steps_per_batch, :] = stage_ref[...].astype(y_ref.dtype)


def _s5(u_blocks, a_re, a_im, log_dt, b_re, b_im, c_re, c_im, d_skip, n_batch):
    nb, n_steps, lb = u_blocks.shape
    gpb, gn, chunk = GROUPS_PER_LANE_BLOCK, STATE_LANES, SSM_CHUNK
    n_rows, kdim = n_steps // chunk, chunk * lb
    rows_per_batch = n_rows // n_batch
    row_tile = _tile(rows_per_batch, 256)

    def lane_row(v):
        return v.astype(F32).reshape(nb, 1, gn)

    ldt = jnp.repeat(log_dt.astype(F32), SSM_STATE).reshape(nb, 1, gn)
    bt_re = jnp.tile(b_re.astype(F32).reshape(nb, gpb, SSM_STATE, SSM_GROUP).transpose(0, 3, 1, 2).reshape(nb, SSM_GROUP, gn), (1, gpb, 1))
    bt_im = jnp.tile(b_im.astype(F32).reshape(nb, gpb, SSM_STATE, SSM_GROUP).transpose(0, 3, 1, 2).reshape(nb, SSM_GROUP, gn), (1, gpb, 1))
    ct_re = jnp.tile(c_re.astype(F32).reshape(nb, lb, SSM_STATE), (1, 1, gpb))
    ct_im = jnp.tile(c_im.astype(F32).reshape(nb, lb, SSM_STATE), (1, 1, gpb))
    d_row = jnp.tile(d_skip.astype(F32).reshape(nb, 1, lb), (1, 1, chunk))

    def per_channel_rows(v):
        return jnp.repeat(v.astype(F32).reshape(nb, gpb, SSM_STATE), SSM_GROUP, axis=1)

    ldt_d = per_channel_rows(jnp.broadcast_to(log_dt[:, None], a_re.shape))
    bd_re = b_re.astype(F32).reshape(nb, gpb, SSM_STATE, SSM_GROUP).transpose(0, 1, 3, 2).reshape(nb, lb, SSM_STATE)
    bd_im = b_im.astype(F32).reshape(nb, gpb, SSM_STATE, SSM_GROUP).transpose(0, 1, 3, 2).reshape(nb, lb, SSM_STATE)
    cd_re = c_re.astype(F32).reshape(nb, lb, SSM_STATE)
    cd_im = c_im.astype(F32).reshape(nb, lb, SSM_STATE)
    dia = pl.BlockSpec((1, lb, SSM_STATE), lambda j: (j, 0, 0))

    vec = pl.BlockSpec((1, 1, gn), lambda j: (j, 0, 0))
    mat = pl.BlockSpec((1, lb, gn), lambda j: (j, 0, 0))
    blk = pl.BlockSpec((1, n_steps, lb), lambda j: (j, 0, 0))
    scratch = [
        pltpu.VMEM((kdim, 2 * gn), BF16),
        pltpu.VMEM((kdim, 2 * gn), BF16),
        pltpu.VMEM((kdim, kdim), BF16),
        pltpu.VMEM((chunk, lb, lb), BF16),
        pltpu.VMEM((n_steps // n_batch, lb), F32),
        pltpu.VMEM((n_rows, kdim), BF16),
        pltpu.VMEM((n_rows, 2 * gn), F32),
    ]
    scratch_bytes = (2 * kdim * 2 * gn * 2 + kdim * kdim * 2 + chunk * lb * lb * 2 + n_steps // n_batch * lb * 4
                     + n_rows * kdim * 2 + n_rows * 2 * gn * 4)
    return pl.pallas_call(
        functools.partial(_s5_kernel, n_batch=n_batch, rows_per_batch=rows_per_batch, row_tile=row_tile),
        grid=(nb,),
        in_specs=[blk, vec, vec, vec, mat, mat, mat, mat, pl.BlockSpec((1, 1, kdim), lambda j: (j, 0, 0))] + [dia] * 7,
        out_specs=blk,
        out_shape=jax.ShapeDtypeStruct((nb, n_steps, lb), BF16),
        scratch_shapes=scratch,
        compiler_params=_params([n_steps * lb * 2, n_steps * lb * 2, 4 * lb * gn * 4], scratch_bytes=scratch_bytes,
                                temp_bytes=4 * row_tile * kdim * 4),
        name="s5_scan",
    )(u_blocks, lane_row(a_re), lane_row(a_im), ldt, bt_re, bt_im, ct_re, ct_im, d_row,
      per_channel_rows(a_re), per_channel_rows(a_im), ldt_d, bd_re, bd_im, cd_re, cd_im)


def _glu_kernel(y_ref, ycol_ref, w_ref, b_ref, z_ref, o_ref):
    y = jnp.concatenate([y_ref[k] for k in range(y_ref.shape[0])], axis=1)
    ycol = jnp.concatenate([ycol_ref[k] for k in range(ycol_ref.shape[0])], axis=1).astype(F32)
    a = jnp.dot(y, w_ref[...], preferred_element_type=F32) + b_ref[...]
    z = z_ref[...].astype(F32)
    o_ref[...] = (ycol * z / ((1.0 + jnp.exp(-a)) * (1.0 + jnp.exp(-z)))).astype(o_ref.dtype)


def _glu(y_blocks, w_glu, b_glu, proj, z_col0, tm, tn):
    nb, n, lb = y_blocks.shape
    width = nb * lb
    tm, tn = _tile(n, tm), _tile(width, tn)
    return pl.pallas_call(
        _glu_kernel,
        grid=(n // tm, width // tn),
        in_specs=[
            pl.BlockSpec((nb, tm, lb), lambda i, j: (0, i, 0)),
            pl.BlockSpec((tn // lb, tm, lb), lambda i, j: (j, i, 0)),
            pl.BlockSpec((width, tn), lambda i, j: (0, j)),
            pl.BlockSpec((1, tn), lambda i, j: (0, j)),
            pl.BlockSpec((tm, tn), lambda i, j: (i, z_col0 // tn + j)),
        ],
        out_specs=pl.BlockSpec((tm, tn), lambda i, j: (i, j)),
        out_shape=jax.ShapeDtypeStruct((n, width), BF16),
        compiler_params=_params([tm * width * 2, tm * tn * 2, width * tn * 2, tm * tn * 2, tm * tn * 2],
                                temp_bytes=tm * width * 2 + 3 * tm * tn * 4, n_axes=2),
        name="s5_glu",
    )(y_blocks, y_blocks, w_glu, b_glu.reshape(1, width).astype(F32), proj)


def _memattn_kernel(q_ref, z_ref, kv_ref, qg_ref, kg_ref, o_ref, *, width):
    dm = width // MEM_HEADS
    for hd in range(MEM_HEADS):
        cols = slice(hd * dm, (hd + 1) * dm)
        q = q_ref[:, cols].astype(F32)
        qn = q * lax.rsqrt(jnp.mean(q * q, axis=-1, keepdims=True) + EPS) * qg_ref[...]
        k = kv_ref[:, cols].astype(F32)
        kn = k * lax.rsqrt(jnp.mean(k * k, axis=-1, keepdims=True) + EPS) * kg_ref[...]
        v = kv_ref[:, width + hd * dm:width + (hd + 1) * dm]
        s = lax.dot_general(qn.astype(BF16), kn.astype(BF16), (((1,), (1,)), ((), ())),
                            preferred_element_type=F32) * (dm ** -0.5)
        p = jnp.exp(s - jnp.max(s, axis=-1, keepdims=True))
        l = jnp.sum(p, axis=-1, keepdims=True)
        o = jnp.dot(p.astype(BF16), v, preferred_element_type=F32) / l
        o_ref[:, cols] = (o * jax.nn.silu(z_ref[:, cols].astype(F32))).astype(o_ref.dtype)


def _memattn(proj, kv, q_gain, k_gain, bsz, seq, width, tq):
    dm = width // MEM_HEADS
    n_mem = kv.shape[0] // bsz
    tq = _tile(seq, tq)
    nq = seq // tq
    return pl.pallas_call(
        functools.partial(_memattn_kernel, width=width),
        grid=(bsz, nq),
        in_specs=[
            pl.BlockSpec((tq, width), lambda b, i: (b * nq + i, 6)),
            pl.BlockSpec((tq, width), lambda b, i: (b * nq + i, 7)),
            pl.BlockSpec((n_mem, 2 * width), lambda b, i: (b, 0)),
            pl.BlockSpec((1, dm), lambda b, i: (0, 0)),
            pl.BlockSpec((1, dm), lambda b, i: (0, 0)),
        ],
        out_specs=pl.BlockSpec((tq, width), lambda b, i: (b * nq + i, 0)),
        out_shape=jax.ShapeDtypeStruct((bsz * seq, width), BF16),
        compiler_params=_params([3 * tq * width * 2, n_mem * 2 * width * 2], temp_bytes=8 * tq * dm * 4, n_axes=2),
        name="memory_attention",
    )(proj, proj, kv, q_gain.reshape(1, dm).astype(F32), k_gain.reshape(1, dm).astype(F32))


def _merge_kernel(ya_ref, ys_ref, yc_ref, wa_ref, ws_ref, wc_ref, ga_ref, gs_ref, gc_ref, o_ref):
    def term(y_ref, w_ref, g_ref):
        return jax.nn.sigmoid(g_ref[...].astype(F32)) * jnp.dot(y_ref[...], w_ref[...], preferred_element_type=F32)

    o_ref[...] = (term(ya_ref, wa_ref, ga_ref) + term(ys_ref, ws_ref, gs_ref)
                  + term(yc_ref, wc_ref, gc_ref)).astype(o_ref.dtype)


def _merge(y_a, y_s, y_c, w_a, w_s, w_c, proj, g_col0, tm, tn):
    n, width = y_a.shape
    d = w_a.shape[1]
    tm, tn = _tile(n, tm), _tile(d, tn)
    y_spec = pl.BlockSpec((tm, width), lambda i, j: (i, 0))
    w_spec = pl.BlockSpec((width, tn), lambda i, j: (0, j))

    def g_spec(branch):
        return pl.BlockSpec((tm, tn), lambda i, j: (i, (g_col0 + branch * d) // tn + j))

    return pl.pallas_call(
        _merge_kernel,
        grid=(n // tm, d // tn),
        in_specs=[y_spec] * 3 + [w_spec] * 3 + [g_spec(0), g_spec(1), g_spec(2)],
        out_specs=pl.BlockSpec((tm, tn), lambda i, j: (i, j)),
        out_shape=jax.ShapeDtypeStruct((n, d), BF16),
        compiler_params=_params([3 * tm * width * 2, 3 * width * tn * 2, 4 * tm * tn * 2], temp_bytes=4 * tm * tn * 4, n_axes=2),
        name="branch_merge",
    )(y_a, y_s, y_c, w_a, w_s, w_c, proj, proj, proj)


def _outproj_kernel(m_ref, w_ref, x_ref, o_ref):
    o_ref[...] = x_ref[...] + jnp.dot(m_ref[...], w_ref[...].astype(BF16), preferred_element_type=F32)


def _outproj(merged, w_out, x, tm, tn):
    n, d = merged.shape
    tm, tn = _tile(n, tm), _tile(d, tn)
    return pl.pallas_call(
        _outproj_kernel,
        grid=(n // tm, d // tn),
        in_specs=[
            pl.BlockSpec((tm, d), lambda i, j: (i, 0)),
            pl.BlockSpec((d, tn), lambda i, j: (0, j)),
            pl.BlockSpec((tm, tn), lambda i, j: (i, j)),
        ],
        out_specs=pl.BlockSpec((tm, tn), lambda i, j: (i, j)),
        out_shape=jax.ShapeDtypeStruct((n, d), F32),
        compiler_params=_params([tm * d * 2, d * tn * 4, 2 * tm * tn * 4], temp_bytes=d * tn * 2 + tm * tn * 4, n_axes=2),
        name="out_projection",
    )(merged, w_out, x)


def kernel(x, mem, w_in, g_norm, g_mem, w_mem_kv, q_gain_a, k_gain_a, q_gain_c, k_gain_c, ssm_a_re, ssm_a_im, ssm_log_dt, ssm_b_re, ssm_b_im, ssm_c_re, ssm_c_im, ssm_d, w_glu, b_glu, w_br_a, w_br_s, w_br_c, w_out):
    bsz, seq, d_model = x.shape
    width = w_glu.shape[0]
    n_tok = bsz * seq
    n_mem = mem.shape[1]
    assert seq % MOBA_BLOCK == 0 and seq % SSM_CHUNK == 0 and width % V7X_LANES == 0
    assert w_in.shape == (d_model, 8 * width + 3 * d_model)

    x2 = x.reshape(n_tok, d_model)
    h = _rmsnorm(x2, g_norm, rows=256)
    proj, u = _matmul(h, w_in.astype(BF16), 1024, _tile(width, 1024), "in_projection",
                      lane_block_cols=(4 * width, width))

    m = _rmsnorm(mem.reshape(bsz * n_mem, d_model), g_mem, rows=256)
    kv = _matmul(m, w_mem_kv, 512, 512, "memory_kv_projection")

    y_a = _moba(proj, q_gain_a, k_gain_a, bsz, seq, width)

    y_g = _s5(u, ssm_a_re, ssm_a_im, ssm_log_dt, ssm_b_re, ssm_b_im, ssm_c_re, ssm_c_im, ssm_d, bsz)
    y_s = _glu(y_g, w_glu.astype(BF16), b_glu, proj, 5 * width, 512, width)

    y_c = _memattn(proj, kv, q_gain_c, k_gain_c, bsz, seq, width, 512)

    merged = _merge(y_a, y_s, y_c, w_br_a.astype(BF16), w_br_s.astype(BF16), w_br_c.astype(BF16), proj,
                    8 * width, 1024, 512)
    out = _outproj(merged, w_out, x2, 1024, 512)
    return out.reshape(bsz, seq, d_model)
```

```python
import functools

import jax
import jax.numpy as jnp
from jax import lax
from jax.experimental import pallas as pl
from jax.experimental.pallas import tpu as pltpu

F32 = jnp.float32
BF16 = jnp.bfloat16

V7X_LANES = 128
V7X_F32_SUBLANES = 8
V7X_BF16_SUBLANES = 16
V7X_VMEM_BYTES = 64 * 1024 * 1024
V7X_VMEM_RESERVE_BYTES = 6 * 1024 * 1024

ATTN_HEAD_DIM = 128
MOBA_BLOCK = 256
MOBA_TOP_K = 3
SSM_GROUP = 16
SSM_STATE = 64
MEM_HEADS = 4
EPS = 1e-6
NEG = -1e30
LOG2E = 1.4426950408889634

MOBA_KEY_GROUP = 2
MOBA_QUERY_GROUP = 4
MOBA_QUERY_STRIP = 256
SSM_CHUNK = 16
S5_OUT_COL_TILE = 512
GROUPS_PER_LANE_BLOCK = V7X_LANES // SSM_GROUP
STATE_LANES = GROUPS_PER_LANE_BLOCK * SSM_STATE


def _tile(n, pref):
    t = min(n, pref)
    while n % t:
        t -= V7X_LANES
    assert t > 0
    return t


def _params(block_bytes, scratch_bytes=0, temp_bytes=0, n_axes=1):
    need = 2 * sum(block_bytes) + scratch_bytes + temp_bytes
    limit = min(max(need, 16 * 1024 * 1024), V7X_VMEM_BYTES - V7X_VMEM_RESERVE_BYTES)
    return pltpu.CompilerParams(dimension_semantics=("arbitrary",) * n_axes, vmem_limit_bytes=int(limit))


def _rmsnorm_kernel(x_ref, g_ref, o_ref):
    x = x_ref[...].astype(F32)
    ms = jnp.mean(x * x, axis=-1, keepdims=True)
    o_ref[...] = (x * lax.rsqrt(ms + EPS) * g_ref[...]).astype(o_ref.dtype)


def _rmsnorm(x, gain, rows):
    n, d = x.shape
    tm = _tile(n, rows)
    return pl.pallas_call(
        _rmsnorm_kernel,
        grid=(n // tm,),
        in_specs=[pl.BlockSpec((tm, d), lambda i: (i, 0)), pl.BlockSpec((1, d), lambda i: (0, 0))],
        out_specs=pl.BlockSpec((tm, d), lambda i: (i, 0)),
        out_shape=jax.ShapeDtypeStruct((n, d), BF16),
        compiler_params=_params([tm * d * 4, tm * d * 2], temp_bytes=2 * tm * d * 4),
        name="rmsnorm",
    )(x, gain.reshape(1, d).astype(F32))


def _mm_kernel(a_ref, b_ref, o_ref):
    o_ref[...] = jnp.dot(a_ref[...], b_ref[...].astype(BF16), preferred_element_type=F32).astype(o_ref.dtype)


def _mm_lane_blocks_kernel(a_ref, b_ref, o_ref, blk_ref, *, j0, nj):
    res = jnp.dot(a_ref[...], b_ref[...], preferred_element_type=F32).astype(o_ref.dtype)
    o_ref[...] = res
    j = pl.program_id(1)

    @pl.when(jnp.logical_and(j >= j0, j < j0 + nj))
    def _():
        for k in range(blk_ref.shape[0]):
            blk_ref[k] = res[:, k * V7X_LANES:(k + 1) * V7X_LANES]


def _matmul(a, b, tm, tn, name, lane_block_cols=None):
    m, k = a.shape
    _, n = b.shape
    tm, tn = _tile(m, tm), _tile(n, tn)
    in_specs = [pl.BlockSpec((tm, k), lambda i, j: (i, 0)), pl.BlockSpec((k, tn), lambda i, j: (0, j))]
    out_spec = pl.BlockSpec((tm, tn), lambda i, j: (i, j))
    out_shape = jax.ShapeDtypeStruct((m, n), BF16)
    b_cast_bytes = k * tn * 2 if b.dtype != BF16 else 0
    params = _params([tm * k * 2, k * tn * b.dtype.itemsize, 2 * tm * tn * 2],
                     temp_bytes=tm * tn * (4 + 4 + 2) + b_cast_bytes, n_axes=2)
    if lane_block_cols is None:
        return pl.pallas_call(_mm_kernel, grid=(m // tm, n // tn), in_specs=in_specs, out_specs=out_spec,
                              out_shape=out_shape, compiler_params=params, name=name)(a, b)
    col0, ncols = lane_block_cols
    assert col0 % tn == 0 and ncols % tn == 0
    j0, nj, per = col0 // tn, ncols // tn, tn // V7X_LANES
    blk_spec = pl.BlockSpec((per, tm, V7X_LANES), lambda i, j: (jnp.clip(j - j0, 0, nj - 1), i, 0))
    return pl.pallas_call(
        functools.partial(_mm_lane_blocks_kernel, j0=j0, nj=nj),
        grid=(m // tm, n // tn), in_specs=in_specs, out_specs=[out_spec, blk_spec],
        out_shape=[out_shape, jax.ShapeDtypeStruct((ncols // V7X_LANES, m, V7X_LANES), BF16)],
        compiler_params=params, name=name)(a, b)


MOBA_ADD_ROW = ATTN_HEAD_DIM + V7X_BF16_SUBLANES
MOBA_PIECES = 3


def _split_bf16(x):
    hi = x.astype(BF16).astype(F32)
    mid = (x - hi).astype(BF16).astype(F32)
    return hi, mid, x - hi - mid


def _moba_kernel(q_ref, q_next_ref, k_ref, v_ref, z_ref, qg_ref, kg_ref, slope_ref, o_ref,
                 kn_ref, vt_ref, kmean_ref, rhs_ref, s_ref, acc_ref, *, n_blocks, kgroup, qgroup):
    it = pl.program_id(2)
    bs, hd, sub = MOBA_BLOCK, ATTN_HEAD_DIM, V7X_BF16_SUBLANES
    kt, qt = kgroup * bs, qgroup * bs
    n_kgroups = n_blocks // kgroup
    strip = min(qt, MOBA_QUERY_STRIP)
    slope2 = slope_ref[0][:, :1] * LOG2E

    @pl.when(it == 0)
    def _():
        key_off = lax.broadcasted_iota(jnp.int32, (bs, hd), 0).astype(F32)
        lane = lax.broadcasted_iota(jnp.int32, (bs, hd), 1)
        add_lane = lane - (MOBA_ADD_ROW - hd)
        in_add = jnp.logical_and(add_lane >= 0, add_lane < MOBA_PIECES * n_blocks)
        ones_rows = jnp.where(lax.broadcasted_iota(jnp.int32, (sub, kt), 0) == 0, 1.0, 0.0).astype(BF16)

        def prep(g, carry):
            for bi in range(kgroup):
                c = g * kgroup + bi
                rows = pl.ds(pl.multiple_of(c * bs, bs), bs)
                kb = k_ref[rows, :].astype(F32)
                ms = jnp.mean(kb * kb, axis=-1, keepdims=True)
                kn = kb * lax.rsqrt(ms + EPS) * kg_ref[...]
                k_aug = jnp.where(lane < MOBA_PIECES, key_off,
                                  jnp.where(jnp.logical_and(in_add, add_lane % n_blocks == c), 1.0, 0.0))
                kn_ref[g, bi * bs:(bi + 1) * bs, :] = jnp.concatenate([kn.astype(BF16), k_aug.astype(BF16)], axis=1)
                kmean_ref[pl.ds(c, 1), :] = jnp.mean(kn, axis=0, keepdims=True)
                vt_ref[g, :hd, bi * bs:(bi + 1) * bs] = v_ref[rows, :].astype(F32).T.astype(BF16)
            vt_ref[g, hd:, :] = ones_rows
            return carry

        lax.fori_loop(0, n_kgroups, prep, 0)

    def prepare_queries(q_block_ref, tile, rhs_slot):
        q = q_block_ref[...].astype(F32)
        ms = jnp.mean(q * q, axis=-1, keepdims=True)
        qn_t = (q * lax.rsqrt(ms + EPS) * qg_ref[...]).T

        gate = jnp.dot(kmean_ref[...], qn_t, preferred_element_type=F32, precision=lax.Precision.HIGHEST)
        blk = lax.broadcasted_iota(jnp.int32, gate.shape, 0)
        own = tile * qgroup + lax.broadcasted_iota(jnp.int32, gate.shape, 1) // bs
        blk_f = blk.astype(F32)
        past = blk < own
        g = jnp.where(past, gate, NEG)
        sel = jnp.zeros(gate.shape, jnp.bool_)
        for _ in range(min(MOBA_TOP_K, n_blocks)):
            top = jnp.max(g, axis=0, keepdims=True)
            first = jnp.min(jnp.where(g == top, blk_f, float(n_blocks)), axis=0, keepdims=True)
            pick = blk_f == first
            sel = jnp.logical_or(sel, pick)
            g = jnp.where(pick, -jnp.inf, g)
        sel = jnp.logical_and(sel, past)
        add = jnp.where(sel, slope2 * (bs * (blk - own)).astype(F32), jnp.where(blk == own, 0.0, NEG))

        row = lax.broadcasted_iota(jnp.int32, (sub, qt), 0)
        s_hi, s_mid, s_lo = _split_bf16(slope2)
        slope_rows = jnp.where(row == 0, s_hi, jnp.where(row == 1, s_mid, jnp.where(row == 2, s_lo, 0.0)))
        pad = jnp.zeros((2 * hd - MOBA_ADD_ROW - MOBA_PIECES * n_blocks, qt), F32)
        rhs_ref[rhs_slot] = jnp.concatenate([qn_t * (hd ** -0.5 * LOG2E), slope_rows, *_split_bf16(add), pad],
                                            axis=0).astype(BF16)

    @pl.when(it == 0)
    def _():
        prepare_queries(q_ref, 0, 0)

    rhs_now = it % 2

    def scores_to(slot, gi, causal_group=None):
        s = jnp.dot(kn_ref[gi], rhs_ref[rhs_now], preferred_element_type=F32)
        top = None
        for bi in range(kgroup):
            part = s[bi * bs:(bi + 1) * bs]
            if causal_group is not None:
                d0 = (causal_group * kgroup + bi) * bs
                tri = lax.broadcasted_iota(jnp.int32, (bs, bs), 0) <= lax.broadcasted_iota(jnp.int32, (bs, bs), 1)
                pieces = [part[:, :d0], jnp.where(tri, part[:, d0:d0 + bs], NEG), part[:, d0 + bs:]]
                part = jnp.concatenate([piece for piece in pieces if piece.shape[1]], axis=1)
            s_ref[slot, bi * bs:(bi + 1) * bs, :] = part
            top = part if top is None else jnp.maximum(top, part)
        return jnp.max(top, axis=0, keepdims=True)

    def accumulate(slot, gi, m, top, causal_group=None):
        m_new = jnp.maximum(m, top)
        alpha = jnp.exp2(m - m_new)
        for c0 in range(0, qt, strip):
            cols = slice(c0, c0 + strip)
            live = kgroup
            if causal_group is not None:
                live = min(max(c0 // bs - causal_group * kgroup + 1, 0), kgroup)
            if live == 0:
                continue
            p = jnp.exp2(s_ref[slot, :live * bs, cols] - m_new[:, cols]).astype(BF16)
            pv = jnp.dot(vt_ref[gi, :, :live * bs], p, preferred_element_type=F32)
            acc_ref[:, cols] = alpha[:, cols] * acc_ref[:, cols] + pv
        return m_new

    def pair(k, m):
        top0 = scores_to(0, 2 * k)
        top1 = scores_to(1, 2 * k + 1)
        return accumulate(1, 2 * k + 1, accumulate(0, 2 * k, m, top0), top1)

    acc_ref[...] = jnp.zeros(acc_ref.shape, F32)
    m = lax.fori_loop(0, it, pair, jnp.full((1, qt), 0.1 * NEG, F32))
    top0 = scores_to(0, 2 * it, causal_group=0)
    top1 = scores_to(1, 2 * it + 1, causal_group=1)
    accumulate(1, 2 * it + 1, accumulate(0, 2 * it, m, top0, causal_group=0), top1, causal_group=1)
    prepare_queries(q_next_ref, it + 1, 1 - rhs_now)
    acc = acc_ref[...]
    o = (acc[:hd] / acc[hd:hd + 1]).T
    o_ref[...] = (o * jax.nn.silu(z_ref[...].astype(F32))).astype(o_ref.dtype)


def _moba(proj, q_gain, k_gain, bsz, seq, width):
    n_heads = width // ATTN_HEAD_DIM
    n_blocks = seq // MOBA_BLOCK
    hd, bs, sub = ATTN_HEAD_DIM, MOBA_BLOCK, V7X_BF16_SUBLANES
    kgroup, qgroup = MOBA_KEY_GROUP, MOBA_QUERY_GROUP
    assert qgroup == 2 * kgroup and n_blocks % qgroup == 0
    assert MOBA_ADD_ROW + MOBA_PIECES * n_blocks <= 2 * hd
    n_kgroups, n_qtiles, kt, qt = n_blocks // kgroup, n_blocks // qgroup, kgroup * bs, qgroup * bs
    slopes = jnp.asarray([[[2.0 ** (-8.0 * (h + 1) / n_heads)] * V7X_LANES] for h in range(n_heads)], F32)
    scratch = [
        pltpu.VMEM((n_kgroups, kt, 2 * hd), BF16),
        pltpu.VMEM((n_kgroups, hd + sub, kt), BF16),
        pltpu.VMEM((n_blocks, hd), F32),
        pltpu.VMEM((2, 2 * hd, qt), BF16),
        pltpu.VMEM((2, kt, qt), F32),
        pltpu.VMEM((hd + sub, qt), F32),
    ]
    scratch_bytes = (n_kgroups * (kt * 2 * hd + (hd + sub) * kt) * 2 + n_blocks * hd * 4
                     + 2 * 2 * hd * qt * 2 + 2 * kt * qt * 4 + (hd + sub) * qt * 4)
    return pl.pallas_call(
        functools.partial(_moba_kernel, n_blocks=n_blocks, kgroup=kgroup, qgroup=qgroup),
        grid=(bsz, n_heads, n_qtiles),
        in_specs=[
            pl.BlockSpec((qt, hd), lambda b, h, i: (b * n_qtiles + i, h)),
            pl.BlockSpec((qt, hd), lambda b, h, i: (b * n_qtiles + jnp.minimum(i + 1, n_qtiles - 1), h)),
            pl.BlockSpec((seq, hd), lambda b, h, i: (b, n_heads + h)),
            pl.BlockSpec((seq, hd), lambda b, h, i: (b, 2 * n_heads + h)),
            pl.BlockSpec((qt, hd), lambda b, h, i: (b * n_qtiles + i, 3 * n_heads + h)),
            pl.BlockSpec((1, hd), lambda b, h, i: (0, 0)),
            pl.BlockSpec((1, hd), lambda b, h, i: (0, 0)),
            pl.BlockSpec((1, 1, V7X_LANES), lambda b, h, i: (h, 0, 0)),
        ],
        out_specs=pl.BlockSpec((qt, hd), lambda b, h, i: (b * n_qtiles + i, h)),
        out_shape=jax.ShapeDtypeStruct((bsz * seq, width), BF16),
        scratch_shapes=scratch,
        compiler_params=_params([2 * seq * hd * 2, 4 * qt * hd * 2], scratch_bytes=scratch_bytes,
                                temp_bytes=4 * kt * qt * 4, n_axes=3),
        name="moba_attention",
    )(proj, proj, proj, proj, proj, q_gain.reshape(1, hd).astype(F32), k_gain.reshape(1, hd).astype(F32), slopes)


def _zoh(ar, ai, log_dt):
    dt = jnp.exp(log_dt)
    mag = jnp.exp(dt * ar)
    abr = mag * jnp.cos(dt * ai)
    abi = mag * jnp.sin(dt * ai)
    den = ar * ar + ai * ai
    nr = abr - 1.0
    return abr, abi, (nr * ar + abi * ai) / den, (abi * ar - nr * ai) / den


def _s5_kernel(u_ref, ar_ref, ai_ref, ldt_ref, bbr_ref, bbi_ref, ccr_ref, cci_ref, d_ref,
               ard_ref, aid_ref, ldtd_ref, bdr_ref, bdi_ref, cdr_ref, cdi_ref, y_ref,
               p_ref, qt_ref, t_ref, dk_ref, stage_ref, uc_ref, xs_ref, *, n_batch, rows_per_batch, row_tile):
    lb, gn, chunk = V7X_LANES, STATE_LANES, SSM_CHUNK
    kdim = chunk * lb
    col_tile = min(kdim, S5_OUT_COL_TILE)
    scan_rows = V7X_F32_SUBLANES
    assert rows_per_batch % scan_rows == 0
    abr, abi, f_re, f_im = _zoh(ar_ref[0], ai_ref[0], ldt_ref[0])

    abr_d, abi_d, f_re_d, f_im_d = _zoh(ard_ref[0], aid_ref[0], ldtd_ref[0])
    bd_re = f_re_d * bdr_ref[0] - f_im_d * bdi_ref[0]
    bd_im = f_re_d * bdi_ref[0] + f_im_d * bdr_ref[0]
    cd = jnp.concatenate([cdr_ref[0], -cdi_ref[0]], axis=1)
    same_group = (lax.broadcasted_iota(jnp.int32, (lb, lb), 0) // SSM_GROUP
                  == lax.broadcasted_iota(jnp.int32, (lb, lb), 1) // SSM_GROUP)
    pr_d = jnp.ones(abr_d.shape, F32)
    pi_d = jnp.zeros(abr_d.shape, F32)
    for tau in range(chunk):
        lag = jnp.concatenate([bd_re * pr_d - bd_im * pi_d, bd_re * pi_d + bd_im * pr_d], axis=1)
        blocks = lax.dot_general(lag, cd, (((1,), (1,)), ((), ())), preferred_element_type=F32,
                                 precision=lax.Precision.HIGHEST)
        dk_ref[tau] = jnp.where(same_group, blocks, 0.0).astype(BF16)
        pr_d, pi_d = pr_d * abr_d - pi_d * abi_d, pr_d * abi_d + pi_d * abr_d

    row_g = lax.broadcasted_iota(jnp.int32, (lb, gn), 0) // SSM_GROUP
    col_g = lax.broadcasted_iota(jnp.int32, (lb, gn), 1) // SSM_STATE
    same = row_g == col_g
    bb_re, bb_im = bbr_ref[0], bbi_ref[0]
    bbar_re = jnp.where(same, f_re * bb_re - f_im * bb_im, 0.0)
    bbar_im = jnp.where(same, f_re * bb_im + f_im * bb_re, 0.0)
    cc_re = jnp.where(same, ccr_ref[0], 0.0)
    cc_im = jnp.where(same, cci_ref[0], 0.0)

    pr = jnp.ones((1, gn), F32)
    pi = jnp.zeros((1, gn), F32)
    for tau in range(chunk + 1):
        if tau < chunk:
            s = chunk - 1 - tau
            p_ref[s * lb:(s + 1) * lb, :] = jnp.concatenate(
                [bbar_re * pr - bbar_im * pi, bbar_re * pi + bbar_im * pr], axis=1).astype(BF16)
        if tau >= 1:
            t = tau - 1
            qt_ref[t * lb:(t + 1) * lb, :] = jnp.concatenate(
                [cc_re * pr - cc_im * pi, -(cc_re * pi + cc_im * pr)], axis=1).astype(BF16)
        if tau < chunk:
            pr, pi = pr * abr - pi * abi, pr * abi + pi * abr
    al_re, al_im = pr, pi

    zero = jnp.zeros((lb, lb), BF16)
    for s in range(chunk):
        for t in range(chunk):
            t_ref[s * lb:(s + 1) * lb, t * lb:(t + 1) * lb] = dk_ref[t - s] if t >= s else zero

    n_rows = n_batch * rows_per_batch
    steps_per_batch = rows_per_batch * chunk
    for b in range(n_batch):
        stage_ref[...] = u_ref[0, b * steps_per_batch:(b + 1) * steps_per_batch, :].astype(F32)
        for s in range(chunk):
            uc_ref[b * rows_per_batch:(b + 1) * rows_per_batch, s * lb:(s + 1) * lb] = (
                stage_ref[pl.ds(s, rows_per_batch, stride=chunk), :].astype(BF16))

    for r0 in range(0, n_rows, row_tile):
        xs_ref[r0:r0 + row_tile, :] = jnp.dot(uc_ref[r0:r0 + row_tile, :], p_ref[...], preferred_element_type=F32)

    def step(c, carry):
        out = []
        for b in range(n_batch):
            xr, xi = carry[b]
            rows = pl.ds(pl.multiple_of(b * rows_per_batch + c * scan_rows, scan_rows), scan_rows)
            inc = xs_ref[rows, :]
            starts_re, starts_im = [], []
            for k in range(scan_rows):
                starts_re.append(xr)
                starts_im.append(xi)
                xr, xi = (al_re * xr - al_im * xi + inc[k:k + 1, :gn], al_re * xi + al_im * xr + inc[k:k + 1, gn:])
            xs_ref[rows, :gn] = jnp.concatenate(starts_re, axis=0)
            xs_ref[rows, gn:] = jnp.concatenate(starts_im, axis=0)
            out.append((xr, xi))
        return tuple(out)

    x0 = jnp.zeros((1, gn), F32)
    lax.fori_loop(0, rows_per_batch // scan_rows, step, tuple((x0, x0) for _ in range(n_batch)))

    for b in range(n_batch):
        for r0 in range(0, rows_per_batch, row_tile):
            rows = slice(b * rows_per_batch + r0, b * rows_per_batch + r0 + row_tile)
            x_start = xs_ref[rows, :].astype(BF16)
            for c0 in range(0, kdim, col_tile):
                cols = slice(c0, c0 + col_tile)
                y = jnp.dot(uc_ref[rows, :c0 + col_tile], t_ref[:c0 + col_tile, cols], preferred_element_type=F32)
                y = y + lax.dot_general(x_start, qt_ref[cols, :], (((1,), (1,)), ((), ())),
                                        preferred_element_type=F32)
                y = jax.nn.gelu(y + d_ref[0][:, cols] * uc_ref[rows, cols].astype(F32))
                for s in range(c0 // lb, (c0 + col_tile) // lb):
                    stage_ref[pl.ds(r0 * chunk + s, row_tile, stride=chunk), :] = y[:, s * lb - c0:(s + 1) * lb - c0]
        y_ref[0, b * steps_per_batch:(b + 1) * steps_per_batch, :] = stage_ref[...].astype(y_ref.dtype)


def _s5(u_blocks, a_re, a_im, log_dt, b_re, b_im, c_re, c_im, d_skip, n_batch):
    nb, n_steps, lb = u_blocks.shape
    gpb, gn, chunk = GROUPS_PER_LANE_BLOCK, STATE_LANES, SSM_CHUNK
    n_rows, kdim = n_steps // chunk, chunk * lb
    rows_per_batch = n_rows // n_batch
    row_tile = _tile(rows_per_batch, 256)

    def lane_row(v):
        return v.astype(F32).reshape(nb, 1, gn)

    ldt = jnp.repeat(log_dt.astype(F32), SSM_STATE).reshape(nb, 1, gn)
    bt_re = jnp.tile(b_re.astype(F32).reshape(nb, gpb, SSM_STATE, SSM_GROUP).transpose(0, 3, 1, 2).reshape(nb, SSM_GROUP, gn), (1, gpb, 1))
    bt_im = jnp.tile(b_im.astype(F32).reshape(nb, gpb, SSM_STATE, SSM_GROUP).transpose(0, 3, 1, 2).reshape(nb, SSM_GROUP, gn), (1, gpb, 1))
    ct_re = jnp.tile(c_re.astype(F32).reshape(nb, lb, SSM_STATE), (1, 1, gpb))
    ct_im = jnp.tile(c_im.astype(F32).reshape(nb, lb, SSM_STATE), (1, 1, gpb))
    d_row = jnp.tile(d_skip.astype(F32).reshape(nb, 1, lb), (1, 1, chunk))

    def per_channel_rows(v):
        return jnp.repeat(v.astype(F32).reshape(nb, gpb, SSM_STATE), SSM_GROUP, axis=1)

    ldt_d = per_channel_rows(jnp.broadcast_to(log_dt[:, None], a_re.shape))
    bd_re = b_re.astype(F32).reshape(nb, gpb, SSM_STATE, SSM_GROUP).transpose(0, 1, 3, 2).reshape(nb, lb, SSM_STATE)
    bd_im = b_im.astype(F32).reshape(nb, gpb, SSM_STATE, SSM_GROUP).transpose(0, 1, 3, 2).reshape(nb, lb, SSM_STATE)
    cd_re = c_re.astype(F32).reshape(nb, lb, SSM_STATE)
    cd_im = c_im.astype(F32).reshape(nb, lb, SSM_STATE)
    dia = pl.BlockSpec((1, lb, SSM_STATE), lambda j: (j, 0, 0))

    vec = pl.BlockSpec((1, 1, gn), lambda j: (j, 0, 0))
    mat = pl.BlockSpec((1, lb, gn), lambda j: (j, 0, 0))
    blk = pl.BlockSpec((1, n_steps, lb), lambda j: (j, 0, 0))
    scratch = [
        pltpu.VMEM((kdim, 2 * gn), BF16),
        pltpu.VMEM((kdim, 2 * gn), BF16),
        pltpu.VMEM((kdim, kdim), BF16),
        pltpu.VMEM((chunk, lb, lb), BF16),
        pltpu.VMEM((n_steps // n_batch, lb), F32),
        pltpu.VMEM((n_rows, kdim), BF16),
        pltpu.VMEM((n_rows, 2 * gn), F32),
    ]
    scratch_bytes = (2 * kdim * 2 * gn * 2 + kdim * kdim * 2 + chunk * lb * lb * 2 + n_steps // n_batch * lb * 4
                     + n_rows * kdim * 2 + n_rows * 2 * gn * 4)
    return pl.pallas_call(
        functools.partial(_s5_kernel, n_batch=n_batch, rows_per_batch=rows_per_batch, row_tile=row_tile),
        grid=(nb,),
        in_specs=[blk, vec, vec, vec, mat, mat, mat, mat, pl.BlockSpec((1, 1, kdim), lambda j: (j, 0, 0))] + [dia] * 7,
        out_specs=blk,
        out_shape=jax.ShapeDtypeStruct((nb, n_steps, lb), BF16),
        scratch_shapes=scratch,
        compiler_params=_params([n_steps * lb * 2, n_steps * lb * 2, 4 * lb * gn * 4], scratch_bytes=scratch_bytes,
                                temp_bytes=4 * row_tile * kdim * 4),
        name="s5_scan",
    )(u_blocks, lane_row(a_re), lane_row(a_im), ldt, bt_re, bt_im, ct_re, ct_im, d_row,
      per_channel_rows(a_re), per_channel_rows(a_im), ldt_d, bd_re, bd_im, cd_re, cd_im)


def _glu_kernel(y_ref, w_ref, b_ref, z_ref, o_ref, *, col_tile):
    lb = y_ref.shape[2]
    y = jnp.concatenate([y_ref[k] for k in range(y_ref.shape[0])], axis=1)
    for c0 in range(0, o_ref.shape[1], col_tile):
        cols = slice(c0, c0 + col_tile)
        a = jnp.dot(y, w_ref[:, cols], preferred_element_type=F32) + b_ref[:, cols]
        z = z_ref[:, cols].astype(F32)
        ycol = jnp.concatenate([y_ref[k] for k in range(c0 // lb, (c0 + col_tile) // lb)], axis=1).astype(F32)
        o_ref[:, cols] = (ycol * z / ((1.0 + jnp.exp(-a)) * (1.0 + jnp.exp(-z)))).astype(o_ref.dtype)


def _glu(y_blocks, w_glu, b_glu, proj, z_col0, tm, col_tile):
    nb, n, lb = y_blocks.shape
    width = nb * lb
    tm, col_tile = _tile(n, tm), _tile(width, col_tile)
    assert z_col0 % width == 0
    return pl.pallas_call(
        functools.partial(_glu_kernel, col_tile=col_tile),
        grid=(n // tm,),
        in_specs=[
            pl.BlockSpec((nb, tm, lb), lambda i: (0, i, 0)),
            pl.BlockSpec((width, width), lambda i: (0, 0)),
            pl.BlockSpec((1, width), lambda i: (0, 0)),
            pl.BlockSpec((tm, width), lambda i: (i, z_col0 // width)),
        ],
        out_specs=pl.BlockSpec((tm, width), lambda i: (i, 0)),
        out_shape=jax.ShapeDtypeStruct((n, width), BF16),
        compiler_params=_params([tm * width * 2, width * width * 2, tm * width * 2, tm * width * 2],
                                temp_bytes=tm * width * 2 + 4 * tm * col_tile * 4),
        name="s5_glu",
    )(y_blocks, w_glu, b_glu.reshape(1, width).astype(F32), proj)


def _memattn_kernel(q_ref, z_ref, kv_ref, qg_ref, kg_ref, o_ref, *, width):
    dm = width // MEM_HEADS
    for hd in range(MEM_HEADS):
        cols = slice(hd * dm, (hd + 1) * dm)
        q = q_ref[:, cols].astype(F32)
        qn = q * lax.rsqrt(jnp.mean(q * q, axis=-1, keepdims=True) + EPS) * qg_ref[...]
        k = kv_ref[:, cols].astype(F32)
        kn = k * lax.rsqrt(jnp.mean(k * k, axis=-1, keepdims=True) + EPS) * kg_ref[...]
        v = kv_ref[:, width + hd * dm:width + (hd + 1) * dm]
        s = lax.dot_general(qn.astype(BF16), kn.astype(BF16), (((1,), (1,)), ((), ())),
                            preferred_element_type=F32) * (dm ** -0.5)
        p = jnp.exp(s - jnp.max(s, axis=-1, keepdims=True))
        l = jnp.sum(p, axis=-1, keepdims=True)
        o = jnp.dot(p.astype(BF16), v, preferred_element_type=F32) / l
        o_ref[:, cols] = (o * jax.nn.silu(z_ref[:, cols].astype(F32))).astype(o_ref.dtype)


def _memattn(proj, kv, q_gain, k_gain, bsz, seq, width, tq):
    dm = width // MEM_HEADS
    n_mem = kv.shape[0] // bsz
    tq = _tile(seq, tq)
    nq = seq // tq
    return pl.pallas_call(
        functools.partial(_memattn_kernel, width=width),
        grid=(bsz, nq),
        in_specs=[
            pl.BlockSpec((tq, width), lambda b, i: (b * nq + i, 6)),
            pl.BlockSpec((tq, width), lambda b, i: (b * nq + i, 7)),
            pl.BlockSpec((n_mem, 2 * width), lambda b, i: (b, 0)),
            pl.BlockSpec((1, dm), lambda b, i: (0, 0)),
            pl.BlockSpec((1, dm), lambda b, i: (0, 0)),
        ],
        out_specs=pl.BlockSpec((tq, width), lambda b, i: (b * nq + i, 0)),
        out_shape=jax.ShapeDtypeStruct((bsz * seq, width), BF16),
        compiler_params=_params([3 * tq * width * 2, n_mem * 2 * width * 2], temp_bytes=8 * tq * dm * 4, n_axes=2),
        name="memory_attention",
    )(proj, proj, kv, q_gain.reshape(1, dm).astype(F32), k_gain.reshape(1, dm).astype(F32))


def _merge_kernel(ya_ref, ys_ref, yc_ref, wa_ref, ws_ref, wc_ref, ga_ref, gs_ref, gc_ref, o_ref):
    def term(y_ref, w_ref, g_ref):
        return jax.nn.sigmoid(g_ref[...].astype(F32)) * jnp.dot(y_ref[...], w_ref[...], preferred_element_type=F32)

    o_ref[...] = (term(ya_ref, wa_ref, ga_ref) + term(ys_ref, ws_ref, gs_ref)
                  + term(yc_ref, wc_ref, gc_ref)).astype(o_ref.dtype)


def _merge(y_a, y_s, y_c, w_a, w_s, w_c, proj, g_col0, tm, tn):
    n, width = y_a.shape
    d = w_a.shape[1]
    tm, tn = _tile(n, tm), _tile(d, tn)
    y_spec = pl.BlockSpec((tm, width), lambda i, j: (i, 0))
    w_spec = pl.BlockSpec((width, tn), lambda i, j: (0, j))

    def g_spec(branch):
        return pl.BlockSpec((tm, tn), lambda i, j: (i, (g_col0 + branch * d) // tn + j))

    return pl.pallas_call(
        _merge_kernel,
        grid=(n // tm, d // tn),
        in_specs=[y_spec] * 3 + [w_spec] * 3 + [g_spec(0), g_spec(1), g_spec(2)],
        out_specs=pl.BlockSpec((tm, tn), lambda i, j: (i, j)),
        out_shape=jax.ShapeDtypeStruct((n, d), BF16),
        compiler_params=_params([3 * tm * width * 2, 3 * width * tn * 2, 4 * tm * tn * 2], temp_bytes=4 * tm * tn * 4, n_axes=2),
        name="branch_merge",
    )(y_a, y_s, y_c, w_a, w_s, w_c, proj, proj, proj)


def _outproj_kernel(m_ref, w_ref, x_ref, o_ref):
    o_ref[...] = x_ref[...] + jnp.dot(m_ref[...], w_ref[...], preferred_element_type=F32)


def _outproj(merged, w_out, x, tm, tn):
    n, d = merged.shape
    tm, tn = _tile(n, tm), _tile(d, tn)
    return pl.pallas_call(
        _outproj_kernel,
        grid=(n // tm, d // tn),
        in_specs=[
            pl.BlockSpec((tm, d), lambda i, j: (i, 0)),
            pl.BlockSpec((d, tn), lambda i, j: (0, j)),
            pl.BlockSpec((tm, tn), lambda i, j: (i, j)),
        ],
        out_specs=pl.BlockSpec((tm, tn), lambda i, j: (i, j)),
        out_shape=jax.ShapeDtypeStruct((n, d), F32),
        compiler_params=_params([tm * d * 2, d * tn * 2, 2 * tm * tn * 4], temp_bytes=tm * tn * 4, n_axes=2),
        name="out_projection",
    )(merged, w_out, x)


def kernel(x, mem, w_in, g_norm, g_mem, w_mem_kv, q_gain_a, k_gain_a, q_gain_c, k_gain_c, ssm_a_re, ssm_a_im, ssm_log_dt, ssm_b_re, ssm_b_im, ssm_c_re, ssm_c_im, ssm_d, w_glu, b_glu, w_br_a, w_br_s, w_br_c, w_out):
    bsz, seq, d_model = x.shape
    width = w_glu.shape[0]
    n_tok = bsz * seq
    n_mem = mem.shape[1]
    assert seq % MOBA_BLOCK == 0 and seq % SSM_CHUNK == 0 and width % V7X_LANES == 0
    assert w_in.shape == (d_model, 8 * width + 3 * d_model)

    x2 = x.reshape(n_tok, d_model)
    h = _rmsnorm(x2, g_norm, rows=256)
    proj, u = _matmul(h, w_in.astype(BF16), 1024, _tile(width, 1024), "in_projection",
                      lane_block_cols=(4 * width, width))

    m = _rmsnorm(mem.reshape(bsz * n_mem, d_model), g_mem, rows=256)
    kv = _matmul(m, w_mem_kv, 512, 512, "memory_kv_projection")

    y_a = _moba(proj, q_gain_a, k_gain_a, bsz, seq, width)

    y_g = _s5(u, ssm_a_re, ssm_a_im, ssm_log_dt, ssm_b_re, ssm_b_im, ssm_c_re, ssm_c_im, ssm_d, bsz)
    y_s = _glu(y_g, w_glu.astype(BF16), b_glu, proj, 5 * width, 512, 512)

    y_c = _memattn(proj, kv, q_gain_c, k_gain_c, bsz, seq, width, 512)

    merged = _merge(y_a, y_s, y_c, w_br_a.astype(BF16), w_br_s.astype(BF16), w_br_c.astype(BF16), proj,
                    8 * width, 1024, 512)
    out = _outproj(merged, w_out.astype(BF16), x2, 1024, 512)
    return out.reshape(bsz, seq, d_model)
```

```python
import functools

import jax
import jax.numpy as jnp
from jax import lax
from jax.experimental import pallas as pl
from jax.experimental.pallas import tpu as pltpu

F32 = jnp.float32
BF16 = jnp.bfloat16

V7X_LANES = 128
V7X_F32_SUBLANES = 8
V7X_BF16_SUBLANES = 16
V7X_VMEM_BYTES = 64 * 1024 * 1024
V7X_VMEM_RESERVE_BYTES = 6 * 1024 * 1024

ATTN_HEAD_DIM = 128
MOBA_BLOCK = 256
MOBA_TOP_K = 3
SSM_GROUP = 16
SSM_STATE = 64
MEM_HEADS = 4
EPS = 1e-6
NEG = -1e30
LOG2E = 1.4426950408889634

MOBA_KEY_GROUP = 2
MOBA_QUERY_GROUP = 4
MOBA_QUERY_STRIP = 256
SSM_CHUNK = 16
S5_OUT_COL_TILE = 512
GROUPS_PER_LANE_BLOCK = V7X_LANES // SSM_GROUP
STATE_LANES = GROUPS_PER_LANE_BLOCK * SSM_STATE


def _tile(n, pref):
    t = min(n, pref)
    while n % t:
        t -= V7X_LANES
    assert t > 0
    return t


def _params(block_bytes, scratch_bytes=0, temp_bytes=0, n_axes=1):
    need = 2 * sum(block_bytes) + scratch_bytes + temp_bytes
    limit = min(max(need, 16 * 1024 * 1024), V7X_VMEM_BYTES - V7X_VMEM_RESERVE_BYTES)
    return pltpu.CompilerParams(dimension_semantics=("arbitrary",) * n_axes, vmem_limit_bytes=int(limit))


def _rmsnorm_kernel(x_ref, g_ref, o_ref):
    x = x_ref[...].astype(F32)
    ms = jnp.mean(x * x, axis=-1, keepdims=True)
    o_ref[...] = (x * lax.rsqrt(ms + EPS) * g_ref[...]).astype(o_ref.dtype)


def _rmsnorm(x, gain, rows):
    n, d = x.shape
    tm = _tile(n, rows)
    return pl.pallas_call(
        _rmsnorm_kernel,
        grid=(n // tm,),
        in_specs=[pl.BlockSpec((tm, d), lambda i: (i, 0)), pl.BlockSpec((1, d), lambda i: (0, 0))],
        out_specs=pl.BlockSpec((tm, d), lambda i: (i, 0)),
        out_shape=jax.ShapeDtypeStruct((n, d), BF16),
        compiler_params=_params([tm * d * 4, tm * d * 2], temp_bytes=2 * tm * d * 4),
        name="rmsnorm",
    )(x, gain.reshape(1, d).astype(F32))


def _mm_kernel(a_ref, b_ref, o_ref):
    o_ref[...] = jnp.dot(a_ref[...], b_ref[...].astype(BF16), preferred_element_type=F32).astype(o_ref.dtype)


def _mm_lane_blocks_kernel(a_ref, b_ref, o_ref, blk_ref, *, j0, nj):
    res = jnp.dot(a_ref[...], b_ref[...], preferred_element_type=F32).astype(o_ref.dtype)
    o_ref[...] = res
    j = pl.program_id(1)

    @pl.when(jnp.logical_and(j >= j0, j < j0 + nj))
    def _():
        for k in range(blk_ref.shape[0]):
            blk_ref[k] = res[:, k * V7X_LANES:(k + 1) * V7X_LANES]


def _matmul(a, b, tm, tn, name, lane_block_cols=None):
    m, k = a.shape
    _, n = b.shape
    tm, tn = _tile(m, tm), _tile(n, tn)
    in_specs = [pl.BlockSpec((tm, k), lambda i, j: (i, 0)), pl.BlockSpec((k, tn), lambda i, j: (0, j))]
    out_spec = pl.BlockSpec((tm, tn), lambda i, j: (i, j))
    out_shape = jax.ShapeDtypeStruct((m, n), BF16)
    b_cast_bytes = k * tn * 2 if b.dtype != BF16 else 0
    params = _params([tm * k * 2, k * tn * b.dtype.itemsize, 2 * tm * tn * 2],
                     temp_bytes=tm * tn * (4 + 4 + 2) + b_cast_bytes, n_axes=2)
    if lane_block_cols is None:
        return pl.pallas_call(_mm_kernel, grid=(m // tm, n // tn), in_specs=in_specs, out_specs=out_spec,
                              out_shape=out_shape, compiler_params=params, name=name)(a, b)
    col0, ncols = lane_block_cols
    assert col0 % tn == 0 and ncols % tn == 0
    j0, nj, per = col0 // tn, ncols // tn, tn // V7X_LANES
    blk_spec = pl.BlockSpec((per, tm, V7X_LANES), lambda i, j: (jnp.clip(j - j0, 0, nj - 1), i, 0))
    return pl.pallas_call(
        functools.partial(_mm_lane_blocks_kernel, j0=j0, nj=nj),
        grid=(m // tm, n // tn), in_specs=in_specs, out_specs=[out_spec, blk_spec],
        out_shape=[out_shape, jax.ShapeDtypeStruct((ncols // V7X_LANES, m, V7X_LANES), BF16)],
        compiler_params=params, name=name)(a, b)


MOBA_ADD_ROW = ATTN_HEAD_DIM + V7X_BF16_SUBLANES
MOBA_PIECES = 3


def _split_bf16(x):
    hi = x.astype(BF16).astype(F32)
    mid = (x - hi).astype(BF16).astype(F32)
    return hi, mid, x - hi - mid


def _moba_kernel(q_ref, q_next_ref, k_ref, v_ref, z_ref, qg_ref, kg_ref, slope_ref, o_ref,
                 kn_ref, vt_ref, kmean_ref, rhs_ref, s_ref, acc_ref, *, n_blocks, kgroup, qgroup):
    it = pl.program_id(2)
    bs, hd, sub = MOBA_BLOCK, ATTN_HEAD_DIM, V7X_BF16_SUBLANES
    kt, qt = kgroup * bs, qgroup * bs
    n_kgroups = n_blocks // kgroup
    strip = min(qt, MOBA_QUERY_STRIP)
    slope2 = slope_ref[0][:, :1] * LOG2E

    def prepare_keys(g):
        key_off = lax.broadcasted_iota(jnp.int32, (bs, hd), 0).astype(F32)
        lane = lax.broadcasted_iota(jnp.int32, (bs, hd), 1)
        add_lane = lane - (MOBA_ADD_ROW - hd)
        in_add = jnp.logical_and(add_lane >= 0, add_lane < MOBA_PIECES * n_blocks)
        for bi in range(kgroup):
            c = g * kgroup + bi
            rows = pl.ds(pl.multiple_of(c * bs, bs), bs)
            kb = k_ref[rows, :].astype(F32)
            ms = jnp.mean(kb * kb, axis=-1, keepdims=True)
            kn = kb * lax.rsqrt(ms + EPS) * kg_ref[...]
            k_aug = jnp.where(lane < MOBA_PIECES, key_off,
                              jnp.where(jnp.logical_and(in_add, add_lane % n_blocks == c), 1.0, 0.0))
            kn_ref[g, bi * bs:(bi + 1) * bs, :] = jnp.concatenate([kn.astype(BF16), k_aug.astype(BF16)], axis=1)
            kmean_ref[pl.ds(c, 1), :] = jnp.mean(kn, axis=0, keepdims=True)
            vt_ref[g, :hd, bi * bs:(bi + 1) * bs] = v_ref[rows, :].astype(F32).T.astype(BF16)
        vt_ref[g, hd:, :] = jnp.where(lax.broadcasted_iota(jnp.int32, (sub, kt), 0) == 0, 1.0, 0.0).astype(BF16)

    def prepare_queries(q_block_ref, tile, rhs_slot):
        q = q_block_ref[...].astype(F32)
        ms = jnp.mean(q * q, axis=-1, keepdims=True)
        qn_t = (q * lax.rsqrt(ms + EPS) * qg_ref[...]).T

        gate = jnp.dot(kmean_ref[...], qn_t, preferred_element_type=F32, precision=lax.Precision.HIGHEST)
        blk = lax.broadcasted_iota(jnp.int32, gate.shape, 0)
        own = tile * qgroup + lax.broadcasted_iota(jnp.int32, gate.shape, 1) // bs
        blk_f = blk.astype(F32)
        past = blk < own
        g = jnp.where(past, gate, NEG)
        sel = jnp.zeros(gate.shape, jnp.bool_)
        for _ in range(min(MOBA_TOP_K, n_blocks)):
            top = jnp.max(g, axis=0, keepdims=True)
            first = jnp.min(jnp.where(g == top, blk_f, float(n_blocks)), axis=0, keepdims=True)
            pick = blk_f == first
            sel = jnp.logical_or(sel, pick)
            g = jnp.where(pick, -jnp.inf, g)
        sel = jnp.logical_and(sel, past)
        add = jnp.where(sel, slope2 * (bs * (blk - own)).astype(F32), jnp.where(blk == own, 0.0, NEG))

        row = lax.broadcasted_iota(jnp.int32, (sub, qt), 0)
        s_hi, s_mid, s_lo = _split_bf16(slope2)
        slope_rows = jnp.where(row == 0, s_hi, jnp.where(row == 1, s_mid, jnp.where(row == 2, s_lo, 0.0)))
        pad = jnp.zeros((2 * hd - MOBA_ADD_ROW - MOBA_PIECES * n_blocks, qt), F32)
        rhs_ref[rhs_slot] = jnp.concatenate([qn_t * (hd ** -0.5 * LOG2E), slope_rows, *_split_bf16(add), pad],
                                            axis=0).astype(BF16)

    own_groups = qgroup // kgroup

    @pl.when(it == 0)
    def _():
        kmean_ref[...] = jnp.zeros(kmean_ref.shape, F32)
        for t in range(own_groups):
            prepare_keys(t)
        prepare_queries(q_ref, 0, 0)

    rhs_now = it % 2

    def scores_to(slot, gi, causal_group=None):
        s = jnp.dot(kn_ref[gi], rhs_ref[rhs_now], preferred_element_type=F32)
        top = None
        for bi in range(kgroup):
            part = s[bi * bs:(bi + 1) * bs]
            if causal_group is not None:
                d0 = (causal_group * kgroup + bi) * bs
                tri = lax.broadcasted_iota(jnp.int32, (bs, bs), 0) <= lax.broadcasted_iota(jnp.int32, (bs, bs), 1)
                pieces = [part[:, :d0], jnp.where(tri, part[:, d0:d0 + bs], NEG), part[:, d0 + bs:]]
                part = jnp.concatenate([piece for piece in pieces if piece.shape[1]], axis=1)
            s_ref[slot, bi * bs:(bi + 1) * bs, :] = part
            top = part if top is None else jnp.maximum(top, part)
        return jnp.max(top, axis=0, keepdims=True)

    def accumulate(slot, gi, m, top, causal_group=None):
        m_new = jnp.maximum(m, top)
        alpha = jnp.exp2(m - m_new)
        for c0 in range(0, qt, strip):
            cols = slice(c0, c0 + strip)
            live = kgroup
            if causal_group is not None:
                live = min(max(c0 // bs - causal_group * kgroup + 1, 0), kgroup)
            if live == 0:
                continue
            p = jnp.exp2(s_ref[slot, :live * bs, cols] - m_new[:, cols]).astype(BF16)
            pv = jnp.dot(vt_ref[gi, :, :live * bs], p, preferred_element_type=F32)
            acc_ref[:, cols] = alpha[:, cols] * acc_ref[:, cols] + pv
        return m_new

    def pair(k, m):
        top0 = scores_to(0, 2 * k)
        top1 = scores_to(1, 2 * k + 1)
        return accumulate(1, 2 * k + 1, accumulate(0, 2 * k, m, top0), top1)

    acc_ref[...] = jnp.zeros(acc_ref.shape, F32)
    m = lax.fori_loop(0, it, pair, jnp.full((1, qt), 0.1 * NEG, F32))
    top0 = scores_to(0, 2 * it, causal_group=0)
    top1 = scores_to(1, 2 * it + 1, causal_group=1)
    accumulate(1, 2 * it + 1, accumulate(0, 2 * it, m, top0, causal_group=0), top1, causal_group=1)
    for t in range(own_groups):
        prepare_keys(jnp.minimum((it + 1) * own_groups + t, n_kgroups - 1))
    prepare_queries(q_next_ref, it + 1, 1 - rhs_now)
    acc = acc_ref[...]
    o = (acc[:hd] / acc[hd:hd + 1]).T
    o_ref[...] = (o * jax.nn.silu(z_ref[...].astype(F32))).astype(o_ref.dtype)


def _moba(proj, q_gain, k_gain, bsz, seq, width):
    n_heads = width // ATTN_HEAD_DIM
    n_blocks = seq // MOBA_BLOCK
    hd, bs, sub = ATTN_HEAD_DIM, MOBA_BLOCK, V7X_BF16_SUBLANES
    kgroup, qgroup = MOBA_KEY_GROUP, MOBA_QUERY_GROUP
    assert qgroup == 2 * kgroup and n_blocks % qgroup == 0
    assert MOBA_ADD_ROW + MOBA_PIECES * n_blocks <= 2 * hd
    n_kgroups, n_qtiles, kt, qt = n_blocks // kgroup, n_blocks // qgroup, kgroup * bs, qgroup * bs
    slopes = jnp.asarray([[[2.0 ** (-8.0 * (h + 1) / n_heads)] * V7X_LANES] for h in range(n_heads)], F32)
    scratch = [
        pltpu.VMEM((n_kgroups, kt, 2 * hd), BF16),
        pltpu.VMEM((n_kgroups, hd + sub, kt), BF16),
        pltpu.VMEM((n_blocks, hd), F32),
        pltpu.VMEM((2, 2 * hd, qt), BF16),
        pltpu.VMEM((2, kt, qt), F32),
        pltpu.VMEM((hd + sub, qt), F32),
    ]
    scratch_bytes = (n_kgroups * (kt * 2 * hd + (hd + sub) * kt) * 2 + n_blocks * hd * 4
                     + 2 * 2 * hd * qt * 2 + 2 * kt * qt * 4 + (hd + sub) * qt * 4)
    return pl.pallas_call(
        functools.partial(_moba_kernel, n_blocks=n_blocks, kgroup=kgroup, qgroup=qgroup),
        grid=(bsz, n_heads, n_qtiles),
        in_specs=[
            pl.BlockSpec((qt, hd), lambda b, h, i: (b * n_qtiles + i, h)),
            pl.BlockSpec((qt, hd), lambda b, h, i: (b * n_qtiles + jnp.minimum(i + 1, n_qtiles - 1), h)),
            pl.BlockSpec((seq, hd), lambda b, h, i: (b, n_heads + h)),
            pl.BlockSpec((seq, hd), lambda b, h, i: (b, 2 * n_heads + h)),
            pl.BlockSpec((qt, hd), lambda b, h, i: (b * n_qtiles + i, 3 * n_heads + h)),
            pl.BlockSpec((1, hd), lambda b, h, i: (0, 0)),
            pl.BlockSpec((1, hd), lambda b, h, i: (0, 0)),
            pl.BlockSpec((1, 1, V7X_LANES), lambda b, h, i: (h, 0, 0)),
        ],
        out_specs=pl.BlockSpec((qt, hd), lambda b, h, i: (b * n_qtiles + i, h)),
        out_shape=jax.ShapeDtypeStruct((bsz * seq, width), BF16),
        scratch_shapes=scratch,
        compiler_params=_params([2 * seq * hd * 2, 4 * qt * hd * 2], scratch_bytes=scratch_bytes,
                                temp_bytes=4 * kt * qt * 4, n_axes=3),
        name="moba_attention",
    )(proj, proj, proj, proj, proj, q_gain.reshape(1, hd).astype(F32), k_gain.reshape(1, hd).astype(F32), slopes)


def _zoh(ar, ai, log_dt):
    dt = jnp.exp(log_dt)
    mag = jnp.exp(dt * ar)
    abr = mag * jnp.cos(dt * ai)
    abi = mag * jnp.sin(dt * ai)
    den = ar * ar + ai * ai
    nr = abr - 1.0
    return abr, abi, (nr * ar + abi * ai) / den, (abi * ar - nr * ai) / den


def _s5_kernel(u_ref, ar_ref, ai_ref, ldt_ref, bbr_ref, bbi_ref, ccr_ref, cci_ref, d_ref,
               ard_ref, aid_ref, ldtd_ref, bdr_ref, bdi_ref, cdr_ref, cdi_ref, y_ref,
               p_ref, qt_ref, t_ref, dk_ref, stage_ref, uc_ref, xs_ref, *, n_batch, rows_per_batch, row_tile):
    lb, gn, chunk = V7X_LANES, STATE_LANES, SSM_CHUNK
    kdim = chunk * lb
    col_tile = min(kdim, S5_OUT_COL_TILE)
    scan_rows = V7X_F32_SUBLANES
    assert rows_per_batch % scan_rows == 0
    abr, abi, f_re, f_im = _zoh(ar_ref[0], ai_ref[0], ldt_ref[0])

    abr_d, abi_d, f_re_d, f_im_d = _zoh(ard_ref[0], aid_ref[0], ldtd_ref[0])
    bd_re = f_re_d * bdr_ref[0] - f_im_d * bdi_ref[0]
    bd_im = f_re_d * bdi_ref[0] + f_im_d * bdr_ref[0]
    cd = jnp.concatenate([cdr_ref[0], -cdi_ref[0]], axis=1)
    same_group = (lax.broadcasted_iota(jnp.int32, (lb, lb), 0) // SSM_GROUP
                  == lax.broadcasted_iota(jnp.int32, (lb, lb), 1) // SSM_GROUP)
    pr_d = jnp.ones(abr_d.shape, F32)
    pi_d = jnp.zeros(abr_d.shape, F32)
    for tau in range(chunk):
        lag = jnp.concatenate([bd_re * pr_d - bd_im * pi_d, bd_re * pi_d + bd_im * pr_d], axis=1)
        blocks = lax.dot_general(lag, cd, (((1,), (1,)), ((), ())), preferred_element_type=F32,
                                 precision=lax.Precision.HIGHEST)
        dk_ref[tau] = jnp.where(same_group, blocks, 0.0).astype(BF16)
        pr_d, pi_d = pr_d * abr_d - pi_d * abi_d, pr_d * abi_d + pi_d * abr_d

    row_g = lax.broadcasted_iota(jnp.int32, (lb, gn), 0) // SSM_GROUP
    col_g = lax.broadcasted_iota(jnp.int32, (lb, gn), 1) // SSM_STATE
    same = row_g == col_g
    bb_re, bb_im = bbr_ref[0], bbi_ref[0]
    bbar_re = jnp.where(same, f_re * bb_re - f_im * bb_im, 0.0)
    bbar_im = jnp.where(same, f_re * bb_im + f_im * bb_re, 0.0)
    cc_re = jnp.where(same, ccr_ref[0], 0.0)
    cc_im = jnp.where(same, cci_ref[0], 0.0)

    pr = jnp.ones((1, gn), F32)
    pi = jnp.zeros((1, gn), F32)
    for tau in range(chunk + 1):
        if tau < chunk:
            s = chunk - 1 - tau
            p_ref[s * lb:(s + 1) * lb, :] = jnp.concatenate(
                [bbar_re * pr - bbar_im * pi, bbar_re * pi + bbar_im * pr], axis=1).astype(BF16)
        if tau >= 1:
            t = tau - 1
            qt_ref[t * lb:(t + 1) * lb, :] = jnp.concatenate(
                [cc_re * pr - cc_im * pi, -(cc_re * pi + cc_im * pr)], axis=1).astype(BF16)
        if tau < chunk:
            pr, pi = pr * abr - pi * abi, pr * abi + pi * abr
    al_re, al_im = pr, pi

    zero = jnp.zeros((lb, lb), BF16)
    for s in range(chunk):
        for t in range(chunk):
            t_ref[s * lb:(s + 1) * lb, t * lb:(t + 1) * lb] = dk_ref[t - s] if t >= s else zero

    n_rows = n_batch * rows_per_batch
    steps_per_batch = rows_per_batch * chunk
    for b in range(n_batch):
        stage_ref[...] = u_ref[0, b * steps_per_batch:(b + 1) * steps_per_batch, :].astype(F32)
        for s in range(chunk):
            uc_ref[b * rows_per_batch:(b + 1) * rows_per_batch, s * lb:(s + 1) * lb] = (
                stage_ref[pl.ds(s, rows_per_batch, stride=chunk), :].astype(BF16))

    for r0 in range(0, n_rows, row_tile):
        xs_ref[r0:r0 + row_tile, :] = jnp.dot(uc_ref[r0:r0 + row_tile, :], p_ref[...], preferred_element_type=F32)

    def step(c, carry):
        out = []
        for b in range(n_batch):
            xr, xi = carry[b]
            rows = pl.ds(pl.multiple_of(b * rows_per_batch + c * scan_rows, scan_rows), scan_rows)
            inc = xs_ref[rows, :]
            starts_re, starts_im = [], []
            for k in range(scan_rows):
                starts_re.append(xr)
                starts_im.append(xi)
                xr, xi = (al_re * xr - al_im * xi + inc[k:k + 1, :gn], al_re * xi + al_im * xr + inc[k:k + 1, gn:])
            xs_ref[rows, :gn] = jnp.concatenate(starts_re, axis=0)
            xs_ref[rows, gn:] = jnp.concatenate(starts_im, axis=0)
            out.append((xr, xi))
        return tuple(out)

    x0 = jnp.zeros((1, gn), F32)
    lax.fori_loop(0, rows_per_batch // scan_rows, step, tuple((x0, x0) for _ in range(n_batch)))

    for b in range(n_batch):
        for r0 in range(0, rows_per_batch, row_tile):
            rows = slice(b * rows_per_batch + r0, b * rows_per_batch + r0 + row_tile)
            x_start = xs_ref[rows, :].astype(BF16)
            for c0 in range(0, kdim, col_tile):
                cols = slice(c0, c0 + col_tile)
                y = jnp.dot(uc_ref[rows, :c0 + col_tile], t_ref[:c0 + col_tile, cols], preferred_element_type=F32)
                y = y + lax.dot_general(x_start, qt_ref[cols, :], (((1,), (1,)), ((), ())),
                                        preferred_element_type=F32)
                y = jax.nn.gelu(y + d_ref[0][:, cols] * uc_ref[rows, cols].astype(F32))
                for s in range(c0 // lb, (c0 + col_tile) // lb):
                    stage_ref[pl.ds(r0 * chunk + s, row_tile, stride=chunk), :] = y[:, s * lb - c0:(s + 1) * lb - c0]
        y_ref[0, b * steps_per_batch:(b + 1) * steps_per_batch, :] = stage_ref[...].astype(y_ref.dtype)


def _s5(u_blocks, a_re, a_im, log_dt, b_re, b_im, c_re, c_im, d_skip, n_batch):
    nb, n_steps, lb = u_blocks.shape
    gpb, gn, chunk = GROUPS_PER_LANE_BLOCK, STATE_LANES, SSM_CHUNK
    n_rows, kdim = n_steps // chunk, chunk * lb
    rows_per_batch = n_rows // n_batch
    row_tile = _tile(rows_per_batch, 256)

    def lane_row(v):
        return v.astype(F32).reshape(nb, 1, gn)

    ldt = jnp.repeat(log_dt.astype(F32), SSM_STATE).reshape(nb, 1, gn)
    bt_re = jnp.tile(b_re.astype(F32).reshape(nb, gpb, SSM_STATE, SSM_GROUP).transpose(0, 3, 1, 2).reshape(nb, SSM_GROUP, gn), (1, gpb, 1))
    bt_im = jnp.tile(b_im.astype(F32).reshape(nb, gpb, SSM_STATE, SSM_GROUP).transpose(0, 3, 1, 2).reshape(nb, SSM_GROUP, gn), (1, gpb, 1))
    ct_re = jnp.tile(c_re.astype(F32).reshape(nb, lb, SSM_STATE), (1, 1, gpb))
    ct_im = jnp.tile(c_im.astype(F32).reshape(nb, lb, SSM_STATE), (1, 1, gpb))
    d_row = jnp.tile(d_skip.astype(F32).reshape(nb, 1, lb), (1, 1, chunk))

    def per_channel_rows(v):
        return jnp.repeat(v.astype(F32).reshape(nb, gpb, SSM_STATE), SSM_GROUP, axis=1)

    ldt_d = per_channel_rows(jnp.broadcast_to(log_dt[:, None], a_re.shape))
    bd_re = b_re.astype(F32).reshape(nb, gpb, SSM_STATE, SSM_GROUP).transpose(0, 1, 3, 2).reshape(nb, lb, SSM_STATE)
    bd_im = b_im.astype(F32).reshape(nb, gpb, SSM_STATE, SSM_GROUP).transpose(0, 1, 3, 2).reshape(nb, lb, SSM_STATE)
    cd_re = c_re.astype(F32).reshape(nb, lb, SSM_STATE)
    cd_im = c_im.astype(F32).reshape(nb, lb, SSM_STATE)
    dia = pl.BlockSpec((1, lb, SSM_STATE), lambda j: (j, 0, 0))

    vec = pl.BlockSpec((1, 1, gn), lambda j: (j, 0, 0))
    mat = pl.BlockSpec((1, lb, gn), lambda j: (j, 0, 0))
    blk = pl.BlockSpec((1, n_steps, lb), lambda j: (j, 0, 0))
    scratch = [
        pltpu.VMEM((kdim, 2 * gn), BF16),
        pltpu.VMEM((kdim, 2 * gn), BF16),
        pltpu.VMEM((kdim, kdim), BF16),
        pltpu.VMEM((chunk, lb, lb), BF16),
        pltpu.VMEM((n_steps // n_batch, lb), F32),
        pltpu.VMEM((n_rows, kdim), BF16),
        pltpu.VMEM((n_rows, 2 * gn), F32),
    ]
    scratch_bytes = (2 * kdim * 2 * gn * 2 + kdim * kdim * 2 + chunk * lb * lb * 2 + n_steps // n_batch * lb * 4
                     + n_rows * kdim * 2 + n_rows * 2 * gn * 4)
    return pl.pallas_call(
        functools.partial(_s5_kernel, n_batch=n_batch, rows_per_batch=rows_per_batch, row_tile=row_tile),
        grid=(nb,),
        in_specs=[blk, vec, vec, vec, mat, mat, mat, mat, pl.BlockSpec((1, 1, kdim), lambda j: (j, 0, 0))] + [dia] * 7,
        out_specs=blk,
        out_shape=jax.ShapeDtypeStruct((nb, n_steps, lb), BF16),
        scratch_shapes=scratch,
        compiler_params=_params([n_steps * lb * 2, n_steps * lb * 2, 4 * lb * gn * 4], scratch_bytes=scratch_bytes,
                                temp_bytes=4 * row_tile * kdim * 4),
        name="s5_scan",
    )(u_blocks, lane_row(a_re), lane_row(a_im), ldt, bt_re, bt_im, ct_re, ct_im, d_row,
      per_channel_rows(a_re), per_channel_rows(a_im), ldt_d, bd_re, bd_im, cd_re, cd_im)


def _glu_kernel(y_ref, w_ref, b_ref, z_ref, o_ref, *, col_tile):
    lb = y_ref.shape[2]
    y = jnp.concatenate([y_ref[k] for k in range(y_ref.shape[0])], axis=1)
    for c0 in range(0, o_ref.shape[1], col_tile):
        cols = slice(c0, c0 + col_tile)
        a = jnp.dot(y, w_ref[:, cols], preferred_element_type=F32) + b_ref[:, cols]
        z = z_ref[:, cols].astype(F32)
        ycol = jnp.concatenate([y_ref[k] for k in range(c0 // lb, (c0 + col_tile) // lb)], axis=1).astype(F32)
        o_ref[:, cols] = (ycol * z / ((1.0 + jnp.exp(-a)) * (1.0 + jnp.exp(-z)))).astype(o_ref.dtype)


def _glu(y_blocks, w_glu, b_glu, proj, z_col0, tm, col_tile):
    nb, n, lb = y_blocks.shape
    width = nb * lb
    tm, col_tile = _tile(n, tm), _tile(width, col_tile)
    assert z_col0 % width == 0
    return pl.pallas_call(
        functools.partial(_glu_kernel, col_tile=col_tile),
        grid=(n // tm,),
        in_specs=[
            pl.BlockSpec((nb, tm, lb), lambda i: (0, i, 0)),
            pl.BlockSpec((width, width), lambda i: (0, 0)),
            pl.BlockSpec((1, width), lambda i: (0, 0)),
            pl.BlockSpec((tm, width), lambda i: (i, z_col0 // width)),
        ],
        out_specs=pl.BlockSpec((tm, width), lambda i: (i, 0)),
        out_shape=jax.ShapeDtypeStruct((n, width), BF16),
        compiler_params=_params([tm * width * 2, width * width * 2, tm * width * 2, tm * width * 2],
                                temp_bytes=tm * width * 2 + 4 * tm * col_tile * 4),
        name="s5_glu",
    )(y_blocks, w_glu, b_glu.reshape(1, width).astype(F32), proj)


def _memattn_kernel(q_ref, z_ref, kv_ref, qg_ref, kg_ref, o_ref, *, width):
    dm = width // MEM_HEADS
    for hd in range(MEM_HEADS):
        cols = slice(hd * dm, (hd + 1) * dm)
        q = q_ref[:, cols].astype(F32)
        qn = q * lax.rsqrt(jnp.mean(q * q, axis=-1, keepdims=True) + EPS) * qg_ref[...]
        k = kv_ref[:, cols].astype(F32)
        kn = k * lax.rsqrt(jnp.mean(k * k, axis=-1, keepdims=True) + EPS) * kg_ref[...]
        v = kv_ref[:, width + hd * dm:width + (hd + 1) * dm]
        s = lax.dot_general(qn.astype(BF16), kn.astype(BF16), (((1,), (1,)), ((), ())),
                            preferred_element_type=F32) * (dm ** -0.5)
        p = jnp.exp(s - jnp.max(s, axis=-1, keepdims=True))
        l = jnp.sum(p, axis=-1, keepdims=True)
        o = jnp.dot(p.astype(BF16), v, preferred_element_type=F32) / l
        o_ref[:, cols] = (o * jax.nn.silu(z_ref[:, cols].astype(F32))).astype(o_ref.dtype)


def _memattn(proj, kv, q_gain, k_gain, bsz, seq, width, tq):
    dm = width // MEM_HEADS
    n_mem = kv.shape[0] // bsz
    tq = _tile(seq, tq)
    nq = seq // tq
    return pl.pallas_call(
        functools.partial(_memattn_kernel, width=width),
        grid=(bsz, nq),
        in_specs=[
            pl.BlockSpec((tq, width), lambda b, i: (b * nq + i, 6)),
            pl.BlockSpec((tq, width), lambda b, i: (b * nq + i, 7)),
            pl.BlockSpec((n_mem, 2 * width), lambda b, i: (b, 0)),
            pl.BlockSpec((1, dm), lambda b, i: (0, 0)),
            pl.BlockSpec((1, dm), lambda b, i: (0, 0)),
        ],
        out_specs=pl.BlockSpec((tq, width), lambda b, i: (b * nq + i, 0)),
        out_shape=jax.ShapeDtypeStruct((bsz * seq, width), BF16),
        compiler_params=_params([3 * tq * width * 2, n_mem * 2 * width * 2], temp_bytes=8 * tq * dm * 4, n_axes=2),
        name="memory_attention",
    )(proj, proj, kv, q_gain.reshape(1, dm).astype(F32), k_gain.reshape(1, dm).astype(F32))


def _merge_kernel(ya_ref, ys_ref, yc_ref, wa_ref, ws_ref, wc_ref, ga_ref, gs_ref, gc_ref, o_ref):
    def term(y_ref, w_ref, g_ref):
        return jax.nn.sigmoid(g_ref[...].astype(F32)) * jnp.dot(y_ref[...], w_ref[...], preferred_element_type=F32)

    o_ref[...] = (term(ya_ref, wa_ref, ga_ref) + term(ys_ref, ws_ref, gs_ref)
                  + term(yc_ref, wc_ref, gc_ref)).astype(o_ref.dtype)


def _merge(y_a, y_s, y_c, w_a, w_s, w_c, proj, g_col0, tm, tn):
    n, width = y_a.shape
    d = w_a.shape[1]
    tm, tn = _tile(n, tm), _tile(d, tn)
    y_spec = pl.BlockSpec((tm, width), lambda i, j: (i, 0))
    w_spec = pl.BlockSpec((width, tn), lambda i, j: (0, j))

    def g_spec(branch):
        return pl.BlockSpec((tm, tn), lambda i, j: (i, (g_col0 + branch * d) // tn + j))

    return pl.pallas_call(
        _merge_kernel,
        grid=(n // tm, d // tn),
        in_specs=[y_spec] * 3 + [w_spec] * 3 + [g_spec(0), g_spec(1), g_spec(2)],
        out_specs=pl.BlockSpec((tm, tn), lambda i, j: (i, j)),
        out_shape=jax.ShapeDtypeStruct((n, d), BF16),
        compiler_params=_params([3 * tm * width * 2, 3 * width * tn * 2, 4 * tm * tn * 2], temp_bytes=4 * tm * tn * 4, n_axes=2),
        name="branch_merge",
    )(y_a, y_s, y_c, w_a, w_s, w_c, proj, proj, proj)


def _outproj_kernel(m_ref, w_ref, x_ref, o_ref):
    o_ref[...] = x_ref[...] + jnp.dot(m_ref[...], w_ref[...], preferred_element_type=F32)


def _outproj(merged, w_out, x, tm, tn):
    n, d = merged.shape
    tm, tn = _tile(n, tm), _tile(d, tn)
    return pl.pallas_call(
        _outproj_kernel,
        grid=(n // tm, d // tn),
        in_specs=[
            pl.BlockSpec((tm, d), lambda i, j: (i, 0)),
            pl.BlockSpec((d, tn), lambda i, j: (0, j)),
            pl.BlockSpec((tm, tn), lambda i, j: (i, j)),
        ],
        out_specs=pl.BlockSpec((tm, tn), lambda i, j: (i, j)),
        out_shape=jax.ShapeDtypeStruct((n, d), F32),
        compiler_params=_params([tm * d * 2, d * tn * 2, 2 * tm * tn * 4], temp_bytes=tm * tn * 4, n_axes=2),
        name="out_projection",
    )(merged, w_out, x)


def kernel(x, mem, w_in, g_norm, g_mem, w_mem_kv, q_gain_a, k_gain_a, q_gain_c, k_gain_c, ssm_a_re, ssm_a_im, ssm_log_dt, ssm_b_re, ssm_b_im, ssm_c_re, ssm_c_im, ssm_d, w_glu, b_glu, w_br_a, w_br_s, w_br_c, w_out):
    bsz, seq, d_model = x.shape
    width = w_glu.shape[0]
    n_tok = bsz * seq
    n_mem = mem.shape[1]
    assert seq % MOBA_BLOCK == 0 and seq % SSM_CHUNK == 0 and width % V7X_LANES == 0
    assert w_in.shape == (d_model, 8 * width + 3 * d_model)

    x2 = x.reshape(n_tok, d_model)
    h = _rmsnorm(x2, g_norm, rows=512)
    proj, u = _matmul(h, w_in.astype(BF16), 1024, _tile(width, 1024), "in_projection",
                      lane_block_cols=(4 * width, width))

    m = _rmsnorm(mem.reshape(bsz * n_mem, d_model), g_mem, rows=256)
    kv = _matmul(m, w_mem_kv, 512, 512, "memory_kv_projection")

    y_a = _moba(proj, q_gain_a, k_gain_a, bsz, seq, width)

    y_g = _s5(u, ssm_a_re, ssm_a_im, ssm_log_dt, ssm_b_re, ssm_b_im, ssm_c_re, ssm_c_im, ssm_d, bsz)
    y_s = _glu(y_g, w_glu.astype(BF16), b_glu, proj, 5 * width, 512, 512)

    y_c = _memattn(proj, kv, q_gain_c, k_gain_c, bsz, seq, width, 512)

    merged = _merge(y_a, y_s, y_c, w_br_a.astype(BF16), w_br_s.astype(BF16), w_br_c.astype(BF16), proj,
                    8 * width, 1024, 512)
    out = _outproj(merged, w_out.astype(BF16), x2, 1024, 512)
    return out.reshape(bsz, seq, d_model)
```

```python
import functools

import jax
import jax.numpy as jnp
from jax import lax
from jax.experimental import pallas as pl
from jax.experimental.pallas import tpu as pltpu

F32 = jnp.float32
BF16 = jnp.bfloat16

V7X_LANES = 128
V7X_F32_SUBLANES = 8
V7X_BF16_SUBLANES = 16
V7X_VMEM_BYTES = 64 * 1024 * 1024
V7X_VMEM_RESERVE_BYTES = 6 * 1024 * 1024

ATTN_HEAD_DIM = 128
MOBA_BLOCK = 256
MOBA_TOP_K = 3
SSM_GROUP = 16
SSM_STATE = 64
MEM_HEADS = 4
EPS = 1e-6
NEG = -1e30
LOG2E = 1.4426950408889634

MOBA_KEY_GROUP = 2
MOBA_QUERY_GROUP = 4
MOBA_QUERY_STRIP = 256
SSM_CHUNK = 16
S5_OUT_COL_TILE = 512
GROUPS_PER_LANE_BLOCK = V7X_LANES // SSM_GROUP
STATE_LANES = GROUPS_PER_LANE_BLOCK * SSM_STATE


def _tile(n, pref):
    t = min(n, pref)
    while n % t:
        t -= V7X_LANES
    assert t > 0
    return t


def _params(block_bytes, scratch_bytes=0, temp_bytes=0, n_axes=1):
    need = 2 * sum(block_bytes) + scratch_bytes + temp_bytes
    limit = min(max(need, 16 * 1024 * 1024), V7X_VMEM_BYTES - V7X_VMEM_RESERVE_BYTES)
    return pltpu.CompilerParams(dimension_semantics=("arbitrary",) * n_axes, vmem_limit_bytes=int(limit))


def _rmsnorm_kernel(x_ref, g_ref, o_ref):
    x = x_ref[...].astype(F32)
    ms = jnp.mean(x * x, axis=-1, keepdims=True)
    o_ref[...] = (x * lax.rsqrt(ms + EPS) * g_ref[...]).astype(o_ref.dtype)


def _rmsnorm(x, gain, rows):
    n, d = x.shape
    tm = _tile(n, rows)
    return pl.pallas_call(
        _rmsnorm_kernel,
        grid=(n // tm,),
        in_specs=[pl.BlockSpec((tm, d), lambda i: (i, 0)), pl.BlockSpec((1, d), lambda i: (0, 0))],
        out_specs=pl.BlockSpec((tm, d), lambda i: (i, 0)),
        out_shape=jax.ShapeDtypeStruct((n, d), BF16),
        compiler_params=_params([tm * d * 4, tm * d * 2], temp_bytes=2 * tm * d * 4),
        name="rmsnorm",
    )(x, gain.reshape(1, d).astype(F32))


def _mm_kernel(a_ref, b_ref, o_ref):
    o_ref[...] = jnp.dot(a_ref[...], b_ref[...].astype(BF16), preferred_element_type=F32).astype(o_ref.dtype)


def _mm_slabs_kernel(a_ref, b_ref, o_ref, slab_ref, *, n_slab_steps):
    res = jnp.dot(a_ref[...], b_ref[...], preferred_element_type=F32).astype(o_ref.dtype)
    o_ref[...] = res

    @pl.when(pl.program_id(1) < n_slab_steps)
    def _():
        for k in range(slab_ref.shape[0]):
            slab_ref[k] = res[:, k * V7X_LANES:(k + 1) * V7X_LANES]


def _matmul(a, b, tm, tn, name, slab_cols=0):
    m, k = a.shape
    _, n = b.shape
    tm, tn = _tile(m, tm), _tile(n, tn)
    in_specs = [pl.BlockSpec((tm, k), lambda i, j: (i, 0)), pl.BlockSpec((k, tn), lambda i, j: (0, j))]
    b_cast_bytes = k * tn * 2 if b.dtype != BF16 else 0
    params = _params([tm * k * 2, k * tn * b.dtype.itemsize, 2 * tm * tn * 2],
                     temp_bytes=tm * tn * (4 + 4 + 2) + b_cast_bytes, n_axes=2)
    if not slab_cols:
        return pl.pallas_call(_mm_kernel, grid=(m // tm, n // tn), in_specs=in_specs,
                              out_specs=pl.BlockSpec((tm, tn), lambda i, j: (i, j)),
                              out_shape=jax.ShapeDtypeStruct((m, n), BF16), compiler_params=params, name=name)(a, b)
    assert slab_cols % tn == 0 and 0 < slab_cols < n
    nj, per = slab_cols // tn, tn // V7X_LANES
    out_spec = pl.BlockSpec((tm, tn), lambda i, j: (i, jnp.maximum(j - nj, 0)))
    slab_spec = pl.BlockSpec((per, tm, V7X_LANES), lambda i, j: (jnp.minimum(j, nj - 1), i, 0))
    return pl.pallas_call(
        functools.partial(_mm_slabs_kernel, n_slab_steps=nj),
        grid=(m // tm, n // tn), in_specs=in_specs, out_specs=[out_spec, slab_spec],
        out_shape=[jax.ShapeDtypeStruct((m, n - slab_cols), BF16),
                   jax.ShapeDtypeStruct((slab_cols // V7X_LANES, m, V7X_LANES), BF16)],
        compiler_params=params, name=name)(a, b)


MOBA_ADD_ROW = ATTN_HEAD_DIM + V7X_BF16_SUBLANES
MOBA_PIECES = 3


def _split_bf16(x):
    hi = x.astype(BF16).astype(F32)
    mid = (x - hi).astype(BF16).astype(F32)
    return hi, mid, x - hi - mid


def _moba_kernel(q_ref, q_next_ref, k_ref, v_ref, z_ref, qg_ref, kg_ref, slope_ref, o_ref,
                 kn_ref, vt_ref, kmean_ref, rhs_ref, s_ref, acc_ref, *, n_blocks, kgroup, qgroup):
    it = pl.program_id(2)
    bs, hd, sub = MOBA_BLOCK, ATTN_HEAD_DIM, V7X_BF16_SUBLANES
    kt, qt = kgroup * bs, qgroup * bs
    n_kgroups = n_blocks // kgroup
    strip = min(qt, MOBA_QUERY_STRIP)
    slope2 = slope_ref[0][:, :1] * LOG2E

    def prepare_keys(g):
        key_off = lax.broadcasted_iota(jnp.int32, (bs, hd), 0).astype(F32)
        lane = lax.broadcasted_iota(jnp.int32, (bs, hd), 1)
        add_lane = lane - (MOBA_ADD_ROW - hd)
        in_add = jnp.logical_and(add_lane >= 0, add_lane < MOBA_PIECES * n_blocks)
        for bi in range(kgroup):
            c = g * kgroup + bi
            rows = pl.ds(pl.multiple_of(c * bs, bs), bs)
            kb = k_ref[rows, :].astype(F32)
            ms = jnp.mean(kb * kb, axis=-1, keepdims=True)
            kn = kb * lax.rsqrt(ms + EPS) * kg_ref[...]
            k_aug = jnp.where(lane < MOBA_PIECES, key_off,
                              jnp.where(jnp.logical_and(in_add, add_lane % n_blocks == c), 1.0, 0.0))
            kn_ref[g, bi * bs:(bi + 1) * bs, :] = jnp.concatenate([kn.astype(BF16), k_aug.astype(BF16)], axis=1)
            kmean_ref[pl.ds(c, 1), :] = jnp.mean(kn, axis=0, keepdims=True)
            vt_ref[g, :hd, bi * bs:(bi + 1) * bs] = v_ref[rows, :].astype(F32).T.astype(BF16)
        vt_ref[g, hd:, :] = jnp.where(lax.broadcasted_iota(jnp.int32, (sub, kt), 0) == 0, 1.0, 0.0).astype(BF16)

    def prepare_queries(q_block_ref, tile, rhs_slot):
        q = q_block_ref[...].astype(F32)
        ms = jnp.mean(q * q, axis=-1, keepdims=True)
        qn_t = (q * lax.rsqrt(ms + EPS) * qg_ref[...]).T

        gate = jnp.dot(kmean_ref[...], qn_t, preferred_element_type=F32, precision=lax.Precision.HIGHEST)
        blk = lax.broadcasted_iota(jnp.int32, gate.shape, 0)
        own = tile * qgroup + lax.broadcasted_iota(jnp.int32, gate.shape, 1) // bs
        blk_f = blk.astype(F32)
        past = blk < own
        g = jnp.where(past, gate, NEG)
        sel = jnp.zeros(gate.shape, jnp.bool_)
        for _ in range(min(MOBA_TOP_K, n_blocks)):
            top = jnp.max(g, axis=0, keepdims=True)
            first = jnp.min(jnp.where(g == top, blk_f, float(n_blocks)), axis=0, keepdims=True)
            pick = blk_f == first
            sel = jnp.logical_or(sel, pick)
            g = jnp.where(pick, -jnp.inf, g)
        sel = jnp.logical_and(sel, past)
        add = jnp.where(sel, slope2 * (bs * (blk - own)).astype(F32), jnp.where(blk == own, 0.0, NEG))

        row = lax.broadcasted_iota(jnp.int32, (sub, qt), 0)
        s_hi, s_mid, s_lo = _split_bf16(slope2)
        slope_rows = jnp.where(row == 0, s_hi, jnp.where(row == 1, s_mid, jnp.where(row == 2, s_lo, 0.0)))
        pad = jnp.zeros((2 * hd - MOBA_ADD_ROW - MOBA_PIECES * n_blocks, qt), F32)
        rhs_ref[rhs_slot] = jnp.concatenate([qn_t * (hd ** -0.5 * LOG2E), slope_rows, *_split_bf16(add), pad],
                                            axis=0).astype(BF16)

    own_groups = qgroup // kgroup

    @pl.when(it == 0)
    def _():
        kmean_ref[...] = jnp.zeros(kmean_ref.shape, F32)
        for t in range(own_groups):
            prepare_keys(t)
        prepare_queries(q_ref, 0, 0)

    rhs_now = it % 2

    def scores_to(slot, gi, causal_group=None):
        s = jnp.dot(kn_ref[gi], rhs_ref[rhs_now], preferred_element_type=F32)
        top = None
        for bi in range(kgroup):
            part = s[bi * bs:(bi + 1) * bs]
            if causal_group is not None:
                d0 = (causal_group * kgroup + bi) * bs
                tri = lax.broadcasted_iota(jnp.int32, (bs, bs), 0) <= lax.broadcasted_iota(jnp.int32, (bs, bs), 1)
                pieces = [part[:, :d0], jnp.where(tri, part[:, d0:d0 + bs], NEG), part[:, d0 + bs:]]
                part = jnp.concatenate([piece for piece in pieces if piece.shape[1]], axis=1)
            s_ref[slot, bi * bs:(bi + 1) * bs, :] = part
            top = part if top is None else jnp.maximum(top, part)
        return jnp.max(top, axis=0, keepdims=True)

    def accumulate(slot, gi, m, top, causal_group=None):
        m_new = jnp.maximum(m, top)
        alpha = jnp.exp2(m - m_new)
        for c0 in range(0, qt, strip):
            cols = slice(c0, c0 + strip)
            live = kgroup
            if causal_group is not None:
                live = min(max(c0 // bs - causal_group * kgroup + 1, 0), kgroup)
            if live == 0:
                continue
            p = jnp.exp2(s_ref[slot, :live * bs, cols] - m_new[:, cols]).astype(BF16)
            pv = jnp.dot(vt_ref[gi, :, :live * bs], p, preferred_element_type=F32)
            acc_ref[:, cols] = alpha[:, cols] * acc_ref[:, cols] + pv
        return m_new

    def pair(k, m):
        top0 = scores_to(0, 2 * k)
        top1 = scores_to(1, 2 * k + 1)
        return accumulate(1, 2 * k + 1, accumulate(0, 2 * k, m, top0), top1)

    acc_ref[...] = jnp.zeros(acc_ref.shape, F32)
    m = lax.fori_loop(0, it, pair, jnp.full((1, qt), 0.1 * NEG, F32))
    top0 = scores_to(0, 2 * it, causal_group=0)
    top1 = scores_to(1, 2 * it + 1, causal_group=1)
    accumulate(1, 2 * it + 1, accumulate(0, 2 * it, m, top0, causal_group=0), top1, causal_group=1)
    for t in range(own_groups):
        prepare_keys(jnp.minimum((it + 1) * own_groups + t, n_kgroups - 1))
    prepare_queries(q_next_ref, it + 1, 1 - rhs_now)
    acc = acc_ref[...]
    o = (acc[:hd] / acc[hd:hd + 1]).T
    o_ref[...] = (o * jax.nn.silu(z_ref[...].astype(F32))).astype(o_ref.dtype)


def _moba(slabs, q_gain, k_gain, bsz, seq, width):
    n_heads = width // ATTN_HEAD_DIM
    n_blocks = seq // MOBA_BLOCK
    hd, bs, sub = ATTN_HEAD_DIM, MOBA_BLOCK, V7X_BF16_SUBLANES
    kgroup, qgroup = MOBA_KEY_GROUP, MOBA_QUERY_GROUP
    assert qgroup == 2 * kgroup and n_blocks % qgroup == 0
    assert MOBA_ADD_ROW + MOBA_PIECES * n_blocks <= 2 * hd
    n_kgroups, n_qtiles, kt, qt = n_blocks // kgroup, n_blocks // qgroup, kgroup * bs, qgroup * bs
    slopes = jnp.asarray([[[2.0 ** (-8.0 * (h + 1) / n_heads)] * V7X_LANES] for h in range(n_heads)], F32)
    scratch = [
        pltpu.VMEM((n_kgroups, kt, 2 * hd), BF16),
        pltpu.VMEM((n_kgroups, hd + sub, kt), BF16),
        pltpu.VMEM((n_blocks, hd), F32),
        pltpu.VMEM((2, 2 * hd, qt), BF16),
        pltpu.VMEM((2, kt, qt), F32),
        pltpu.VMEM((hd + sub, qt), F32),
    ]
    scratch_bytes = (n_kgroups * (kt * 2 * hd + (hd + sub) * kt) * 2 + n_blocks * hd * 4
                     + 2 * 2 * hd * qt * 2 + 2 * kt * qt * 4 + (hd + sub) * qt * 4)
    return pl.pallas_call(
        functools.partial(_moba_kernel, n_blocks=n_blocks, kgroup=kgroup, qgroup=qgroup),
        grid=(bsz, n_heads, n_qtiles),
        in_specs=[
            pl.BlockSpec((None, qt, hd), lambda b, h, i: (h, b * n_qtiles + i, 0)),
            pl.BlockSpec((None, qt, hd), lambda b, h, i: (h, b * n_qtiles + jnp.minimum(i + 1, n_qtiles - 1), 0)),
            pl.BlockSpec((None, seq, hd), lambda b, h, i: (n_heads + h, b, 0)),
            pl.BlockSpec((None, seq, hd), lambda b, h, i: (2 * n_heads + h, b, 0)),
            pl.BlockSpec((None, qt, hd), lambda b, h, i: (3 * n_heads + h, b * n_qtiles + i, 0)),
            pl.BlockSpec((1, hd), lambda b, h, i: (0, 0)),
            pl.BlockSpec((1, hd), lambda b, h, i: (0, 0)),
            pl.BlockSpec((1, 1, V7X_LANES), lambda b, h, i: (h, 0, 0)),
        ],
        out_specs=pl.BlockSpec((qt, hd), lambda b, h, i: (b * n_qtiles + i, h)),
        out_shape=jax.ShapeDtypeStruct((bsz * seq, width), BF16),
        scratch_shapes=scratch,
        compiler_params=_params([2 * seq * hd * 2, 4 * qt * hd * 2], scratch_bytes=scratch_bytes,
                                temp_bytes=4 * kt * qt * 4, n_axes=3),
        name="moba_attention",
    )(slabs, slabs, slabs, slabs, slabs, q_gain.reshape(1, hd).astype(F32), k_gain.reshape(1, hd).astype(F32), slopes)


def _zoh(ar, ai, log_dt):
    dt = jnp.exp(log_dt)
    mag = jnp.exp(dt * ar)
    abr = mag * jnp.cos(dt * ai)
    abi = mag * jnp.sin(dt * ai)
    den = ar * ar + ai * ai
    nr = abr - 1.0
    return abr, abi, (nr * ar + abi * ai) / den, (abi * ar - nr * ai) / den


def _s5_kernel(u_ref, ar_ref, ai_ref, ldt_ref, bbr_ref, bbi_ref, ccr_ref, cci_ref, d_ref,
               ard_ref, aid_ref, ldtd_ref, bdr_ref, bdi_ref, cdr_ref, cdi_ref, y_ref,
               p_ref, qt_ref, t_ref, dk_ref, stage_ref, uc_ref, xs_ref, *, n_batch, rows_per_batch, row_tile):
    lb, gn, chunk = V7X_LANES, STATE_LANES, SSM_CHUNK
    kdim = chunk * lb
    col_tile = min(kdim, S5_OUT_COL_TILE)
    scan_rows = V7X_F32_SUBLANES
    assert rows_per_batch % scan_rows == 0
    abr, abi, f_re, f_im = _zoh(ar_ref[0], ai_ref[0], ldt_ref[0])

    abr_d, abi_d, f_re_d, f_im_d = _zoh(ard_ref[0], aid_ref[0], ldtd_ref[0])
    bd_re = f_re_d * bdr_ref[0] - f_im_d * bdi_ref[0]
    bd_im = f_re_d * bdi_ref[0] + f_im_d * bdr_ref[0]
    cd = jnp.concatenate([cdr_ref[0], -cdi_ref[0]], axis=1)
    same_group = (lax.broadcasted_iota(jnp.int32, (lb, lb), 0) // SSM_GROUP
                  == lax.broadcasted_iota(jnp.int32, (lb, lb), 1) // SSM_GROUP)
    pr_d = jnp.ones(abr_d.shape, F32)
    pi_d = jnp.zeros(abr_d.shape, F32)
    for tau in range(chunk):
        lag = jnp.concatenate([bd_re * pr_d - bd_im * pi_d, bd_re * pi_d + bd_im * pr_d], axis=1)
        blocks = lax.dot_general(lag, cd, (((1,), (1,)), ((), ())), preferred_element_type=F32,
                                 precision=lax.Precision.HIGHEST)
        dk_ref[tau] = jnp.where(same_group, blocks, 0.0).astype(BF16)
        pr_d, pi_d = pr_d * abr_d - pi_d * abi_d, pr_d * abi_d + pi_d * abr_d

    row_g = lax.broadcasted_iota(jnp.int32, (lb, gn), 0) // SSM_GROUP
    col_g = lax.broadcasted_iota(jnp.int32, (lb, gn), 1) // SSM_STATE
    same = row_g == col_g
    bb_re, bb_im = bbr_ref[0], bbi_ref[0]
    bbar_re = jnp.where(same, f_re * bb_re - f_im * bb_im, 0.0)
    bbar_im = jnp.where(same, f_re * bb_im + f_im * bb_re, 0.0)
    cc_re = jnp.where(same, ccr_ref[0], 0.0)
    cc_im = jnp.where(same, cci_ref[0], 0.0)

    pr = jnp.ones((1, gn), F32)
    pi = jnp.zeros((1, gn), F32)
    for tau in range(chunk + 1):
        if tau < chunk:
            s = chunk - 1 - tau
            p_ref[s * lb:(s + 1) * lb, :] = jnp.concatenate(
                [bbar_re * pr - bbar_im * pi, bbar_re * pi + bbar_im * pr], axis=1).astype(BF16)
        if tau >= 1:
            t = tau - 1
            qt_ref[t * lb:(t + 1) * lb, :] = jnp.concatenate(
                [cc_re * pr - cc_im * pi, -(cc_re * pi + cc_im * pr)], axis=1).astype(BF16)
        if tau < chunk:
            pr, pi = pr * abr - pi * abi, pr * abi + pi * abr
    al_re, al_im = pr, pi

    zero = jnp.zeros((lb, lb), BF16)
    for s in range(chunk):
        for t in range(chunk):
            t_ref[s * lb:(s + 1) * lb, t * lb:(t + 1) * lb] = dk_ref[t - s] if t >= s else zero

    n_rows = n_batch * rows_per_batch
    steps_per_batch = rows_per_batch * chunk
    for b in range(n_batch):
        stage_ref[...] = u_ref[0, b * steps_per_batch:(b + 1) * steps_per_batch, :].astype(F32)
        for s in range(chunk):
            uc_ref[b * rows_per_batch:(b + 1) * rows_per_batch, s * lb:(s + 1) * lb] = (
                stage_ref[pl.ds(s, rows_per_batch, stride=chunk), :].astype(BF16))

    for r0 in range(0, n_rows, row_tile):
        xs_ref[r0:r0 + row_tile, :] = jnp.dot(uc_ref[r0:r0 + row_tile, :], p_ref[...], preferred_element_type=F32)

    def step(c, carry):
        out = []
        for b in range(n_batch):
            xr, xi = carry[b]
            rows = pl.ds(pl.multiple_of(b * rows_per_batch + c * scan_rows, scan_rows), scan_rows)
            inc = xs_ref[rows, :]
            starts_re, starts_im = [], []
            for k in range(scan_rows):
                starts_re.append(xr)
                starts_im.append(xi)
                xr, xi = (al_re * xr - al_im * xi + inc[k:k + 1, :gn], al_re * xi + al_im * xr + inc[k:k + 1, gn:])
            xs_ref[rows, :gn] = jnp.concatenate(starts_re, axis=0)
            xs_ref[rows, gn:] = jnp.concatenate(starts_im, axis=0)
            out.append((xr, xi))
        return tuple(out)

    x0 = jnp.zeros((1, gn), F32)
    lax.fori_loop(0, rows_per_batch // scan_rows, step, tuple((x0, x0) for _ in range(n_batch)))

    for b in range(n_batch):
        for r0 in range(0, rows_per_batch, row_tile):
            rows = slice(b * rows_per_batch + r0, b * rows_per_batch + r0 + row_tile)
            x_start = xs_ref[rows, :].astype(BF16)
            for c0 in range(0, kdim, col_tile):
                cols = slice(c0, c0 + col_tile)
                y = jnp.dot(uc_ref[rows, :c0 + col_tile], t_ref[:c0 + col_tile, cols], preferred_element_type=F32)
                y = y + lax.dot_general(x_start, qt_ref[cols, :], (((1,), (1,)), ((), ())),
                                        preferred_element_type=F32)
                y = jax.nn.gelu(y + d_ref[0][:, cols] * uc_ref[rows, cols].astype(F32))
                for s in range(c0 // lb, (c0 + col_tile) // lb):
                    stage_ref[pl.ds(r0 * chunk + s, row_tile, stride=chunk), :] = y[:, s * lb - c0:(s + 1) * lb - c0]
        y_ref[0, b * steps_per_batch:(b + 1) * steps_per_batch, :] = stage_ref[...].astype(y_ref.dtype)


def _s5(slabs, slab0, nb, a_re, a_im, log_dt, b_re, b_im, c_re, c_im, d_skip, n_batch):
    _, n_steps, lb = slabs.shape
    gpb, gn, chunk = GROUPS_PER_LANE_BLOCK, STATE_LANES, SSM_CHUNK
    n_rows, kdim = n_steps // chunk, chunk * lb
    rows_per_batch = n_rows // n_batch
    row_tile = _tile(rows_per_batch, 256)

    def lane_row(v):
        return v.astype(F32).reshape(nb, 1, gn)

    ldt = jnp.repeat(log_dt.astype(F32), SSM_STATE).reshape(nb, 1, gn)
    bt_re = jnp.tile(b_re.astype(F32).reshape(nb, gpb, SSM_STATE, SSM_GROUP).transpose(0, 3, 1, 2).reshape(nb, SSM_GROUP, gn), (1, gpb, 1))
    bt_im = jnp.tile(b_im.astype(F32).reshape(nb, gpb, SSM_STATE, SSM_GROUP).transpose(0, 3, 1, 2).reshape(nb, SSM_GROUP, gn), (1, gpb, 1))
    ct_re = jnp.tile(c_re.astype(F32).reshape(nb, lb, SSM_STATE), (1, 1, gpb))
    ct_im = jnp.tile(c_im.astype(F32).reshape(nb, lb, SSM_STATE), (1, 1, gpb))
    d_row = jnp.tile(d_skip.astype(F32).reshape(nb, 1, lb), (1, 1, chunk))

    def per_channel_rows(v):
        return jnp.repeat(v.astype(F32).reshape(nb, gpb, SSM_STATE), SSM_GROUP, axis=1)

    ldt_d = per_channel_rows(jnp.broadcast_to(log_dt[:, None], a_re.shape))
    bd_re = b_re.astype(F32).reshape(nb, gpb, SSM_STATE, SSM_GROUP).transpose(0, 1, 3, 2).reshape(nb, lb, SSM_STATE)
    bd_im = b_im.astype(F32).reshape(nb, gpb, SSM_STATE, SSM_GROUP).transpose(0, 1, 3, 2).reshape(nb, lb, SSM_STATE)
    cd_re = c_re.astype(F32).reshape(nb, lb, SSM_STATE)
    cd_im = c_im.astype(F32).reshape(nb, lb, SSM_STATE)
    dia = pl.BlockSpec((1, lb, SSM_STATE), lambda j: (j, 0, 0))

    vec = pl.BlockSpec((1, 1, gn), lambda j: (j, 0, 0))
    mat = pl.BlockSpec((1, lb, gn), lambda j: (j, 0, 0))
    blk = pl.BlockSpec((1, n_steps, lb), lambda j: (j, 0, 0))
    blk_in = pl.BlockSpec((1, n_steps, lb), lambda j: (slab0 + j, 0, 0))
    scratch = [
        pltpu.VMEM((kdim, 2 * gn), BF16),
        pltpu.VMEM((kdim, 2 * gn), BF16),
        pltpu.VMEM((kdim, kdim), BF16),
        pltpu.VMEM((chunk, lb, lb), BF16),
        pltpu.VMEM((n_steps // n_batch, lb), F32),
        pltpu.VMEM((n_rows, kdim), BF16),
        pltpu.VMEM((n_rows, 2 * gn), F32),
    ]
    scratch_bytes = (2 * kdim * 2 * gn * 2 + kdim * kdim * 2 + chunk * lb * lb * 2 + n_steps // n_batch * lb * 4
                     + n_rows * kdim * 2 + n_rows * 2 * gn * 4)
    return pl.pallas_call(
        functools.partial(_s5_kernel, n_batch=n_batch, rows_per_batch=rows_per_batch, row_tile=row_tile),
        grid=(nb,),
        in_specs=[blk_in, vec, vec, vec, mat, mat, mat, mat, pl.BlockSpec((1, 1, kdim), lambda j: (j, 0, 0))] + [dia] * 7,
        out_specs=blk,
        out_shape=jax.ShapeDtypeStruct((nb, n_steps, lb), BF16),
        scratch_shapes=scratch,
        compiler_params=_params([n_steps * lb * 2, n_steps * lb * 2, 4 * lb * gn * 4], scratch_bytes=scratch_bytes,
                                temp_bytes=4 * row_tile * kdim * 4),
        name="s5_scan",
    )(slabs, lane_row(a_re), lane_row(a_im), ldt, bt_re, bt_im, ct_re, ct_im, d_row,
      per_channel_rows(a_re), per_channel_rows(a_im), ldt_d, bd_re, bd_im, cd_re, cd_im)


def _glu_kernel(y_ref, w_ref, b_ref, z_ref, o_ref, *, col_tile):
    lb = y_ref.shape[2]
    y = jnp.concatenate([y_ref[k] for k in range(y_ref.shape[0])], axis=1)
    for c0 in range(0, o_ref.shape[1], col_tile):
        cols = slice(c0, c0 + col_tile)
        a = jnp.dot(y, w_ref[:, cols], preferred_element_type=F32) + b_ref[:, cols]
        z = z_ref[:, cols].astype(F32)
        ycol = jnp.concatenate([y_ref[k] for k in range(c0 // lb, (c0 + col_tile) // lb)], axis=1).astype(F32)
        o_ref[:, cols] = (ycol * z / ((1.0 + jnp.exp(-a)) * (1.0 + jnp.exp(-z)))).astype(o_ref.dtype)


def _glu(y_blocks, w_glu, b_glu, proj, z_col0, tm, col_tile):
    nb, n, lb = y_blocks.shape
    width = nb * lb
    tm, col_tile = _tile(n, tm), _tile(width, col_tile)
    assert z_col0 % width == 0
    return pl.pallas_call(
        functools.partial(_glu_kernel, col_tile=col_tile),
        grid=(n // tm,),
        in_specs=[
            pl.BlockSpec((nb, tm, lb), lambda i: (0, i, 0)),
            pl.BlockSpec((width, width), lambda i: (0, 0)),
            pl.BlockSpec((1, width), lambda i: (0, 0)),
            pl.BlockSpec((tm, width), lambda i: (i, z_col0 // width)),
        ],
        out_specs=pl.BlockSpec((tm, width), lambda i: (i, 0)),
        out_shape=jax.ShapeDtypeStruct((n, width), BF16),
        compiler_params=_params([tm * width * 2, width * width * 2, tm * width * 2, tm * width * 2],
                                temp_bytes=tm * width * 2 + 4 * tm * col_tile * 4),
        name="s5_glu",
    )(y_blocks, w_glu, b_glu.reshape(1, width).astype(F32), proj)


def _memattn_kernel(q_ref, z_ref, kv_ref, qg_ref, kg_ref, o_ref, *, width):
    dm = width // MEM_HEADS
    for hd in range(MEM_HEADS):
        cols = slice(hd * dm, (hd + 1) * dm)
        q = q_ref[:, cols].astype(F32)
        qn = q * lax.rsqrt(jnp.mean(q * q, axis=-1, keepdims=True) + EPS) * qg_ref[...]
        k = kv_ref[:, cols].astype(F32)
        kn = k * lax.rsqrt(jnp.mean(k * k, axis=-1, keepdims=True) + EPS) * kg_ref[...]
        v = kv_ref[:, width + hd * dm:width + (hd + 1) * dm]
        s = lax.dot_general(qn.astype(BF16), kn.astype(BF16), (((1,), (1,)), ((), ())),
                            preferred_element_type=F32) * (dm ** -0.5)
        p = jnp.exp(s - jnp.max(s, axis=-1, keepdims=True))
        l = jnp.sum(p, axis=-1, keepdims=True)
        o = jnp.dot(p.astype(BF16), v, preferred_element_type=F32) / l
        o_ref[:, cols] = (o * jax.nn.silu(z_ref[:, cols].astype(F32))).astype(o_ref.dtype)


def _memattn(proj, q_col0, z_col0, kv, q_gain, k_gain, bsz, seq, width, tq):
    assert q_col0 % width == 0 and z_col0 % width == 0
    dm = width // MEM_HEADS
    n_mem = kv.shape[0] // bsz
    tq = _tile(seq, tq)
    nq = seq // tq
    return pl.pallas_call(
        functools.partial(_memattn_kernel, width=width),
        grid=(bsz, nq),
        in_specs=[
            pl.BlockSpec((tq, width), lambda b, i: (b * nq + i, q_col0 // width)),
            pl.BlockSpec((tq, width), lambda b, i: (b * nq + i, z_col0 // width)),
            pl.BlockSpec((n_mem, 2 * width), lambda b, i: (b, 0)),
            pl.BlockSpec((1, dm), lambda b, i: (0, 0)),
            pl.BlockSpec((1, dm), lambda b, i: (0, 0)),
        ],
        out_specs=pl.BlockSpec((tq, width), lambda b, i: (b * nq + i, 0)),
        out_shape=jax.ShapeDtypeStruct((bsz * seq, width), BF16),
        compiler_params=_params([3 * tq * width * 2, n_mem * 2 * width * 2], temp_bytes=8 * tq * dm * 4, n_axes=2),
        name="memory_attention",
    )(proj, proj, kv, q_gain.reshape(1, dm).astype(F32), k_gain.reshape(1, dm).astype(F32))


def _merge_kernel(ya_ref, ys_ref, yc_ref, wa_ref, ws_ref, wc_ref, ga_ref, gs_ref, gc_ref, o_ref):
    def term(y_ref, w_ref, g_ref):
        return jax.nn.sigmoid(g_ref[...].astype(F32)) * jnp.dot(y_ref[...], w_ref[...], preferred_element_type=F32)

    o_ref[...] = (term(ya_ref, wa_ref, ga_ref) + term(ys_ref, ws_ref, gs_ref)
                  + term(yc_ref, wc_ref, gc_ref)).astype(o_ref.dtype)


def _merge(y_a, y_s, y_c, w_a, w_s, w_c, proj, g_col0, tm, tn):
    n, width = y_a.shape
    d = w_a.shape[1]
    tm, tn = _tile(n, tm), _tile(d, tn)
    y_spec = pl.BlockSpec((tm, width), lambda i, j: (i, 0))
    w_spec = pl.BlockSpec((width, tn), lambda i, j: (0, j))

    def g_spec(branch):
        assert (g_col0 + branch * d) % tn == 0
        return pl.BlockSpec((tm, tn), lambda i, j: (i, (g_col0 + branch * d) // tn + j))

    return pl.pallas_call(
        _merge_kernel,
        grid=(n // tm, d // tn),
        in_specs=[y_spec] * 3 + [w_spec] * 3 + [g_spec(0), g_spec(1), g_spec(2)],
        out_specs=pl.BlockSpec((tm, tn), lambda i, j: (i, j)),
        out_shape=jax.ShapeDtypeStruct((n, d), BF16),
        compiler_params=_params([3 * tm * width * 2, 3 * width * tn * 2, 4 * tm * tn * 2], temp_bytes=4 * tm * tn * 4, n_axes=2),
        name="branch_merge",
    )(y_a, y_s, y_c, w_a, w_s, w_c, proj, proj, proj)


def _outproj_kernel(m_ref, w_ref, x_ref, o_ref):
    o_ref[...] = x_ref[...] + jnp.dot(m_ref[...], w_ref[...], preferred_element_type=F32)


def _outproj(merged, w_out, x, tm, tn):
    n, d = merged.shape
    tm, tn = _tile(n, tm), _tile(d, tn)
    return pl.pallas_call(
        _outproj_kernel,
        grid=(n // tm, d // tn),
        in_specs=[
            pl.BlockSpec((tm, d), lambda i, j: (i, 0)),
            pl.BlockSpec((d, tn), lambda i, j: (0, j)),
            pl.BlockSpec((tm, tn), lambda i, j: (i, j)),
        ],
        out_specs=pl.BlockSpec((tm, tn), lambda i, j: (i, j)),
        out_shape=jax.ShapeDtypeStruct((n, d), F32),
        compiler_params=_params([tm * d * 2, d * tn * 2, 2 * tm * tn * 4], temp_bytes=tm * tn * 4, n_axes=2),
        name="out_projection",
    )(merged, w_out, x)


def kernel(x, mem, w_in, g_norm, g_mem, w_mem_kv, q_gain_a, k_gain_a, q_gain_c, k_gain_c, ssm_a_re, ssm_a_im, ssm_log_dt, ssm_b_re, ssm_b_im, ssm_c_re, ssm_c_im, ssm_d, w_glu, b_glu, w_br_a, w_br_s, w_br_c, w_out):
    bsz, seq, d_model = x.shape
    width = w_glu.shape[0]
    n_tok = bsz * seq
    n_mem = mem.shape[1]
    assert seq % MOBA_BLOCK == 0 and seq % SSM_CHUNK == 0 and width % V7X_LANES == 0
    assert w_in.shape == (d_model, 8 * width + 3 * d_model)

    x2 = x.reshape(n_tok, d_model)
    h = _rmsnorm(x2, g_norm, rows=512)
    proj, slabs = _matmul(h, w_in.astype(BF16), 1024, _tile(width, 1024), "in_projection", slab_cols=5 * width)
    nb = width // V7X_LANES

    m = _rmsnorm(mem.reshape(bsz * n_mem, d_model), g_mem, rows=256)
    kv = _matmul(m, w_mem_kv, 512, 512, "memory_kv_projection")

    y_a = _moba(slabs, q_gain_a, k_gain_a, bsz, seq, width)

    y_g = _s5(slabs, 4 * nb, nb, ssm_a_re, ssm_a_im, ssm_log_dt, ssm_b_re, ssm_b_im, ssm_c_re, ssm_c_im, ssm_d, bsz)
    y_s = _glu(y_g, w_glu.astype(BF16), b_glu, proj, 0, 512, 512)

    y_c = _memattn(proj, width, 2 * width, kv, q_gain_c, k_gain_c, bsz, seq, width, 512)

    merged = _merge(y_a, y_s, y_c, w_br_a.astype(BF16), w_br_s.astype(BF16), w_br_c.astype(BF16), proj,
                    3 * width, 1024, 512)
    out = _outproj(merged, w_out.astype(BF16), x2, 1024, 512)
    return out.reshape(bsz, seq, d_model)
```

```python
import functools

import jax
import jax.numpy as jnp
from jax import lax
from jax.experimental import pallas as pl
from jax.experimental.pallas import tpu as pltpu

F32 = jnp.float32
BF16 = jnp.bfloat16

V7X_LANES = 128
V7X_F32_SUBLANES = 8
V7X_BF16_SUBLANES = 16
V7X_VMEM_BYTES = 64 * 1024 * 1024
V7X_VMEM_RESERVE_BYTES = 6 * 1024 * 1024

ATTN_HEAD_DIM = 128
MOBA_BLOCK = 256
MOBA_TOP_K = 3
SSM_GROUP = 16
SSM_STATE = 64
MEM_HEADS = 4
EPS = 1e-6
NEG = -1e30
LOG2E = 1.4426950408889634

MOBA_KEY_GROUP = 2
MOBA_QUERY_GROUP = 4
MOBA_QUERY_STRIP = 256
SSM_CHUNK = 16
S5_OUT_COL_TILE = 512
GROUPS_PER_LANE_BLOCK = V7X_LANES // SSM_GROUP
STATE_LANES = GROUPS_PER_LANE_BLOCK * SSM_STATE


def _tile(n, pref):
    t = min(n, pref)
    while n % t:
        t -= V7X_LANES
    assert t > 0
    return t


def _params(block_bytes, scratch_bytes=0, temp_bytes=0, n_axes=1):
    need = 2 * sum(block_bytes) + scratch_bytes + temp_bytes
    limit = min(max(need, 16 * 1024 * 1024), V7X_VMEM_BYTES - V7X_VMEM_RESERVE_BYTES)
    return pltpu.CompilerParams(dimension_semantics=("arbitrary",) * n_axes, vmem_limit_bytes=int(limit))


def _rmsnorm_kernel(x_ref, g_ref, o_ref):
    x = x_ref[...].astype(F32)
    ms = jnp.mean(x * x, axis=-1, keepdims=True)
    o_ref[...] = (x * lax.rsqrt(ms + EPS) * g_ref[...]).astype(o_ref.dtype)


def _rmsnorm(x, gain, rows):
    n, d = x.shape
    tm = _tile(n, rows)
    return pl.pallas_call(
        _rmsnorm_kernel,
        grid=(n // tm,),
        in_specs=[pl.BlockSpec((tm, d), lambda i: (i, 0)), pl.BlockSpec((1, d), lambda i: (0, 0))],
        out_specs=pl.BlockSpec((tm, d), lambda i: (i, 0)),
        out_shape=jax.ShapeDtypeStruct((n, d), BF16),
        compiler_params=_params([tm * d * 4, tm * d * 2], temp_bytes=2 * tm * d * 4),
        name="rmsnorm",
    )(x, gain.reshape(1, d).astype(F32))


def _mm_kernel(a_ref, b_ref, o_ref):
    o_ref[...] = jnp.dot(a_ref[...], b_ref[...].astype(BF16), preferred_element_type=F32).astype(o_ref.dtype)


def _mm_slabs_kernel(a_ref, b_ref, o_ref, slab_ref, *, n_slab_steps):
    res = jnp.dot(a_ref[...], b_ref[...], preferred_element_type=F32).astype(o_ref.dtype)
    o_ref[...] = res

    @pl.when(pl.program_id(1) < n_slab_steps)
    def _():
        for k in range(slab_ref.shape[0]):
            slab_ref[k] = res[:, k * V7X_LANES:(k + 1) * V7X_LANES]


def _matmul(a, b, tm, tn, name, slab_cols=0):
    m, k = a.shape
    _, n = b.shape
    tm, tn = _tile(m, tm), _tile(n, tn)
    in_specs = [pl.BlockSpec((tm, k), lambda i, j: (i, 0)), pl.BlockSpec((k, tn), lambda i, j: (0, j))]
    b_cast_bytes = k * tn * 2 if b.dtype != BF16 else 0
    params = _params([tm * k * 2, k * tn * b.dtype.itemsize, 2 * tm * tn * 2],
                     temp_bytes=tm * tn * (4 + 4 + 2) + b_cast_bytes, n_axes=2)
    if not slab_cols:
        return pl.pallas_call(_mm_kernel, grid=(m // tm, n // tn), in_specs=in_specs,
                              out_specs=pl.BlockSpec((tm, tn), lambda i, j: (i, j)),
                              out_shape=jax.ShapeDtypeStruct((m, n), BF16), compiler_params=params, name=name)(a, b)
    assert slab_cols % tn == 0 and 0 < slab_cols < n
    nj, per = slab_cols // tn, tn // V7X_LANES
    out_spec = pl.BlockSpec((tm, tn), lambda i, j: (i, jnp.maximum(j - nj, 0)))
    slab_spec = pl.BlockSpec((per, tm, V7X_LANES), lambda i, j: (jnp.minimum(j, nj - 1), i, 0))
    return pl.pallas_call(
        functools.partial(_mm_slabs_kernel, n_slab_steps=nj),
        grid=(m // tm, n // tn), in_specs=in_specs, out_specs=[out_spec, slab_spec],
        out_shape=[jax.ShapeDtypeStruct((m, n - slab_cols), BF16),
                   jax.ShapeDtypeStruct((slab_cols // V7X_LANES, m, V7X_LANES), BF16)],
        compiler_params=params, name=name)(a, b)


MOBA_ADD_ROW = ATTN_HEAD_DIM + V7X_BF16_SUBLANES
MOBA_PIECES = 3


def _split_bf16(x):
    hi = x.astype(BF16).astype(F32)
    mid = (x - hi).astype(BF16).astype(F32)
    return hi, mid, x - hi - mid


def _moba_kernel(q_ref, q_next_ref, k_ref, v_ref, z_ref, qg_ref, kg_ref, slope_ref, o_ref,
                 kn_ref, vt_ref, kmean_ref, rhs_ref, s_ref, acc_ref, *, n_blocks, kgroup, qgroup):
    it = pl.program_id(2)
    bs, hd, sub = MOBA_BLOCK, ATTN_HEAD_DIM, V7X_BF16_SUBLANES
    kt, qt = kgroup * bs, qgroup * bs
    n_kgroups = n_blocks // kgroup
    strip = min(qt, MOBA_QUERY_STRIP)
    slope2 = slope_ref[0][:, :1] * LOG2E

    def prepare_keys(g):
        key_off = lax.broadcasted_iota(jnp.int32, (bs, hd), 0).astype(F32)
        lane = lax.broadcasted_iota(jnp.int32, (bs, hd), 1)
        add_lane = lane - (MOBA_ADD_ROW - hd)
        in_add = jnp.logical_and(add_lane >= 0, add_lane < MOBA_PIECES * n_blocks)
        for bi in range(kgroup):
            c = g * kgroup + bi
            rows = pl.ds(pl.multiple_of(c * bs, bs), bs)
            kb = k_ref[rows, :].astype(F32)
            ms = jnp.mean(kb * kb, axis=-1, keepdims=True)
            kn = kb * lax.rsqrt(ms + EPS) * kg_ref[...]
            k_aug = jnp.where(lane < MOBA_PIECES, key_off,
                              jnp.where(jnp.logical_and(in_add, add_lane % n_blocks == c), 1.0, 0.0))
            kn_ref[g, bi * bs:(bi + 1) * bs, :] = jnp.concatenate([kn.astype(BF16), k_aug.astype(BF16)], axis=1)
            kmean_ref[pl.ds(c, 1), :] = jnp.mean(kn, axis=0, keepdims=True)
            vt_ref[g, :hd, bi * bs:(bi + 1) * bs] = v_ref[rows, :].astype(F32).T.astype(BF16)
        vt_ref[g, hd:, :] = jnp.where(lax.broadcasted_iota(jnp.int32, (sub, kt), 0) == 0, 1.0, 0.0).astype(BF16)

    def prepare_queries(q_block_ref, tile, rhs_slot):
        q = q_block_ref[...].astype(F32)
        ms = jnp.mean(q * q, axis=-1, keepdims=True)
        qn_t = (q * lax.rsqrt(ms + EPS) * qg_ref[...]).T

        gate = jnp.dot(kmean_ref[...], qn_t, preferred_element_type=F32, precision=lax.Precision.HIGHEST)
        blk = lax.broadcasted_iota(jnp.int32, gate.shape, 0)
        own = tile * qgroup + lax.broadcasted_iota(jnp.int32, gate.shape, 1) // bs
        blk_f = blk.astype(F32)
        past = blk < own
        g = jnp.where(past, gate, NEG)
        sel = jnp.zeros(gate.shape, jnp.bool_)
        for _ in range(min(MOBA_TOP_K, n_blocks)):
            top = jnp.max(g, axis=0, keepdims=True)
            first = jnp.min(jnp.where(g == top, blk_f, float(n_blocks)), axis=0, keepdims=True)
            pick = blk_f == first
            sel = jnp.logical_or(sel, pick)
            g = jnp.where(pick, -jnp.inf, g)
        sel = jnp.logical_and(sel, past)
        add = jnp.where(sel, slope2 * (bs * (blk - own)).astype(F32), jnp.where(blk == own, 0.0, NEG))

        row = lax.broadcasted_iota(jnp.int32, (sub, qt), 0)
        s_hi, s_mid, s_lo = _split_bf16(slope2)
        slope_rows = jnp.where(row == 0, s_hi, jnp.where(row == 1, s_mid, jnp.where(row == 2, s_lo, 0.0)))
        pad = jnp.zeros((2 * hd - MOBA_ADD_ROW - MOBA_PIECES * n_blocks, qt), F32)
        rhs_ref[rhs_slot] = jnp.concatenate([qn_t * (hd ** -0.5 * LOG2E), slope_rows, *_split_bf16(add), pad],
                                            axis=0).astype(BF16)

    own_groups = qgroup // kgroup

    @pl.when(it == 0)
    def _():
        kmean_ref[...] = jnp.zeros(kmean_ref.shape, F32)
        for t in range(own_groups):
            prepare_keys(t)
        prepare_queries(q_ref, 0, 0)

    rhs_now = it % 2

    def scores_to(slot, gi, causal_group=None):
        s = jnp.dot(kn_ref[gi], rhs_ref[rhs_now], preferred_element_type=F32)
        top = None
        for bi in range(kgroup):
            part = s[bi * bs:(bi + 1) * bs]
            if causal_group is not None:
                d0 = (causal_group * kgroup + bi) * bs
                tri = lax.broadcasted_iota(jnp.int32, (bs, bs), 0) <= lax.broadcasted_iota(jnp.int32, (bs, bs), 1)
                pieces = [part[:, :d0], jnp.where(tri, part[:, d0:d0 + bs], NEG), part[:, d0 + bs:]]
                part = jnp.concatenate([piece for piece in pieces if piece.shape[1]], axis=1)
            s_ref[slot, bi * bs:(bi + 1) * bs, :] = part
            top = part if top is None else jnp.maximum(top, part)
        return jnp.max(top, axis=0, keepdims=True)

    def accumulate(slot, gi, m, top, causal_group=None):
        m_new = jnp.maximum(m, top)
        alpha = jnp.exp2(m - m_new)
        for c0 in range(0, qt, strip):
            cols = slice(c0, c0 + strip)
            live = kgroup
            if causal_group is not None:
                live = min(max(c0 // bs - causal_group * kgroup + 1, 0), kgroup)
            if live == 0:
                continue
            p = jnp.exp2(s_ref[slot, :live * bs, cols] - m_new[:, cols]).astype(BF16)
            pv = jnp.dot(vt_ref[gi, :, :live * bs], p, preferred_element_type=F32)
            acc_ref[:, cols] = alpha[:, cols] * acc_ref[:, cols] + pv
        return m_new

    def pair(k, m):
        top0 = scores_to(0, 2 * k)
        top1 = scores_to(1, 2 * k + 1)
        return accumulate(1, 2 * k + 1, accumulate(0, 2 * k, m, top0), top1)

    acc_ref[...] = jnp.zeros(acc_ref.shape, F32)
    m = lax.fori_loop(0, it, pair, jnp.full((1, qt), 0.1 * NEG, F32))
    top0 = scores_to(0, 2 * it, causal_group=0)
    top1 = scores_to(1, 2 * it + 1, causal_group=1)
    accumulate(1, 2 * it + 1, accumulate(0, 2 * it, m, top0, causal_group=0), top1, causal_group=1)
    for t in range(own_groups):
        prepare_keys(jnp.minimum((it + 1) * own_groups + t, n_kgroups - 1))
    prepare_queries(q_next_ref, it + 1, 1 - rhs_now)
    acc = acc_ref[...]
    o = (acc[:hd] / acc[hd:hd + 1]).T
    o_ref[...] = (o * jax.nn.silu(z_ref[...].astype(F32))).astype(o_ref.dtype)


def _moba(slabs, q_gain, k_gain, bsz, seq, width):
    n_heads = width // ATTN_HEAD_DIM
    n_blocks = seq // MOBA_BLOCK
    hd, bs, sub = ATTN_HEAD_DIM, MOBA_BLOCK, V7X_BF16_SUBLANES
    kgroup, qgroup = MOBA_KEY_GROUP, MOBA_QUERY_GROUP
    assert qgroup == 2 * kgroup and n_blocks % qgroup == 0
    assert MOBA_ADD_ROW + MOBA_PIECES * n_blocks <= 2 * hd
    n_kgroups, n_qtiles, kt, qt = n_blocks // kgroup, n_blocks // qgroup, kgroup * bs, qgroup * bs
    slopes = jnp.asarray([[[2.0 ** (-8.0 * (h + 1) / n_heads)] * V7X_LANES] for h in range(n_heads)], F32)
    scratch = [
        pltpu.VMEM((n_kgroups, kt, 2 * hd), BF16),
        pltpu.VMEM((n_kgroups, hd + sub, kt), BF16),
        pltpu.VMEM((n_blocks, hd), F32),
        pltpu.VMEM((2, 2 * hd, qt), BF16),
        pltpu.VMEM((2, kt, qt), F32),
        pltpu.VMEM((hd + sub, qt), F32),
    ]
    scratch_bytes = (n_kgroups * (kt * 2 * hd + (hd + sub) * kt) * 2 + n_blocks * hd * 4
                     + 2 * 2 * hd * qt * 2 + 2 * kt * qt * 4 + (hd + sub) * qt * 4)
    return pl.pallas_call(
        functools.partial(_moba_kernel, n_blocks=n_blocks, kgroup=kgroup, qgroup=qgroup),
        grid=(bsz, n_heads, n_qtiles),
        in_specs=[
            pl.BlockSpec((None, qt, hd), lambda b, h, i: (h, b * n_qtiles + i, 0)),
            pl.BlockSpec((None, qt, hd), lambda b, h, i: (h, b * n_qtiles + jnp.minimum(i + 1, n_qtiles - 1), 0)),
            pl.BlockSpec((None, seq, hd), lambda b, h, i: (n_heads + h, b, 0)),
            pl.BlockSpec((None, seq, hd), lambda b, h, i: (2 * n_heads + h, b, 0)),
            pl.BlockSpec((None, qt, hd), lambda b, h, i: (3 * n_heads + h, b * n_qtiles + i, 0)),
            pl.BlockSpec((1, hd), lambda b, h, i: (0, 0)),
            pl.BlockSpec((1, hd), lambda b, h, i: (0, 0)),
            pl.BlockSpec((1, 1, V7X_LANES), lambda b, h, i: (h, 0, 0)),
        ],
        out_specs=pl.BlockSpec((qt, hd), lambda b, h, i: (b * n_qtiles + i, h)),
        out_shape=jax.ShapeDtypeStruct((bsz * seq, width), BF16),
        scratch_shapes=scratch,
        compiler_params=_params([2 * seq * hd * 2, 4 * qt * hd * 2], scratch_bytes=scratch_bytes,
                                temp_bytes=4 * kt * qt * 4, n_axes=3),
        name="moba_attention",
    )(slabs, slabs, slabs, slabs, slabs, q_gain.reshape(1, hd).astype(F32), k_gain.reshape(1, hd).astype(F32), slopes)


def _zoh(ar, ai, log_dt):
    dt = jnp.exp(log_dt)
    mag = jnp.exp(dt * ar)
    abr = mag * jnp.cos(dt * ai)
    abi = mag * jnp.sin(dt * ai)
    den = ar * ar + ai * ai
    nr = abr - 1.0
    return abr, abi, (nr * ar + abi * ai) / den, (abi * ar - nr * ai) / den


def _s5_kernel(u_ref, ar_ref, ai_ref, ldt_ref, bbr_ref, bbi_ref, ccr_ref, cci_ref, d_ref,
               ard_ref, aid_ref, ldtd_ref, bdr_ref, bdi_ref, cdr_ref, cdi_ref, y_ref,
               p_ref, qt_ref, t_ref, dk_ref, stage_ref, uc_ref, xs_ref, *, n_batch, rows_per_batch, row_tile):
    lb, gn, chunk = V7X_LANES, STATE_LANES, SSM_CHUNK
    kdim = chunk * lb
    col_tile = min(kdim, S5_OUT_COL_TILE)
    scan_rows = V7X_F32_SUBLANES
    assert rows_per_batch % scan_rows == 0
    abr, abi, f_re, f_im = _zoh(ar_ref[0], ai_ref[0], ldt_ref[0])

    abr_d, abi_d, f_re_d, f_im_d = _zoh(ard_ref[0], aid_ref[0], ldtd_ref[0])
    bd_re = f_re_d * bdr_ref[0] - f_im_d * bdi_ref[0]
    bd_im = f_re_d * bdi_ref[0] + f_im_d * bdr_ref[0]
    cd = jnp.concatenate([cdr_ref[0], -cdi_ref[0]], axis=1)
    same_group = (lax.broadcasted_iota(jnp.int32, (lb, lb), 0) // SSM_GROUP
                  == lax.broadcasted_iota(jnp.int32, (lb, lb), 1) // SSM_GROUP)
    pr_d = jnp.ones(abr_d.shape, F32)
    pi_d = jnp.zeros(abr_d.shape, F32)
    for tau in range(chunk):
        lag = jnp.concatenate([bd_re * pr_d - bd_im * pi_d, bd_re * pi_d + bd_im * pr_d], axis=1)
        blocks = lax.dot_general(lag, cd, (((1,), (1,)), ((), ())), preferred_element_type=F32,
                                 precision=lax.Precision.HIGHEST)
        dk_ref[tau] = jnp.where(same_group, blocks, 0.0).astype(BF16)
        pr_d, pi_d = pr_d * abr_d - pi_d * abi_d, pr_d * abi_d + pi_d * abr_d

    row_g = lax.broadcasted_iota(jnp.int32, (lb, gn), 0) // SSM_GROUP
    col_g = lax.broadcasted_iota(jnp.int32, (lb, gn), 1) // SSM_STATE
    same = row_g == col_g
    bb_re, bb_im = bbr_ref[0], bbi_ref[0]
    bbar_re = jnp.where(same, f_re * bb_re - f_im * bb_im, 0.0)
    bbar_im = jnp.where(same, f_re * bb_im + f_im * bb_re, 0.0)
    cc_re = jnp.where(same, ccr_ref[0], 0.0)
    cc_im = jnp.where(same, cci_ref[0], 0.0)

    pr = jnp.ones((1, gn), F32)
    pi = jnp.zeros((1, gn), F32)
    for tau in range(chunk + 1):
        if tau < chunk:
            s = chunk - 1 - tau
            p_ref[s * lb:(s + 1) * lb, :] = jnp.concatenate(
                [bbar_re * pr - bbar_im * pi, bbar_re * pi + bbar_im * pr], axis=1).astype(BF16)
        if tau >= 1:
            t = tau - 1
            qt_ref[t * lb:(t + 1) * lb, :] = jnp.concatenate(
                [cc_re * pr - cc_im * pi, -(cc_re * pi + cc_im * pr)], axis=1).astype(BF16)
        if tau < chunk:
            pr, pi = pr * abr - pi * abi, pr * abi + pi * abr
    al_re, al_im = pr, pi

    zero = jnp.zeros((lb, lb), BF16)
    for s in range(chunk):
        for t in range(chunk):
            t_ref[s * lb:(s + 1) * lb, t * lb:(t + 1) * lb] = dk_ref[t - s] if t >= s else zero

    n_rows = n_batch * rows_per_batch
    steps_per_batch = rows_per_batch * chunk
    for b in range(n_batch):
        stage_ref[...] = u_ref[0, b * steps_per_batch:(b + 1) * steps_per_batch, :].astype(F32)
        for s in range(chunk):
            uc_ref[b * rows_per_batch:(b + 1) * rows_per_batch, s * lb:(s + 1) * lb] = (
                stage_ref[pl.ds(s, rows_per_batch, stride=chunk), :].astype(BF16))

    for r0 in range(0, n_rows, row_tile):
        xs_ref[r0:r0 + row_tile, :] = jnp.dot(uc_ref[r0:r0 + row_tile, :], p_ref[...], preferred_element_type=F32)

    def step(c, carry):
        out = []
        for b in range(n_batch):
            xr, xi = carry[b]
            rows = pl.ds(pl.multiple_of(b * rows_per_batch + c * scan_rows, scan_rows), scan_rows)
            inc = xs_ref[rows, :]
            starts_re, starts_im = [], []
            for k in range(scan_rows):
                starts_re.append(xr)
                starts_im.append(xi)
                xr, xi = (al_re * xr - al_im * xi + inc[k:k + 1, :gn], al_re * xi + al_im * xr + inc[k:k + 1, gn:])
            xs_ref[rows, :gn] = jnp.concatenate(starts_re, axis=0)
            xs_ref[rows, gn:] = jnp.concatenate(starts_im, axis=0)
            out.append((xr, xi))
        return tuple(out)

    x0 = jnp.zeros((1, gn), F32)
    lax.fori_loop(0, rows_per_batch // scan_rows, step, tuple((x0, x0) for _ in range(n_batch)))

    for b in range(n_batch):
        for r0 in range(0, rows_per_batch, row_tile):
            rows = slice(b * rows_per_batch + r0, b * rows_per_batch + r0 + row_tile)
            x_start = xs_ref[rows, :].astype(BF16)
            for c0 in range(0, kdim, col_tile):
                cols = slice(c0, c0 + col_tile)
                y = jnp.dot(uc_ref[rows, :c0 + col_tile], t_ref[:c0 + col_tile, cols], preferred_element_type=F32)
                y = y + lax.dot_general(x_start, qt_ref[cols, :], (((1,), (1,)), ((), ())),
                                        preferred_element_type=F32)
                y = jax.nn.gelu(y + d_ref[0][:, cols] * uc_ref[rows, cols].astype(F32))
                for s in range(c0 // lb, (c0 + col_tile) // lb):
                    stage_ref[pl.ds(r0 * chunk + s, row_tile, stride=chunk), :] = y[:, s * lb - c0:(s + 1) * lb - c0]
        y_ref[0, b * steps_per_batch:(b + 1) * steps_per_batch, :] = stage_ref[...].astype(y_ref.dtype)


def _s5(slabs, slab0, nb, a_re, a_im, log_dt, b_re, b_im, c_re, c_im, d_skip, n_batch):
    _, n_steps, lb = slabs.shape
    gpb, gn, chunk = GROUPS_PER_LANE_BLOCK, STATE_LANES, SSM_CHUNK
    n_rows, kdim = n_steps // chunk, chunk * lb
    rows_per_batch = n_rows // n_batch
    row_tile = _tile(rows_per_batch, 256)

    def lane_row(v):
        return v.astype(F32).reshape(nb, 1, gn)

    ldt = jnp.repeat(log_dt.astype(F32), SSM_STATE).reshape(nb, 1, gn)
    bt_re = jnp.tile(b_re.astype(F32).reshape(nb, gpb, SSM_STATE, SSM_GROUP).transpose(0, 3, 1, 2).reshape(nb, SSM_GROUP, gn), (1, gpb, 1))
    bt_im = jnp.tile(b_im.astype(F32).reshape(nb, gpb, SSM_STATE, SSM_GROUP).transpose(0, 3, 1, 2).reshape(nb, SSM_GROUP, gn), (1, gpb, 1))
    ct_re = jnp.tile(c_re.astype(F32).reshape(nb, lb, SSM_STATE), (1, 1, gpb))
    ct_im = jnp.tile(c_im.astype(F32).reshape(nb, lb, SSM_STATE), (1, 1, gpb))
    d_row = jnp.tile(d_skip.astype(F32).reshape(nb, 1, lb), (1, 1, chunk))

    def per_channel_rows(v):
        return jnp.repeat(v.astype(F32).reshape(nb, gpb, SSM_STATE), SSM_GROUP, axis=1)

    ldt_d = per_channel_rows(jnp.broadcast_to(log_dt[:, None], a_re.shape))
    bd_re = b_re.astype(F32).reshape(nb, gpb, SSM_STATE, SSM_GROUP).transpose(0, 1, 3, 2).reshape(nb, lb, SSM_STATE)
    bd_im = b_im.astype(F32).reshape(nb, gpb, SSM_STATE, SSM_GROUP).transpose(0, 1, 3, 2).reshape(nb, lb, SSM_STATE)
    cd_re = c_re.astype(F32).reshape(nb, lb, SSM_STATE)
    cd_im = c_im.astype(F32).reshape(nb, lb, SSM_STATE)
    dia = pl.BlockSpec((1, lb, SSM_STATE), lambda j: (j, 0, 0))

    vec = pl.BlockSpec((1, 1, gn), lambda j: (j, 0, 0))
    mat = pl.BlockSpec((1, lb, gn), lambda j: (j, 0, 0))
    blk = pl.BlockSpec((1, n_steps, lb), lambda j: (j, 0, 0))
    blk_in = pl.BlockSpec((1, n_steps, lb), lambda j: (slab0 + j, 0, 0))
    scratch = [
        pltpu.VMEM((kdim, 2 * gn), BF16),
        pltpu.VMEM((kdim, 2 * gn), BF16),
        pltpu.VMEM((kdim, kdim), BF16),
        pltpu.VMEM((chunk, lb, lb), BF16),
        pltpu.VMEM((n_steps // n_batch, lb), F32),
        pltpu.VMEM((n_rows, kdim), BF16),
        pltpu.VMEM((n_rows, 2 * gn), F32),
    ]
    scratch_bytes = (2 * kdim * 2 * gn * 2 + kdim * kdim * 2 + chunk * lb * lb * 2 + n_steps // n_batch * lb * 4
                     + n_rows * kdim * 2 + n_rows * 2 * gn * 4)
    return pl.pallas_call(
        functools.partial(_s5_kernel, n_batch=n_batch, rows_per_batch=rows_per_batch, row_tile=row_tile),
        grid=(nb,),
        in_specs=[blk_in, vec, vec, vec, mat, mat, mat, mat, pl.BlockSpec((1, 1, kdim), lambda j: (j, 0, 0))] + [dia] * 7,
        out_specs=blk,
        out_shape=jax.ShapeDtypeStruct((nb, n_steps, lb), BF16),
        scratch_shapes=scratch,
        compiler_params=_params([n_steps * lb * 2, n_steps * lb * 2, 4 * lb * gn * 4], scratch_bytes=scratch_bytes,
                                temp_bytes=4 * row_tile * kdim * 4),
        name="s5_scan",
    )(slabs, lane_row(a_re), lane_row(a_im), ldt, bt_re, bt_im, ct_re, ct_im, d_row,
      per_channel_rows(a_re), per_channel_rows(a_im), ldt_d, bd_re, bd_im, cd_re, cd_im)


def _glu_kernel(y_ref, w_ref, b_ref, z_ref, o_ref, *, col_tile):
    lb = y_ref.shape[2]
    y = jnp.concatenate([y_ref[k] for k in range(y_ref.shape[0])], axis=1)
    for c0 in range(0, o_ref.shape[1], col_tile):
        cols = slice(c0, c0 + col_tile)
        a = jnp.dot(y, w_ref[:, cols], preferred_element_type=F32) + b_ref[:, cols]
        z = z_ref[:, cols].astype(F32)
        ycol = jnp.concatenate([y_ref[k] for k in range(c0 // lb, (c0 + col_tile) // lb)], axis=1).astype(F32)
        o_ref[:, cols] = (ycol * z / ((1.0 + jnp.exp(-a)) * (1.0 + jnp.exp(-z)))).astype(o_ref.dtype)


def _glu(y_blocks, w_glu, b_glu, proj, z_col0, tm, col_tile):
    nb, n, lb = y_blocks.shape
    width = nb * lb
    tm, col_tile = _tile(n, tm), _tile(width, col_tile)
    assert z_col0 % width == 0
    return pl.pallas_call(
        functools.partial(_glu_kernel, col_tile=col_tile),
        grid=(n // tm,),
        in_specs=[
            pl.BlockSpec((nb, tm, lb), lambda i: (0, i, 0)),
            pl.BlockSpec((width, width), lambda i: (0, 0)),
            pl.BlockSpec((1, width), lambda i: (0, 0)),
            pl.BlockSpec((tm, width), lambda i: (i, z_col0 // width)),
        ],
        out_specs=pl.BlockSpec((tm, width), lambda i: (i, 0)),
        out_shape=jax.ShapeDtypeStruct((n, width), BF16),
        compiler_params=_params([tm * width * 2, width * width * 2, tm * width * 2, tm * width * 2],
                                temp_bytes=tm * width * 2 + 4 * tm * col_tile * 4),
        name="s5_glu",
    )(y_blocks, w_glu, b_glu.reshape(1, width).astype(F32), proj)


def _memattn_kernel(q_ref, z_ref, kv_ref, qg_ref, kg_ref, o_ref, kn_ref, *, width):
    dm = width // MEM_HEADS

    @pl.when(pl.program_id(1) == 0)
    def _():
        for hd in range(MEM_HEADS):
            cols = slice(hd * dm, (hd + 1) * dm)
            k = kv_ref[:, cols].astype(F32)
            kn = k * lax.rsqrt(jnp.mean(k * k, axis=-1, keepdims=True) + EPS) * kg_ref[...]
            kn_ref[:, cols] = kn.astype(BF16)

    for hd in range(MEM_HEADS):
        cols = slice(hd * dm, (hd + 1) * dm)
        q = q_ref[:, cols].astype(F32)
        qn = q * lax.rsqrt(jnp.mean(q * q, axis=-1, keepdims=True) + EPS) * qg_ref[...]
        v = kv_ref[:, width + hd * dm:width + (hd + 1) * dm]
        s = lax.dot_general(qn.astype(BF16), kn_ref[:, cols], (((1,), (1,)), ((), ())),
                            preferred_element_type=F32) * (dm ** -0.5)
        p = jnp.exp(s - jnp.max(s, axis=-1, keepdims=True))
        l = jnp.sum(p, axis=-1, keepdims=True)
        o = jnp.dot(p.astype(BF16), v, preferred_element_type=F32) / l
        o_ref[:, cols] = (o * jax.nn.silu(z_ref[:, cols].astype(F32))).astype(o_ref.dtype)


def _memattn(proj, q_col0, z_col0, kv, q_gain, k_gain, bsz, seq, width, tq):
    assert q_col0 % width == 0 and z_col0 % width == 0
    dm = width // MEM_HEADS
    n_mem = kv.shape[0] // bsz
    tq = _tile(seq, tq)
    nq = seq // tq
    return pl.pallas_call(
        functools.partial(_memattn_kernel, width=width),
        grid=(bsz, nq),
        in_specs=[
            pl.BlockSpec((tq, width), lambda b, i: (b * nq + i, q_col0 // width)),
            pl.BlockSpec((tq, width), lambda b, i: (b * nq + i, z_col0 // width)),
            pl.BlockSpec((n_mem, 2 * width), lambda b, i: (b, 0)),
            pl.BlockSpec((1, dm), lambda b, i: (0, 0)),
            pl.BlockSpec((1, dm), lambda b, i: (0, 0)),
        ],
        out_specs=pl.BlockSpec((tq, width), lambda b, i: (b * nq + i, 0)),
        out_shape=jax.ShapeDtypeStruct((bsz * seq, width), BF16),
        scratch_shapes=[pltpu.VMEM((n_mem, width), BF16)],
        compiler_params=_params([3 * tq * width * 2, n_mem * 2 * width * 2], scratch_bytes=n_mem * width * 2,
                                temp_bytes=8 * tq * dm * 4, n_axes=2),
        name="memory_attention",
    )(proj, proj, kv, q_gain.reshape(1, dm).astype(F32), k_gain.reshape(1, dm).astype(F32))


def _merge_kernel(ya_ref, ys_ref, yc_ref, wa_ref, ws_ref, wc_ref, ga_ref, gs_ref, gc_ref, o_ref):
    def term(y_ref, w_ref, g_ref):
        return jax.nn.sigmoid(g_ref[...].astype(F32)) * jnp.dot(y_ref[...], w_ref[...], preferred_element_type=F32)

    o_ref[...] = (term(ya_ref, wa_ref, ga_ref) + term(ys_ref, ws_ref, gs_ref)
                  + term(yc_ref, wc_ref, gc_ref)).astype(o_ref.dtype)


def _merge(y_a, y_s, y_c, w_a, w_s, w_c, proj, g_col0, tm, tn):
    n, width = y_a.shape
    d = w_a.shape[1]
    tm, tn = _tile(n, tm), _tile(d, tn)
    y_spec = pl.BlockSpec((tm, width), lambda i, j: (i, 0))
    w_spec = pl.BlockSpec((width, tn), lambda i, j: (0, j))

    def g_spec(branch):
        assert (g_col0 + branch * d) % tn == 0
        return pl.BlockSpec((tm, tn), lambda i, j: (i, (g_col0 + branch * d) // tn + j))

    return pl.pallas_call(
        _merge_kernel,
        grid=(n // tm, d // tn),
        in_specs=[y_spec] * 3 + [w_spec] * 3 + [g_spec(0), g_spec(1), g_spec(2)],
        out_specs=pl.BlockSpec((tm, tn), lambda i, j: (i, j)),
        out_shape=jax.ShapeDtypeStruct((n, d), BF16),
        compiler_params=_params([3 * tm * width * 2, 3 * width * tn * 2, 4 * tm * tn * 2], temp_bytes=4 * tm * tn * 4, n_axes=2),
        name="branch_merge",
    )(y_a, y_s, y_c, w_a, w_s, w_c, proj, proj, proj)


def _outproj_kernel(m_ref, w_ref, x_ref, o_ref):
    o_ref[...] = x_ref[...] + jnp.dot(m_ref[...], w_ref[...], preferred_element_type=F32)


def _outproj(merged, w_out, x, tm, tn):
    n, d = merged.shape
    tm, tn = _tile(n, tm), _tile(d, tn)
    return pl.pallas_call(
        _outproj_kernel,
        grid=(n // tm, d // tn),
        in_specs=[
            pl.BlockSpec((tm, d), lambda i, j: (i, 0)),
            pl.BlockSpec((d, tn), lambda i, j: (0, j)),
            pl.BlockSpec((tm, tn), lambda i, j: (i, j)),
        ],
        out_specs=pl.BlockSpec((tm, tn), lambda i, j: (i, j)),
        out_shape=jax.ShapeDtypeStruct((n, d), F32),
        compiler_params=_params([tm * d * 2, d * tn * 2, 2 * tm * tn * 4], temp_bytes=tm * tn * 4, n_axes=2),
        name="out_projection",
    )(merged, w_out, x)


def kernel(x, mem, w_in, g_norm, g_mem, w_mem_kv, q_gain_a, k_gain_a, q_gain_c, k_gain_c, ssm_a_re, ssm_a_im, ssm_log_dt, ssm_b_re, ssm_b_im, ssm_c_re, ssm_c_im, ssm_d, w_glu, b_glu, w_br_a, w_br_s, w_br_c, w_out):
    bsz, seq, d_model = x.shape
    width = w_glu.shape[0]
    n_tok = bsz * seq
    n_mem = mem.shape[1]
    assert seq % MOBA_BLOCK == 0 and seq % SSM_CHUNK == 0 and width % V7X_LANES == 0
    assert w_in.shape == (d_model, 8 * width + 3 * d_model)

    x2 = x.reshape(n_tok, d_model)
    h = _rmsnorm(x2, g_norm, rows=512)
    proj, slabs = _matmul(h, w_in.astype(BF16), 1024, _tile(width, 1024), "in_projection", slab_cols=5 * width)
    nb = width // V7X_LANES

    m = _rmsnorm(mem.reshape(bsz * n_mem, d_model), g_mem, rows=256)
    kv = _matmul(m, w_mem_kv, 512, 512, "memory_kv_projection")

    y_a = _moba(slabs, q_gain_a, k_gain_a, bsz, seq, width)

    y_g = _s5(slabs, 4 * nb, nb, ssm_a_re, ssm_a_im, ssm_log_dt, ssm_b_re, ssm_b_im, ssm_c_re, ssm_c_im, ssm_d, bsz)
    y_s = _glu(y_g, w_glu.astype(BF16), b_glu, proj, 0, 512, 512)

    y_c = _memattn(proj, width, 2 * width, kv, q_gain_c, k_gain_c, bsz, seq, width, 512)

    merged = _merge(y_a, y_s, y_c, w_br_a.astype(BF16), w_br_s.astype(BF16), w_br_c.astype(BF16), proj,
                    3 * width, 1024, 512)
    out = _outproj(merged, w_out.astype(BF16), x2, 1024, 512)
    return out.reshape(bsz, seq, d_model)
```

```python
import functools

import jax
import jax.numpy as jnp
from jax import lax
from jax.experimental import pallas as pl
from jax.experimental.pallas import tpu as pltpu

F32 = jnp.float32
BF16 = jnp.bfloat16

V7X_LANES = 128
V7X_F32_SUBLANES = 8
V7X_BF16_SUBLANES = 16
V7X_VMEM_BYTES = 64 * 1024 * 1024
V7X_VMEM_RESERVE_BYTES = 6 * 1024 * 1024

ATTN_HEAD_DIM = 128
MOBA_BLOCK = 256
MOBA_TOP_K = 3
SSM_GROUP = 16
SSM_STATE = 64
MEM_HEADS = 4
EPS = 1e-6
NEG = -1e30
LOG2E = 1.4426950408889634

MOBA_KEY_GROUP = 2
MOBA_QUERY_GROUP = 4
MOBA_QUERY_STRIP = 256
SSM_CHUNK = 16
S5_OUT_COL_TILE = 512
GROUPS_PER_LANE_BLOCK = V7X_LANES // SSM_GROUP
STATE_LANES = GROUPS_PER_LANE_BLOCK * SSM_STATE

TILE_IN_PROJECTION = (1024, 1024)
TILE_MEMORY_KV = (512, 512)
TILE_BRANCH_MERGE = (1024, 512)
TILE_OUT_PROJECTION = (1024, 512)
TILE_GLU = (512, 512)
ROWS_RMSNORM = 512
ROWS_MEMORY_ATTENTION = 512


def _tile(n, pref):
    t = min(n, pref)
    while n % t:
        t -= V7X_LANES
    assert t > 0
    return t


def _params(block_bytes, scratch_bytes=0, temp_bytes=0, n_axes=1):
    need = 2 * sum(block_bytes) + scratch_bytes + temp_bytes
    limit = min(max(need, 16 * 1024 * 1024), V7X_VMEM_BYTES - V7X_VMEM_RESERVE_BYTES)
    return pltpu.CompilerParams(dimension_semantics=("arbitrary",) * n_axes, vmem_limit_bytes=int(limit))


def _rmsnorm_kernel(x_ref, g_ref, o_ref):
    x = x_ref[...].astype(F32)
    ms = jnp.mean(x * x, axis=-1, keepdims=True)
    o_ref[...] = (x * lax.rsqrt(ms + EPS) * g_ref[...]).astype(o_ref.dtype)


def _rmsnorm(x, gain, rows):
    n, d = x.shape
    tm = _tile(n, rows)
    return pl.pallas_call(
        _rmsnorm_kernel,
        grid=(n // tm,),
        in_specs=[pl.BlockSpec((tm, d), lambda i: (i, 0)), pl.BlockSpec((1, d), lambda i: (0, 0))],
        out_specs=pl.BlockSpec((tm, d), lambda i: (i, 0)),
        out_shape=jax.ShapeDtypeStruct((n, d), BF16),
        compiler_params=_params([tm * d * 4, tm * d * 2], temp_bytes=2 * tm * d * 4),
        name="rmsnorm",
    )(x, gain.reshape(1, d).astype(F32))


def _mm_kernel(a_ref, b_ref, o_ref):
    o_ref[...] = jnp.dot(a_ref[...], b_ref[...].astype(BF16), preferred_element_type=F32).astype(o_ref.dtype)


def _mm_slabs_kernel(a_ref, b_ref, o_ref, slab_ref):
    res = jnp.dot(a_ref[...], b_ref[...], preferred_element_type=F32).astype(o_ref.dtype)
    o_ref[...] = res
    for k in range(slab_ref.shape[0]):
        slab_ref[k] = res[:, k * V7X_LANES:(k + 1) * V7X_LANES]


def _matmul(a, b, tm, tn, name, slab_cols=0):
    m, k = a.shape
    _, n = b.shape
    tm, tn = _tile(m, tm), _tile(n, tn)
    in_specs = [pl.BlockSpec((tm, k), lambda i, j: (i, 0)), pl.BlockSpec((k, tn), lambda i, j: (0, j))]
    b_cast_bytes = k * tn * 2 if b.dtype != BF16 else 0
    params = _params([tm * k * 2, k * tn * b.dtype.itemsize, 2 * tm * tn * 2],
                     temp_bytes=tm * tn * (4 + 4 + 2) + b_cast_bytes, n_axes=2)
    if not slab_cols:
        return pl.pallas_call(_mm_kernel, grid=(m // tm, n // tn), in_specs=in_specs,
                              out_specs=pl.BlockSpec((tm, tn), lambda i, j: (i, j)),
                              out_shape=jax.ShapeDtypeStruct((m, n), BF16), compiler_params=params, name=name)(a, b)
    assert slab_cols % tn == 0 and 0 < slab_cols < n
    nj, per = slab_cols // tn, tn // V7X_LANES
    out_spec = pl.BlockSpec((tm, tn), lambda i, j: (i, jnp.maximum(j - nj, 0)))
    slab_spec = pl.BlockSpec((per, tm, V7X_LANES), lambda i, j: (jnp.minimum(j, nj), i, 0))
    return pl.pallas_call(
        _mm_slabs_kernel,
        grid=(m // tm, n // tn), in_specs=in_specs, out_specs=[out_spec, slab_spec],
        out_shape=[jax.ShapeDtypeStruct((m, n - slab_cols), BF16),
                   jax.ShapeDtypeStruct(((nj + 1) * per, m, V7X_LANES), BF16)],
        compiler_params=params, name=name)(a, b)


MOBA_ADD_ROW = ATTN_HEAD_DIM + V7X_BF16_SUBLANES
MOBA_PIECES = 3


def _split_bf16(x):
    hi = x.astype(BF16).astype(F32)
    mid = (x - hi).astype(BF16).astype(F32)
    return hi, mid, x - hi - mid


def _moba_kernel(q_ref, q_next_ref, k_ref, v_ref, z_ref, qg_ref, kg_ref, slope_ref, o_ref,
                 kn_ref, vt_ref, kmean_ref, rhs_ref, s_ref, acc_ref, *, n_blocks, kgroup, qgroup):
    it = pl.program_id(2)
    bs, hd, sub = MOBA_BLOCK, ATTN_HEAD_DIM, V7X_BF16_SUBLANES
    kt, qt = kgroup * bs, qgroup * bs
    n_kgroups = n_blocks // kgroup
    strip = min(qt, MOBA_QUERY_STRIP)
    slope2 = slope_ref[0][:, :1] * LOG2E

    def prepare_keys(g):
        key_off = lax.broadcasted_iota(jnp.int32, (bs, hd), 0).astype(F32)
        lane = lax.broadcasted_iota(jnp.int32, (bs, hd), 1)
        add_lane = lane - (MOBA_ADD_ROW - hd)
        in_add = jnp.logical_and(add_lane >= 0, add_lane < MOBA_PIECES * n_blocks)
        for bi in range(kgroup):
            c = g * kgroup + bi
            rows = pl.ds(pl.multiple_of(c * bs, bs), bs)
            kb = k_ref[rows, :].astype(F32)
            ms = jnp.mean(kb * kb, axis=-1, keepdims=True)
            kn = kb * lax.rsqrt(ms + EPS) * kg_ref[...]
            k_aug = jnp.where(lane < MOBA_PIECES, key_off,
                              jnp.where(jnp.logical_and(in_add, add_lane % n_blocks == c), 1.0, 0.0))
            kn_ref[g, bi * bs:(bi + 1) * bs, :] = jnp.concatenate([kn.astype(BF16), k_aug.astype(BF16)], axis=1)
            kmean_ref[pl.ds(c, 1), :] = jnp.mean(kn, axis=0, keepdims=True)
            vt_ref[g, :hd, bi * bs:(bi + 1) * bs] = v_ref[rows, :].astype(F32).T.astype(BF16)
        vt_ref[g, hd:, :] = jnp.where(lax.broadcasted_iota(jnp.int32, (sub, kt), 0) == 0, 1.0, 0.0).astype(BF16)

    def prepare_queries(q_block_ref, tile, rhs_slot):
        q = q_block_ref[...].astype(F32)
        ms = jnp.mean(q * q, axis=-1, keepdims=True)
        qn_t = (q * lax.rsqrt(ms + EPS) * qg_ref[...]).T

        gate = jnp.dot(kmean_ref[...], qn_t, preferred_element_type=F32, precision=lax.Precision.HIGHEST)
        blk = lax.broadcasted_iota(jnp.int32, gate.shape, 0)
        own = tile * qgroup + lax.broadcasted_iota(jnp.int32, gate.shape, 1) // bs
        blk_f = blk.astype(F32)
        past = blk < own
        g = jnp.where(past, gate, NEG)
        sel = jnp.zeros(gate.shape, jnp.bool_)
        for _ in range(min(MOBA_TOP_K, n_blocks)):
            top = jnp.max(g, axis=0, keepdims=True)
            first = jnp.min(jnp.where(g == top, blk_f, float(n_blocks)), axis=0, keepdims=True)
            pick = blk_f == first
            sel = jnp.logical_or(sel, pick)
            g = jnp.where(pick, -jnp.inf, g)
        sel = jnp.logical_and(sel, past)
        add = jnp.where(sel, slope2 * (bs * (blk - own)).astype(F32), jnp.where(blk == own, 0.0, NEG))

        row = lax.broadcasted_iota(jnp.int32, (sub, qt), 0)
        s_hi, s_mid, s_lo = _split_bf16(slope2)
        slope_rows = jnp.where(row == 0, s_hi, jnp.where(row == 1, s_mid, jnp.where(row == 2, s_lo, 0.0)))
        pad = jnp.zeros((2 * hd - MOBA_ADD_ROW - MOBA_PIECES * n_blocks, qt), F32)
        rhs_ref[rhs_slot] = jnp.concatenate([qn_t * (hd ** -0.5 * LOG2E), slope_rows, *_split_bf16(add), pad],
                                            axis=0).astype(BF16)

    own_groups = qgroup // kgroup

    @pl.when(it == 0)
    def _():
        kmean_ref[...] = jnp.zeros(kmean_ref.shape, F32)
        for t in range(own_groups):
            prepare_keys(t)
        prepare_queries(q_ref, 0, 0)

    rhs_now = it % 2

    def scores_to(slot, gi, causal_group=None):
        s = jnp.dot(kn_ref[gi], rhs_ref[rhs_now], preferred_element_type=F32)
        top = None
        for bi in range(kgroup):
            part = s[bi * bs:(bi + 1) * bs]
            if causal_group is not None:
                d0 = (causal_group * kgroup + bi) * bs
                tri = lax.broadcasted_iota(jnp.int32, (bs, bs), 0) <= lax.broadcasted_iota(jnp.int32, (bs, bs), 1)
                pieces = [jnp.where(tri, part[:, d0:d0 + bs], NEG)]
                if d0 > 0:
                    pieces.insert(0, part[:, :d0])
                if d0 + bs < qt:
                    pieces.append(part[:, d0 + bs:])
                part = jnp.concatenate(pieces, axis=1)
            s_ref[slot, bi * bs:(bi + 1) * bs, :] = part
            top = part if top is None else jnp.maximum(top, part)
        return jnp.max(top, axis=0, keepdims=True)

    def accumulate(slot, gi, m, top, causal_group=None):
        m_new = jnp.maximum(m, top)
        alpha = jnp.exp2(m - m_new)
        for c0 in range(0, qt, strip):
            cols = slice(c0, c0 + strip)
            live = kgroup
            if causal_group is not None:
                live = min(max(c0 // bs - causal_group * kgroup + 1, 0), kgroup)
            if live == 0:
                continue
            p = jnp.exp2(s_ref[slot, :live * bs, cols] - m_new[:, cols]).astype(BF16)
            pv = jnp.dot(vt_ref[gi, :, :live * bs], p, preferred_element_type=F32)
            acc_ref[:, cols] = alpha[:, cols] * acc_ref[:, cols] + pv
        return m_new

    def pair(k, m):
        top0 = scores_to(0, 2 * k)
        top1 = scores_to(1, 2 * k + 1)
        return accumulate(1, 2 * k + 1, accumulate(0, 2 * k, m, top0), top1)

    acc_ref[...] = jnp.zeros(acc_ref.shape, F32)
    m = lax.fori_loop(0, it, pair, jnp.full((1, qt), 0.1 * NEG, F32))
    top0 = scores_to(0, 2 * it, causal_group=0)
    top1 = scores_to(1, 2 * it + 1, causal_group=1)
    accumulate(1, 2 * it + 1, accumulate(0, 2 * it, m, top0, causal_group=0), top1, causal_group=1)
    for t in range(own_groups):
        prepare_keys(jnp.minimum((it + 1) * own_groups + t, n_kgroups - 1))
    prepare_queries(q_next_ref, it + 1, 1 - rhs_now)
    acc = acc_ref[...]
    o = (acc[:hd] / acc[hd:hd + 1]).T
    o_ref[...] = (o * jax.nn.silu(z_ref[...].astype(F32))).astype(o_ref.dtype)


def _moba(slabs, q_gain, k_gain, bsz, seq, width):
    n_heads = width // ATTN_HEAD_DIM
    n_blocks = seq // MOBA_BLOCK
    hd, bs, sub = ATTN_HEAD_DIM, MOBA_BLOCK, V7X_BF16_SUBLANES
    kgroup, qgroup = MOBA_KEY_GROUP, MOBA_QUERY_GROUP
    assert qgroup == 2 * kgroup and n_blocks % qgroup == 0
    assert MOBA_ADD_ROW + MOBA_PIECES * n_blocks <= 2 * hd
    n_kgroups, n_qtiles, kt, qt = n_blocks // kgroup, n_blocks // qgroup, kgroup * bs, qgroup * bs
    slopes = jnp.asarray([[[2.0 ** (-8.0 * (h + 1) / n_heads)] * V7X_LANES] for h in range(n_heads)], F32)
    scratch = [
        pltpu.VMEM((n_kgroups, kt, 2 * hd), BF16),
        pltpu.VMEM((n_kgroups, hd + sub, kt), BF16),
        pltpu.VMEM((n_blocks, hd), F32),
        pltpu.VMEM((2, 2 * hd, qt), BF16),
        pltpu.VMEM((2, kt, qt), F32),
        pltpu.VMEM((hd + sub, qt), F32),
    ]
    scratch_bytes = (n_kgroups * (kt * 2 * hd + (hd + sub) * kt) * 2 + n_blocks * hd * 4
                     + 2 * 2 * hd * qt * 2 + 2 * kt * qt * 4 + (hd + sub) * qt * 4)
    return pl.pallas_call(
        functools.partial(_moba_kernel, n_blocks=n_blocks, kgroup=kgroup, qgroup=qgroup),
        grid=(bsz, n_heads, n_qtiles),
        in_specs=[
            pl.BlockSpec((None, qt, hd), lambda b, h, i: (h, b * n_qtiles + i, 0)),
            pl.BlockSpec((None, qt, hd), lambda b, h, i: (h, b * n_qtiles + jnp.minimum(i + 1, n_qtiles - 1), 0)),
            pl.BlockSpec((None, seq, hd), lambda b, h, i: (n_heads + h, b, 0)),
            pl.BlockSpec((None, seq, hd), lambda b, h, i: (2 * n_heads + h, b, 0)),
            pl.BlockSpec((None, qt, hd), lambda b, h, i: (3 * n_heads + h, b * n_qtiles + i, 0)),
            pl.BlockSpec((1, hd), lambda b, h, i: (0, 0)),
            pl.BlockSpec((1, hd), lambda b, h, i: (0, 0)),
            pl.BlockSpec((1, 1, V7X_LANES), lambda b, h, i: (h, 0, 0)),
        ],
        out_specs=pl.BlockSpec((qt, hd), lambda b, h, i: (b * n_qtiles + i, h)),
        out_shape=jax.ShapeDtypeStruct((bsz * seq, width), BF16),
        scratch_shapes=scratch,
        compiler_params=_params([2 * seq * hd * 2, 4 * qt * hd * 2], scratch_bytes=scratch_bytes,
                                temp_bytes=4 * kt * qt * 4, n_axes=3),
        name="moba_attention",
    )(slabs, slabs, slabs, slabs, slabs, q_gain.reshape(1, hd).astype(F32), k_gain.reshape(1, hd).astype(F32), slopes)


def _zoh(ar, ai, log_dt):
    dt = jnp.exp(log_dt)
    mag = jnp.exp(dt * ar)
    abr = mag * jnp.cos(dt * ai)
    abi = mag * jnp.sin(dt * ai)
    den = ar * ar + ai * ai
    nr = abr - 1.0
    return abr, abi, (nr * ar + abi * ai) / den, (abi * ar - nr * ai) / den


def _s5_kernel(u_ref, ar_ref, ai_ref, ldt_ref, bbr_ref, bbi_ref, ccr_ref, cci_ref, d_ref,
               ard_ref, aid_ref, ldtd_ref, bdr_ref, bdi_ref, cdr_ref, cdi_ref, y_ref,
               p_ref, qt_ref, t_ref, dk_ref, stage_ref, uc_ref, xs_ref, *, n_batch, rows_per_batch, row_tile):
    lb, gn, chunk = V7X_LANES, STATE_LANES, SSM_CHUNK
    kdim = chunk * lb
    col_tile = min(kdim, S5_OUT_COL_TILE)
    scan_rows = V7X_F32_SUBLANES
    assert rows_per_batch % scan_rows == 0
    abr, abi, f_re, f_im = _zoh(ar_ref[0], ai_ref[0], ldt_ref[0])

    abr_d, abi_d, f_re_d, f_im_d = _zoh(ard_ref[0], aid_ref[0], ldtd_ref[0])
    bd_re = f_re_d * bdr_ref[0] - f_im_d * bdi_ref[0]
    bd_im = f_re_d * bdi_ref[0] + f_im_d * bdr_ref[0]
    cd = jnp.concatenate([cdr_ref[0], -cdi_ref[0]], axis=1)
    same_group = (lax.broadcasted_iota(jnp.int32, (lb, lb), 0) // SSM_GROUP
                  == lax.broadcasted_iota(jnp.int32, (lb, lb), 1) // SSM_GROUP)
    pr_d = jnp.ones(abr_d.shape, F32)
    pi_d = jnp.zeros(abr_d.shape, F32)
    for tau in range(chunk):
        lag = jnp.concatenate([bd_re * pr_d - bd_im * pi_d, bd_re * pi_d + bd_im * pr_d], axis=1)
        blocks = lax.dot_general(lag, cd, (((1,), (1,)), ((), ())), preferred_element_type=F32,
                                 precision=lax.Precision.HIGHEST)
        dk_ref[tau] = jnp.where(same_group, blocks, 0.0).astype(BF16)
        pr_d, pi_d = pr_d * abr_d - pi_d * abi_d, pr_d * abi_d + pi_d * abr_d

    row_g = lax.broadcasted_iota(jnp.int32, (lb, gn), 0) // SSM_GROUP
    col_g = lax.broadcasted_iota(jnp.int32, (lb, gn), 1) // SSM_STATE
    same = row_g == col_g
    bb_re, bb_im = bbr_ref[0], bbi_ref[0]
    bbar_re = jnp.where(same, f_re * bb_re - f_im * bb_im, 0.0)
    bbar_im = jnp.where(same, f_re * bb_im + f_im * bb_re, 0.0)
    cc_re = jnp.where(same, ccr_ref[0], 0.0)
    cc_im = jnp.where(same, cci_ref[0], 0.0)

    pr = jnp.ones((1, gn), F32)
    pi = jnp.zeros((1, gn), F32)
    for tau in range(chunk + 1):
        if tau < chunk:
            s = chunk - 1 - tau
            p_ref[s * lb:(s + 1) * lb, :] = jnp.concatenate(
                [bbar_re * pr - bbar_im * pi, bbar_re * pi + bbar_im * pr], axis=1).astype(BF16)
        if tau >= 1:
            t = tau - 1
            qt_ref[t * lb:(t + 1) * lb, :] = jnp.concatenate(
                [cc_re * pr - cc_im * pi, -(cc_re * pi + cc_im * pr)], axis=1).astype(BF16)
        if tau < chunk:
            pr, pi = pr * abr - pi * abi, pr * abi + pi * abr
    al_re, al_im = pr, pi

    zero = jnp.zeros((lb, lb), BF16)
    for s in range(chunk):
        for t in range(chunk):
            t_ref[s * lb:(s + 1) * lb, t * lb:(t + 1) * lb] = dk_ref[t - s] if t >= s else zero

    n_rows = n_batch * rows_per_batch
    steps_per_batch = rows_per_batch * chunk
    for b in range(n_batch):
        stage_ref[...] = u_ref[0, b * steps_per_batch:(b + 1) * steps_per_batch, :].astype(F32)
        for s in range(chunk):
            uc_ref[b * rows_per_batch:(b + 1) * rows_per_batch, s * lb:(s + 1) * lb] = (
                stage_ref[pl.ds(s, rows_per_batch, stride=chunk), :].astype(BF16))

    for r0 in range(0, n_rows, row_tile):
        xs_ref[r0:r0 + row_tile, :] = jnp.dot(uc_ref[r0:r0 + row_tile, :], p_ref[...], preferred_element_type=F32)

    def step(c, carry):
        out = []
        for b in range(n_batch):
            xr, xi = carry[b]
            rows = pl.ds(pl.multiple_of(b * rows_per_batch + c * scan_rows, scan_rows), scan_rows)
            inc = xs_ref[rows, :]
            starts_re, starts_im = [], []
            for k in range(scan_rows):
                starts_re.append(xr)
                starts_im.append(xi)
                xr, xi = (al_re * xr - al_im * xi + inc[k:k + 1, :gn], al_re * xi + al_im * xr + inc[k:k + 1, gn:])
            xs_ref[rows, :gn] = jnp.concatenate(starts_re, axis=0)
            xs_ref[rows, gn:] = jnp.concatenate(starts_im, axis=0)
            out.append((xr, xi))
        return tuple(out)

    x0 = jnp.zeros((1, gn), F32)
    lax.fori_loop(0, rows_per_batch // scan_rows, step, tuple((x0, x0) for _ in range(n_batch)))

    for b in range(n_batch):
        for r0 in range(0, rows_per_batch, row_tile):
            rows = slice(b * rows_per_batch + r0, b * rows_per_batch + r0 + row_tile)
            x_start = xs_ref[rows, :].astype(BF16)
            for c0 in range(0, kdim, col_tile):
                cols = slice(c0, c0 + col_tile)
                y = jnp.dot(uc_ref[rows, :c0 + col_tile], t_ref[:c0 + col_tile, cols], preferred_element_type=F32)
                y = y + lax.dot_general(x_start, qt_ref[cols, :], (((1,), (1,)), ((), ())),
                                        preferred_element_type=F32)
                y = jax.nn.gelu(y + d_ref[0][:, cols] * uc_ref[rows, cols].astype(F32))
                for s in range(c0 // lb, (c0 + col_tile) // lb):
                    stage_ref[pl.ds(r0 * chunk + s, row_tile, stride=chunk), :] = y[:, s * lb - c0:(s + 1) * lb - c0]
        y_ref[0, b * steps_per_batch:(b + 1) * steps_per_batch, :] = stage_ref[...].astype(y_ref.dtype)


def _s5(slabs, slab0, nb, a_re, a_im, log_dt, b_re, b_im, c_re, c_im, d_skip, n_batch):
    _, n_steps, lb = slabs.shape
    gpb, gn, chunk = GROUPS_PER_LANE_BLOCK, STATE_LANES, SSM_CHUNK
    n_rows, kdim = n_steps // chunk, chunk * lb
    rows_per_batch = n_rows // n_batch
    row_tile = _tile(rows_per_batch, 256)

    def lane_row(v):
        return v.astype(F32).reshape(nb, 1, gn)

    ldt = jnp.repeat(log_dt.astype(F32), SSM_STATE).reshape(nb, 1, gn)
    bt_re = jnp.tile(b_re.astype(F32).reshape(nb, gpb, SSM_STATE, SSM_GROUP).transpose(0, 3, 1, 2).reshape(nb, SSM_GROUP, gn), (1, gpb, 1))
    bt_im = jnp.tile(b_im.astype(F32).reshape(nb, gpb, SSM_STATE, SSM_GROUP).transpose(0, 3, 1, 2).reshape(nb, SSM_GROUP, gn), (1, gpb, 1))
    ct_re = jnp.tile(c_re.astype(F32).reshape(nb, lb, SSM_STATE), (1, 1, gpb))
    ct_im = jnp.tile(c_im.astype(F32).reshape(nb, lb, SSM_STATE), (1, 1, gpb))
    d_row = jnp.tile(d_skip.astype(F32).reshape(nb, 1, lb), (1, 1, chunk))

    def per_channel_rows(v):
        return jnp.repeat(v.astype(F32).reshape(nb, gpb, SSM_STATE), SSM_GROUP, axis=1)

    ldt_d = per_channel_rows(jnp.broadcast_to(log_dt[:, None], a_re.shape))
    bd_re = b_re.astype(F32).reshape(nb, gpb, SSM_STATE, SSM_GROUP).transpose(0, 1, 3, 2).reshape(nb, lb, SSM_STATE)
    bd_im = b_im.astype(F32).reshape(nb, gpb, SSM_STATE, SSM_GROUP).transpose(0, 1, 3, 2).reshape(nb, lb, SSM_STATE)
    cd_re = c_re.astype(F32).reshape(nb, lb, SSM_STATE)
    cd_im = c_im.astype(F32).reshape(nb, lb, SSM_STATE)
    dia = pl.BlockSpec((1, lb, SSM_STATE), lambda j: (j, 0, 0))

    vec = pl.BlockSpec((1, 1, gn), lambda j: (j, 0, 0))
    mat = pl.BlockSpec((1, lb, gn), lambda j: (j, 0, 0))
    blk = pl.BlockSpec((1, n_steps, lb), lambda j: (j, 0, 0))
    blk_in = pl.BlockSpec((1, n_steps, lb), lambda j: (slab0 + j, 0, 0))
    scratch = [
        pltpu.VMEM((kdim, 2 * gn), BF16),
        pltpu.VMEM((kdim, 2 * gn), BF16),
        pltpu.VMEM((kdim, kdim), BF16),
        pltpu.VMEM((chunk, lb, lb), BF16),
        pltpu.VMEM((n_steps // n_batch, lb), F32),
        pltpu.VMEM((n_rows, kdim), BF16),
        pltpu.VMEM((n_rows, 2 * gn), F32),
    ]
    scratch_bytes = (2 * kdim * 2 * gn * 2 + kdim * kdim * 2 + chunk * lb * lb * 2 + n_steps // n_batch * lb * 4
                     + n_rows * kdim * 2 + n_rows * 2 * gn * 4)
    return pl.pallas_call(
        functools.partial(_s5_kernel, n_batch=n_batch, rows_per_batch=rows_per_batch, row_tile=row_tile),
        grid=(nb,),
        in_specs=[blk_in, vec, vec, vec, mat, mat, mat, mat, pl.BlockSpec((1, 1, kdim), lambda j: (j, 0, 0))] + [dia] * 7,
        out_specs=blk,
        out_shape=jax.ShapeDtypeStruct((nb, n_steps, lb), BF16),
        scratch_shapes=scratch,
        compiler_params=_params([n_steps * lb * 2, n_steps * lb * 2, 4 * lb * gn * 4], scratch_bytes=scratch_bytes,
                                temp_bytes=4 * row_tile * kdim * 4),
        name="s5_scan",
    )(slabs, lane_row(a_re), lane_row(a_im), ldt, bt_re, bt_im, ct_re, ct_im, d_row,
      per_channel_rows(a_re), per_channel_rows(a_im), ldt_d, bd_re, bd_im, cd_re, cd_im)


def _glu_kernel(y_ref, w_ref, b_ref, z_ref, o_ref, *, col_tile):
    lb = y_ref.shape[2]
    y = jnp.concatenate([y_ref[k] for k in range(y_ref.shape[0])], axis=1)
    for c0 in range(0, o_ref.shape[1], col_tile):
        cols = slice(c0, c0 + col_tile)
        a = jnp.dot(y, w_ref[:, cols], preferred_element_type=F32) + b_ref[:, cols]
        z = z_ref[:, cols].astype(F32)
        ycol = jnp.concatenate([y_ref[k] for k in range(c0 // lb, (c0 + col_tile) // lb)], axis=1).astype(F32)
        o_ref[:, cols] = (ycol * z / ((1.0 + jnp.exp(-a)) * (1.0 + jnp.exp(-z)))).astype(o_ref.dtype)


def _glu(y_blocks, w_glu, b_glu, proj, z_col0, tm, col_tile):
    nb, n, lb = y_blocks.shape
    width = nb * lb
    tm, col_tile = _tile(n, tm), _tile(width, col_tile)
    assert z_col0 % width == 0
    return pl.pallas_call(
        functools.partial(_glu_kernel, col_tile=col_tile),
        grid=(n // tm,),
        in_specs=[
            pl.BlockSpec((nb, tm, lb), lambda i: (0, i, 0)),
            pl.BlockSpec((width, width), lambda i: (0, 0)),
            pl.BlockSpec((1, width), lambda i: (0, 0)),
            pl.BlockSpec((tm, width), lambda i: (i, z_col0 // width)),
        ],
        out_specs=pl.BlockSpec((tm, width), lambda i: (i, 0)),
        out_shape=jax.ShapeDtypeStruct((n, width), BF16),
        compiler_params=_params([tm * width * 2, width * width * 2, tm * width * 2, tm * width * 2],
                                temp_bytes=tm * width * 2 + 4 * tm * col_tile * 4),
        name="s5_glu",
    )(y_blocks, w_glu, b_glu.reshape(1, width).astype(F32), proj)


def _memattn_kernel(q_ref, z_ref, kv_ref, qg_ref, kg_ref, o_ref, kn_ref, *, width):
    dm = width // MEM_HEADS

    @pl.when(pl.program_id(1) == 0)
    def _():
        for hd in range(MEM_HEADS):
            cols = slice(hd * dm, (hd + 1) * dm)
            k = kv_ref[:, cols].astype(F32)
            kn = k * lax.rsqrt(jnp.mean(k * k, axis=-1, keepdims=True) + EPS) * kg_ref[...]
            kn_ref[:, cols] = kn.astype(BF16)

    for hd in range(MEM_HEADS):
        cols = slice(hd * dm, (hd + 1) * dm)
        q = q_ref[:, cols].astype(F32)
        qn = q * lax.rsqrt(jnp.mean(q * q, axis=-1, keepdims=True) + EPS) * qg_ref[...]
        v = kv_ref[:, width + hd * dm:width + (hd + 1) * dm]
        s = lax.dot_general(qn.astype(BF16), kn_ref[:, cols], (((1,), (1,)), ((), ())),
                            preferred_element_type=F32) * (dm ** -0.5)
        p = jnp.exp(s - jnp.max(s, axis=-1, keepdims=True))
        l = jnp.sum(p, axis=-1, keepdims=True)
        o = jnp.dot(p.astype(BF16), v, preferred_element_type=F32) / l
        o_ref[:, cols] = (o * jax.nn.silu(z_ref[:, cols].astype(F32))).astype(o_ref.dtype)


def _memattn(proj, q_col0, z_col0, kv, q_gain, k_gain, bsz, seq, width, tq):
    assert q_col0 % width == 0 and z_col0 % width == 0
    dm = width // MEM_HEADS
    n_mem = kv.shape[0] // bsz
    tq = _tile(seq, tq)
    nq = seq // tq
    return pl.pallas_call(
        functools.partial(_memattn_kernel, width=width),
        grid=(bsz, nq),
        in_specs=[
            pl.BlockSpec((tq, width), lambda b, i: (b * nq + i, q_col0 // width)),
            pl.BlockSpec((tq, width), lambda b, i: (b * nq + i, z_col0 // width)),
            pl.BlockSpec((n_mem, 2 * width), lambda b, i: (b, 0)),
            pl.BlockSpec((1, dm), lambda b, i: (0, 0)),
            pl.BlockSpec((1, dm), lambda b, i: (0, 0)),
        ],
        out_specs=pl.BlockSpec((tq, width), lambda b, i: (b * nq + i, 0)),
        out_shape=jax.ShapeDtypeStruct((bsz * seq, width), BF16),
        scratch_shapes=[pltpu.VMEM((n_mem, width), BF16)],
        compiler_params=_params([3 * tq * width * 2, n_mem * 2 * width * 2], scratch_bytes=n_mem * width * 2,
                                temp_bytes=8 * tq * dm * 4, n_axes=2),
        name="memory_attention",
    )(proj, proj, kv, q_gain.reshape(1, dm).astype(F32), k_gain.reshape(1, dm).astype(F32))


def _merge_kernel(ya_ref, ys_ref, yc_ref, wa_ref, ws_ref, wc_ref, ga_ref, gs_ref, gc_ref, o_ref):
    def term(y_ref, w_ref, g_ref):
        return jax.nn.sigmoid(g_ref[...].astype(F32)) * jnp.dot(y_ref[...], w_ref[...], preferred_element_type=F32)

    o_ref[...] = (term(ya_ref, wa_ref, ga_ref) + term(ys_ref, ws_ref, gs_ref)
                  + term(yc_ref, wc_ref, gc_ref)).astype(o_ref.dtype)


def _merge(y_a, y_s, y_c, w_a, w_s, w_c, proj, g_col0, tm, tn):
    n, width = y_a.shape
    d = w_a.shape[1]
    tm, tn = _tile(n, tm), _tile(d, tn)
    y_spec = pl.BlockSpec((tm, width), lambda i, j: (i, 0))
    w_spec = pl.BlockSpec((width, tn), lambda i, j: (0, j))

    def g_spec(branch):
        assert (g_col0 + branch * d) % tn == 0
        return pl.BlockSpec((tm, tn), lambda i, j: (i, (g_col0 + branch * d) // tn + j))

    return pl.pallas_call(
        _merge_kernel,
        grid=(n // tm, d // tn),
        in_specs=[y_spec] * 3 + [w_spec] * 3 + [g_spec(0), g_spec(1), g_spec(2)],
        out_specs=pl.BlockSpec((tm, tn), lambda i, j: (i, j)),
        out_shape=jax.ShapeDtypeStruct((n, d), BF16),
        compiler_params=_params([3 * tm * width * 2, 3 * width * tn * 2, 4 * tm * tn * 2], temp_bytes=4 * tm * tn * 4, n_axes=2),
        name="branch_merge",
    )(y_a, y_s, y_c, w_a, w_s, w_c, proj, proj, proj)


def _outproj_kernel(m_ref, w_ref, x_ref, o_ref):
    o_ref[...] = x_ref[...] + jnp.dot(m_ref[...], w_ref[...], preferred_element_type=F32)


def _outproj(merged, w_out, x, tm, tn):
    n, d = merged.shape
    tm, tn = _tile(n, tm), _tile(d, tn)
    return pl.pallas_call(
        _outproj_kernel,
        grid=(n // tm, d // tn),
        in_specs=[
            pl.BlockSpec((tm, d), lambda i, j: (i, 0)),
            pl.BlockSpec((d, tn), lambda i, j: (0, j)),
            pl.BlockSpec((tm, tn), lambda i, j: (i, j)),
        ],
        out_specs=pl.BlockSpec((tm, tn), lambda i, j: (i, j)),
        out_shape=jax.ShapeDtypeStruct((n, d), F32),
        compiler_params=_params([tm * d * 2, d * tn * 2, 2 * tm * tn * 4], temp_bytes=tm * tn * 4, n_axes=2),
        name="out_projection",
    )(merged, w_out, x)


def kernel(x, mem, w_in, g_norm, g_mem, w_mem_kv, q_gain_a, k_gain_a, q_gain_c, k_gain_c, ssm_a_re, ssm_a_im, ssm_log_dt, ssm_b_re, ssm_b_im, ssm_c_re, ssm_c_im, ssm_d, w_glu, b_glu, w_br_a, w_br_s, w_br_c, w_out):
    bsz, seq, d_model = x.shape
    width = w_glu.shape[0]
    n_tok = bsz * seq
    n_mem = mem.shape[1]
    assert seq % MOBA_BLOCK == 0 and seq % SSM_CHUNK == 0 and width % V7X_LANES == 0
    assert w_in.shape == (d_model, 8 * width + 3 * d_model)

    x2 = x.reshape(n_tok, d_model)
    h = _rmsnorm(x2, g_norm, rows=ROWS_RMSNORM)
    proj, slabs = _matmul(h, w_in.astype(BF16), TILE_IN_PROJECTION[0], _tile(width, TILE_IN_PROJECTION[1]),
                          "in_projection", slab_cols=5 * width)
    nb = width // V7X_LANES

    m = _rmsnorm(mem.reshape(bsz * n_mem, d_model), g_mem, rows=ROWS_RMSNORM)
    kv = _matmul(m, w_mem_kv, *TILE_MEMORY_KV, "memory_kv_projection")

    y_a = _moba(slabs, q_gain_a, k_gain_a, bsz, seq, width)

    y_g = _s5(slabs, 4 * nb, nb, ssm_a_re, ssm_a_im, ssm_log_dt, ssm_b_re, ssm_b_im, ssm_c_re, ssm_c_im, ssm_d, bsz)
    y_s = _glu(y_g, w_glu.astype(BF16), b_glu, proj, 0, *TILE_GLU)

    y_c = _memattn(proj, width, 2 * width, kv, q_gain_c, k_gain_c, bsz, seq, width, ROWS_MEMORY_ATTENTION)

    merged = _merge(y_a, y_s, y_c, w_br_a.astype(BF16), w_br_s.astype(BF16), w_br_c.astype(BF16), proj,
                    3 * width, *TILE_BRANCH_MERGE)
    out = _outproj(merged, w_out.astype(BF16), x2, *TILE_OUT_PROJECTION)
    return out.reshape(bsz, seq, d_model)
```

```python
import functools

import jax
import jax.numpy as jnp
from jax import lax
from jax.experimental import pallas as pl
from jax.experimental.pallas import tpu as pltpu

F32 = jnp.float32
BF16 = jnp.bfloat16

V7X_LANES = 128
V7X_F32_SUBLANES = 8
V7X_BF16_SUBLANES = 16
V7X_VMEM_BYTES = 64 * 1024 * 1024
V7X_VMEM_RESERVE_BYTES = 6 * 1024 * 1024

ATTN_HEAD_DIM = 128
MOBA_BLOCK = 256
MOBA_TOP_K = 3
SSM_GROUP = 16
SSM_STATE = 64
MEM_HEADS = 4
EPS = 1e-6
NEG = -1e30
LOG2E = 1.4426950408889634

MOBA_KEY_GROUP = 2
MOBA_QUERY_GROUP = 4
MOBA_QUERY_STRIP = 256
SSM_CHUNK = 16
S5_OUT_COL_TILE = 512
GROUPS_PER_LANE_BLOCK = V7X_LANES // SSM_GROUP
STATE_LANES = GROUPS_PER_LANE_BLOCK * SSM_STATE

TILE_IN_PROJECTION = (1024, 1024)
TILE_MEMORY_KV = (512, 512)
TILE_BRANCH_MERGE = (1024, 512)
TILE_OUT_PROJECTION = (1024, 512)
TILE_GLU = (512, 512)
ROWS_RMSNORM = 512
ROWS_MEMORY_ATTENTION = 512


def _tile(n, pref):
    t = min(n, pref)
    while n % t:
        t -= V7X_LANES
    assert t > 0
    return t


def _params(block_bytes, scratch_bytes=0, temp_bytes=0, n_axes=1):
    need = 2 * sum(block_bytes) + scratch_bytes + temp_bytes
    limit = min(max(need, 16 * 1024 * 1024), V7X_VMEM_BYTES - V7X_VMEM_RESERVE_BYTES)
    return pltpu.CompilerParams(dimension_semantics=("arbitrary",) * n_axes, vmem_limit_bytes=int(limit))


def _rmsnorm_kernel(x_ref, g_ref, o_ref):
    x = x_ref[...].astype(F32)
    ms = jnp.mean(x * x, axis=-1, keepdims=True)
    o_ref[...] = (x * lax.rsqrt(ms + EPS) * g_ref[...]).astype(o_ref.dtype)


def _rmsnorm(x, gain, rows):
    n, d = x.shape
    tm = _tile(n, rows)
    return pl.pallas_call(
        _rmsnorm_kernel,
        grid=(n // tm,),
        in_specs=[pl.BlockSpec((tm, d), lambda i: (i, 0)), pl.BlockSpec((1, d), lambda i: (0, 0))],
        out_specs=pl.BlockSpec((tm, d), lambda i: (i, 0)),
        out_shape=jax.ShapeDtypeStruct((n, d), BF16),
        compiler_params=_params([tm * d * 4, tm * d * 2], temp_bytes=2 * tm * d * 4),
        name="rmsnorm",
    )(x, gain.reshape(1, d).astype(F32))


def _mm_kernel(a_ref, b_ref, o_ref):
    o_ref[...] = jnp.dot(a_ref[...], b_ref[...].astype(BF16), preferred_element_type=F32).astype(o_ref.dtype)


def _mm_slabs_kernel(a_ref, b_ref, o_ref, slab_ref):
    res = jnp.dot(a_ref[...], b_ref[...], preferred_element_type=F32).astype(o_ref.dtype)
    o_ref[...] = res
    for k in range(slab_ref.shape[0]):
        slab_ref[k] = res[:, k * V7X_LANES:(k + 1) * V7X_LANES]


def _matmul(a, b, tm, tn, name, slab_cols=0):
    m, k = a.shape
    _, n = b.shape
    tm, tn = _tile(m, tm), _tile(n, tn)
    in_specs = [pl.BlockSpec((tm, k), lambda i, j: (i, 0)), pl.BlockSpec((k, tn), lambda i, j: (0, j))]
    b_cast_bytes = k * tn * 2 if b.dtype != BF16 else 0
    params = _params([tm * k * 2, k * tn * b.dtype.itemsize, 2 * tm * tn * 2],
                     temp_bytes=tm * tn * (4 + 4 + 2) + b_cast_bytes, n_axes=2)
    if not slab_cols:
        return pl.pallas_call(_mm_kernel, grid=(m // tm, n // tn), in_specs=in_specs,
                              out_specs=pl.BlockSpec((tm, tn), lambda i, j: (i, j)),
                              out_shape=jax.ShapeDtypeStruct((m, n), BF16), compiler_params=params, name=name)(a, b)
    assert slab_cols % tn == 0 and 0 < slab_cols < n
    nj, per = slab_cols // tn, tn // V7X_LANES
    out_spec = pl.BlockSpec((tm, tn), lambda i, j: (i, jnp.maximum(j - nj, 0)))
    slab_spec = pl.BlockSpec((per, tm, V7X_LANES), lambda i, j: (jnp.minimum(j, nj), i, 0))
    return pl.pallas_call(
        _mm_slabs_kernel,
        grid=(m // tm, n // tn), in_specs=in_specs, out_specs=[out_spec, slab_spec],
        out_shape=[jax.ShapeDtypeStruct((m, n - slab_cols), BF16),
                   jax.ShapeDtypeStruct(((nj + 1) * per, m, V7X_LANES), BF16)],
        compiler_params=params, name=name)(a, b)


MOBA_ADD_ROW = ATTN_HEAD_DIM + V7X_BF16_SUBLANES
MOBA_PIECES = 3


def _split_bf16(x):
    hi = x.astype(BF16).astype(F32)
    mid = (x - hi).astype(BF16).astype(F32)
    return hi, mid, x - hi - mid


def _moba_kernel(q_ref, q_next_ref, k_ref, v_ref, z_ref, qg_ref, kg_ref, slope_ref, o_ref,
                 kn_ref, vt_ref, kmean_ref, rhs_ref, s_ref, acc_ref, *, n_blocks, kgroup, qgroup):
    it = pl.program_id(2)
    bs, hd, sub = MOBA_BLOCK, ATTN_HEAD_DIM, V7X_BF16_SUBLANES
    kt, qt = kgroup * bs, qgroup * bs
    n_kgroups = n_blocks // kgroup
    strip = min(qt, MOBA_QUERY_STRIP)
    slope2 = slope_ref[0][:, :1] * LOG2E

    def prepare_keys(g):
        key_off = lax.broadcasted_iota(jnp.int32, (bs, hd), 0).astype(F32)
        lane = lax.broadcasted_iota(jnp.int32, (bs, hd), 1)
        add_lane = lane - (MOBA_ADD_ROW - hd)
        in_add = jnp.logical_and(add_lane >= 0, add_lane < MOBA_PIECES * n_blocks)
        for bi in range(kgroup):
            c = g * kgroup + bi
            rows = pl.ds(pl.multiple_of(c * bs, bs), bs)
            kb = k_ref[rows, :].astype(F32)
            ms = jnp.mean(kb * kb, axis=-1, keepdims=True)
            kn = kb * lax.rsqrt(ms + EPS) * kg_ref[...]
            k_aug = jnp.where(lane < MOBA_PIECES, key_off,
                              jnp.where(jnp.logical_and(in_add, add_lane % n_blocks == c), 1.0, 0.0))
            kn_ref[g, bi * bs:(bi + 1) * bs, :] = jnp.concatenate([kn.astype(BF16), k_aug.astype(BF16)], axis=1)
            kmean_ref[pl.ds(c, 1), :] = jnp.mean(kn, axis=0, keepdims=True)
            vt_ref[g, :hd, bi * bs:(bi + 1) * bs] = v_ref[rows, :].astype(F32).T.astype(BF16)
        vt_ref[g, hd:, :] = jnp.where(lax.broadcasted_iota(jnp.int32, (sub, kt), 0) == 0, 1.0, 0.0).astype(BF16)

    def prepare_queries(q_block_ref, tile, rhs_slot):
        q = q_block_ref[...].astype(F32)
        ms = jnp.mean(q * q, axis=-1, keepdims=True)
        qn_t = (q * lax.rsqrt(ms + EPS) * qg_ref[...]).T

        gate = jnp.dot(kmean_ref[...], qn_t, preferred_element_type=F32, precision=lax.Precision.HIGHEST)
        blk = lax.broadcasted_iota(jnp.int32, gate.shape, 0)
        own = tile * qgroup + lax.broadcasted_iota(jnp.int32, gate.shape, 1) // bs
        blk_f = blk.astype(F32)
        past = blk < own
        g = jnp.where(past, gate, NEG)
        sel = jnp.zeros(gate.shape, jnp.bool_)
        for _ in range(min(MOBA_TOP_K, n_blocks)):
            top = jnp.max(g, axis=0, keepdims=True)
            first = jnp.min(jnp.where(g == top, blk_f, float(n_blocks)), axis=0, keepdims=True)
            pick = blk_f == first
            sel = jnp.logical_or(sel, pick)
            g = jnp.where(pick, -jnp.inf, g)
        sel = jnp.logical_and(sel, past)
        add = jnp.where(sel, slope2 * (bs * (blk - own)).astype(F32), jnp.where(blk == own, 0.0, NEG))

        row = lax.broadcasted_iota(jnp.int32, (sub, qt), 0)
        s_hi, s_mid, s_lo = _split_bf16(slope2)
        slope_rows = jnp.where(row == 0, s_hi, jnp.where(row == 1, s_mid, jnp.where(row == 2, s_lo, 0.0)))
        pad = jnp.zeros((2 * hd - MOBA_ADD_ROW - MOBA_PIECES * n_blocks, qt), F32)
        rhs_ref[rhs_slot] = jnp.concatenate([qn_t * (hd ** -0.5 * LOG2E), slope_rows, *_split_bf16(add), pad],
                                            axis=0).astype(BF16)

    own_groups = qgroup // kgroup

    @pl.when(it == 0)
    def _():
        kmean_ref[...] = jnp.zeros(kmean_ref.shape, F32)
        for t in range(own_groups):
            prepare_keys(t)
        prepare_queries(q_ref, 0, 0)

    rhs_now = it % 2

    def scores_to(slot, gi, causal_group=None):
        s = jnp.dot(kn_ref[gi], rhs_ref[rhs_now], preferred_element_type=F32)
        top = None
        for bi in range(kgroup):
            part = s[bi * bs:(bi + 1) * bs]
            if causal_group is not None:
                d0 = (causal_group * kgroup + bi) * bs
                tri = lax.broadcasted_iota(jnp.int32, (bs, bs), 0) <= lax.broadcasted_iota(jnp.int32, (bs, bs), 1)
                pieces = [jnp.where(tri, part[:, d0:d0 + bs], NEG)]
                if d0 > 0:
                    pieces.insert(0, part[:, :d0])
                if d0 + bs < qt:
                    pieces.append(part[:, d0 + bs:])
                part = jnp.concatenate(pieces, axis=1)
            s_ref[slot, bi * bs:(bi + 1) * bs, :] = part
            top = part if top is None else jnp.maximum(top, part)
        return jnp.max(top, axis=0, keepdims=True)

    def accumulate(slot, gi, m, top, causal_group=None):
        m_new = jnp.maximum(m, top)
        alpha = jnp.exp2(m - m_new)
        for c0 in range(0, qt, strip):
            cols = slice(c0, c0 + strip)
            live = kgroup
            if causal_group is not None:
                live = min(max(c0 // bs - causal_group * kgroup + 1, 0), kgroup)
            if live == 0:
                continue
            p = jnp.exp2(s_ref[slot, :live * bs, cols] - m_new[:, cols]).astype(BF16)
            pv = jnp.dot(vt_ref[gi, :, :live * bs], p, preferred_element_type=F32)
            acc_ref[:, cols] = alpha[:, cols] * acc_ref[:, cols] + pv
        return m_new

    def pair(k, m):
        top0 = scores_to(0, 2 * k)
        top1 = scores_to(1, 2 * k + 1)
        return accumulate(1, 2 * k + 1, accumulate(0, 2 * k, m, top0), top1)

    acc_ref[...] = jnp.zeros(acc_ref.shape, F32)
    m = lax.fori_loop(0, it, pair, jnp.full((1, qt), 0.1 * NEG, F32))
    top0 = scores_to(0, 2 * it, causal_group=0)
    top1 = scores_to(1, 2 * it + 1, causal_group=1)
    accumulate(1, 2 * it + 1, accumulate(0, 2 * it, m, top0, causal_group=0), top1, causal_group=1)
    for t in range(own_groups):
        prepare_keys(jnp.minimum((it + 1) * own_groups + t, n_kgroups - 1))
    prepare_queries(q_next_ref, it + 1, 1 - rhs_now)
    acc = acc_ref[...]
    o = (acc[:hd] / acc[hd:hd + 1]).T
    o_ref[...] = (o * jax.nn.silu(z_ref[...].astype(F32))).astype(o_ref.dtype)


def _moba(slabs, q_gain, k_gain, bsz, seq, width):
    n_heads = width // ATTN_HEAD_DIM
    n_blocks = seq // MOBA_BLOCK
    hd, bs, sub = ATTN_HEAD_DIM, MOBA_BLOCK, V7X_BF16_SUBLANES
    kgroup, qgroup = MOBA_KEY_GROUP, MOBA_QUERY_GROUP
    assert qgroup == 2 * kgroup and n_blocks % qgroup == 0
    assert MOBA_ADD_ROW + MOBA_PIECES * n_blocks <= 2 * hd
    n_kgroups, n_qtiles, kt, qt = n_blocks // kgroup, n_blocks // qgroup, kgroup * bs, qgroup * bs
    slopes = jnp.asarray([[[2.0 ** (-8.0 * (h + 1) / n_heads)] * V7X_LANES] for h in range(n_heads)], F32)
    scratch = [
        pltpu.VMEM((n_kgroups, kt, 2 * hd), BF16),
        pltpu.VMEM((n_kgroups, hd + sub, kt), BF16),
        pltpu.VMEM((n_blocks, hd), F32),
        pltpu.VMEM((2, 2 * hd, qt), BF16),
        pltpu.VMEM((2, kt, qt), F32),
        pltpu.VMEM((hd + sub, qt), F32),
    ]
    scratch_bytes = (n_kgroups * (kt * 2 * hd + (hd + sub) * kt) * 2 + n_blocks * hd * 4
                     + 2 * 2 * hd * qt * 2 + 2 * kt * qt * 4 + (hd + sub) * qt * 4)
    return pl.pallas_call(
        functools.partial(_moba_kernel, n_blocks=n_blocks, kgroup=kgroup, qgroup=qgroup),
        grid=(bsz, n_heads, n_qtiles),
        in_specs=[
            pl.BlockSpec((None, qt, hd), lambda b, h, i: (h, b * n_qtiles + i, 0)),
            pl.BlockSpec((None, qt, hd), lambda b, h, i: (h, b * n_qtiles + jnp.minimum(i + 1, n_qtiles - 1), 0)),
            pl.BlockSpec((None, seq, hd), lambda b, h, i: (n_heads + h, b, 0)),
            pl.BlockSpec((None, seq, hd), lambda b, h, i: (2 * n_heads + h, b, 0)),
            pl.BlockSpec((None, qt, hd), lambda b, h, i: (3 * n_heads + h, b * n_qtiles + i, 0)),
            pl.BlockSpec((1, hd), lambda b, h, i: (0, 0)),
            pl.BlockSpec((1, hd), lambda b, h, i: (0, 0)),
            pl.BlockSpec((1, 1, V7X_LANES), lambda b, h, i: (h, 0, 0)),
        ],
        out_specs=pl.BlockSpec((qt, hd), lambda b, h, i: (b * n_qtiles + i, h)),
        out_shape=jax.ShapeDtypeStruct((bsz * seq, width), BF16),
        scratch_shapes=scratch,
        compiler_params=_params([2 * seq * hd * 2, 4 * qt * hd * 2], scratch_bytes=scratch_bytes,
                                temp_bytes=4 * kt * qt * 4, n_axes=3),
        name="moba_attention",
    )(slabs, slabs, slabs, slabs, slabs, q_gain.reshape(1, hd).astype(F32), k_gain.reshape(1, hd).astype(F32), slopes)


def _zoh(ar, ai, log_dt):
    dt = jnp.exp(log_dt)
    mag = jnp.exp(dt * ar)
    abr = mag * jnp.cos(dt * ai)
    abi = mag * jnp.sin(dt * ai)
    den = ar * ar + ai * ai
    nr = abr - 1.0
    return abr, abi, (nr * ar + abi * ai) / den, (abi * ar - nr * ai) / den


def _s5_kernel(u_ref, ar_ref, ai_ref, ldt_ref, bbr_ref, bbi_ref, ccr_ref, cci_ref, d_ref,
               ard_ref, aid_ref, ldtd_ref, bdr_ref, bdi_ref, cdr_ref, cdi_ref, y_ref,
               p_ref, qt_ref, t_ref, dk_ref, stage_ref, uc_ref, xs_ref, *, n_batch, rows_per_batch, row_tile):
    lb, gn, chunk = V7X_LANES, STATE_LANES, SSM_CHUNK
    kdim = chunk * lb
    col_tile = min(kdim, S5_OUT_COL_TILE)
    scan_rows = V7X_F32_SUBLANES
    assert rows_per_batch % scan_rows == 0
    abr, abi, f_re, f_im = _zoh(ar_ref[0], ai_ref[0], ldt_ref[0])

    abr_d, abi_d, f_re_d, f_im_d = _zoh(ard_ref[0], aid_ref[0], ldtd_ref[0])
    bd_re = f_re_d * bdr_ref[0] - f_im_d * bdi_ref[0]
    bd_im = f_re_d * bdi_ref[0] + f_im_d * bdr_ref[0]
    cd = jnp.concatenate([cdr_ref[0], -cdi_ref[0]], axis=1)
    same_group = (lax.broadcasted_iota(jnp.int32, (lb, lb), 0) // SSM_GROUP
                  == lax.broadcasted_iota(jnp.int32, (lb, lb), 1) // SSM_GROUP)
    pr_d = jnp.ones(abr_d.shape, F32)
    pi_d = jnp.zeros(abr_d.shape, F32)
    for tau in range(chunk):
        lag = jnp.concatenate([bd_re * pr_d - bd_im * pi_d, bd_re * pi_d + bd_im * pr_d], axis=1)
        blocks = lax.dot_general(lag, cd, (((1,), (1,)), ((), ())), preferred_element_type=F32,
                                 precision=lax.Precision.HIGHEST)
        dk_ref[tau] = jnp.where(same_group, blocks, 0.0).astype(BF16)
        pr_d, pi_d = pr_d * abr_d - pi_d * abi_d, pr_d * abi_d + pi_d * abr_d

    row_g = lax.broadcasted_iota(jnp.int32, (lb, gn), 0) // SSM_GROUP
    col_g = lax.broadcasted_iota(jnp.int32, (lb, gn), 1) // SSM_STATE
    same = row_g == col_g
    bb_re, bb_im = bbr_ref[0], bbi_ref[0]
    bbar_re = jnp.where(same, f_re * bb_re - f_im * bb_im, 0.0)
    bbar_im = jnp.where(same, f_re * bb_im + f_im * bb_re, 0.0)
    cc_re = jnp.where(same, ccr_ref[0], 0.0)
    cc_im = jnp.where(same, cci_ref[0], 0.0)

    pr = jnp.ones((1, gn), F32)
    pi = jnp.zeros((1, gn), F32)
    for tau in range(chunk + 1):
        if tau < chunk:
            s = chunk - 1 - tau
            p_ref[s * lb:(s + 1) * lb, :] = jnp.concatenate(
                [bbar_re * pr - bbar_im * pi, bbar_re * pi + bbar_im * pr], axis=1).astype(BF16)
        if tau >= 1:
            t = tau - 1
            qt_ref[t * lb:(t + 1) * lb, :] = jnp.concatenate(
                [cc_re * pr - cc_im * pi, -(cc_re * pi + cc_im * pr)], axis=1).astype(BF16)
        if tau < chunk:
            pr, pi = pr * abr - pi * abi, pr * abi + pi * abr
    al_re, al_im = pr, pi

    zero = jnp.zeros((lb, lb), BF16)
    for s in range(chunk):
        for t in range(chunk):
            t_ref[s * lb:(s + 1) * lb, t * lb:(t + 1) * lb] = dk_ref[t - s] if t >= s else zero

    n_rows = n_batch * rows_per_batch
    steps_per_batch = rows_per_batch * chunk
    for b in range(n_batch):
        stage_ref[...] = u_ref[0, b * steps_per_batch:(b + 1) * steps_per_batch, :].astype(F32)
        for s in range(chunk):
            uc_ref[b * rows_per_batch:(b + 1) * rows_per_batch, s * lb:(s + 1) * lb] = (
                stage_ref[pl.ds(s, rows_per_batch, stride=chunk), :].astype(BF16))

    for r0 in range(0, n_rows, row_tile):
        xs_ref[r0:r0 + row_tile, :] = jnp.dot(uc_ref[r0:r0 + row_tile, :], p_ref[...], preferred_element_type=F32)

    def step(c, carry):
        out = []
        for b in range(n_batch):
            xr, xi = carry[b]
            rows = pl.ds(pl.multiple_of(b * rows_per_batch + c * scan_rows, scan_rows), scan_rows)
            inc = xs_ref[rows, :]
            starts_re, starts_im = [], []
            for k in range(scan_rows):
                starts_re.append(xr)
                starts_im.append(xi)
                xr, xi = (al_re * xr - al_im * xi + inc[k:k + 1, :gn], al_re * xi + al_im * xr + inc[k:k + 1, gn:])
            xs_ref[rows, :gn] = jnp.concatenate(starts_re, axis=0)
            xs_ref[rows, gn:] = jnp.concatenate(starts_im, axis=0)
            out.append((xr, xi))
        return tuple(out)

    x0 = jnp.zeros((1, gn), F32)
    carry = tuple((x0, x0) for _ in range(n_batch))
    for c in range(rows_per_batch // scan_rows):
        carry = step(c, carry)

    for b in range(n_batch):
        for r0 in range(0, rows_per_batch, row_tile):
            rows = slice(b * rows_per_batch + r0, b * rows_per_batch + r0 + row_tile)
            x_start = xs_ref[rows, :].astype(BF16)
            for c0 in range(0, kdim, col_tile):
                cols = slice(c0, c0 + col_tile)
                y = jnp.dot(uc_ref[rows, :c0 + col_tile], t_ref[:c0 + col_tile, cols], preferred_element_type=F32)
                y = y + lax.dot_general(x_start, qt_ref[cols, :], (((1,), (1,)), ((), ())),
                                        preferred_element_type=F32)
                y = jax.nn.gelu(y + d_ref[0][:, cols] * uc_ref[rows, cols].astype(F32))
                for s in range(c0 // lb, (c0 + col_tile) // lb):
                    stage_ref[pl.ds(r0 * chunk + s, row_tile, stride=chunk), :] = y[:, s * lb - c0:(s + 1) * lb - c0]
        y_ref[0, b * steps_per_batch:(b + 1) * steps_per_batch, :] = stage_ref[...].astype(y_ref.dtype)


def _s5(slabs, slab0, nb, a_re, a_im, log_dt, b_re, b_im, c_re, c_im, d_skip, n_batch):
    _, n_steps, lb = slabs.shape
    gpb, gn, chunk = GROUPS_PER_LANE_BLOCK, STATE_LANES, SSM_CHUNK
    n_rows, kdim = n_steps // chunk, chunk * lb
    rows_per_batch = n_rows // n_batch
    row_tile = _tile(rows_per_batch, 256)

    def lane_row(v):
        return v.astype(F32).reshape(nb, 1, gn)

    ldt = jnp.repeat(log_dt.astype(F32), SSM_STATE).reshape(nb, 1, gn)
    bt_re = jnp.tile(b_re.astype(F32).reshape(nb, gpb, SSM_STATE, SSM_GROUP).transpose(0, 3, 1, 2).reshape(nb, SSM_GROUP, gn), (1, gpb, 1))
    bt_im = jnp.tile(b_im.astype(F32).reshape(nb, gpb, SSM_STATE, SSM_GROUP).transpose(0, 3, 1, 2).reshape(nb, SSM_GROUP, gn), (1, gpb, 1))
    ct_re = jnp.tile(c_re.astype(F32).reshape(nb, lb, SSM_STATE), (1, 1, gpb))
    ct_im = jnp.tile(c_im.astype(F32).reshape(nb, lb, SSM_STATE), (1, 1, gpb))
    d_row = jnp.tile(d_skip.astype(F32).reshape(nb, 1, lb), (1, 1, chunk))

    def per_channel_rows(v):
        return jnp.repeat(v.astype(F32).reshape(nb, gpb, SSM_STATE), SSM_GROUP, axis=1)

    ldt_d = per_channel_rows(jnp.broadcast_to(log_dt[:, None], a_re.shape))
    bd_re = b_re.astype(F32).reshape(nb, gpb, SSM_STATE, SSM_GROUP).transpose(0, 1, 3, 2).reshape(nb, lb, SSM_STATE)
    bd_im = b_im.astype(F32).reshape(nb, gpb, SSM_STATE, SSM_GROUP).transpose(0, 1, 3, 2).reshape(nb, lb, SSM_STATE)
    cd_re = c_re.astype(F32).reshape(nb, lb, SSM_STATE)
    cd_im = c_im.astype(F32).reshape(nb, lb, SSM_STATE)
    dia = pl.BlockSpec((1, lb, SSM_STATE), lambda j: (j, 0, 0))

    vec = pl.BlockSpec((1, 1, gn), lambda j: (j, 0, 0))
    mat = pl.BlockSpec((1, lb, gn), lambda j: (j, 0, 0))
    blk = pl.BlockSpec((1, n_steps, lb), lambda j: (j, 0, 0))
    blk_in = pl.BlockSpec((1, n_steps, lb), lambda j: (slab0 + j, 0, 0))
    scratch = [
        pltpu.VMEM((kdim, 2 * gn), BF16),
        pltpu.VMEM((kdim, 2 * gn), BF16),
        pltpu.VMEM((kdim, kdim), BF16),
        pltpu.VMEM((chunk, lb, lb), BF16),
        pltpu.VMEM((n_steps // n_batch, lb), F32),
        pltpu.VMEM((n_rows, kdim), BF16),
        pltpu.VMEM((n_rows, 2 * gn), F32),
    ]
    scratch_bytes = (2 * kdim * 2 * gn * 2 + kdim * kdim * 2 + chunk * lb * lb * 2 + n_steps // n_batch * lb * 4
                     + n_rows * kdim * 2 + n_rows * 2 * gn * 4)
    return pl.pallas_call(
        functools.partial(_s5_kernel, n_batch=n_batch, rows_per_batch=rows_per_batch, row_tile=row_tile),
        grid=(nb,),
        in_specs=[blk_in, vec, vec, vec, mat, mat, mat, mat, pl.BlockSpec((1, 1, kdim), lambda j: (j, 0, 0))] + [dia] * 7,
        out_specs=blk,
        out_shape=jax.ShapeDtypeStruct((nb, n_steps, lb), BF16),
        scratch_shapes=scratch,
        compiler_params=_params([n_steps * lb * 2, n_steps * lb * 2, 4 * lb * gn * 4], scratch_bytes=scratch_bytes,
                                temp_bytes=4 * row_tile * kdim * 4),
        name="s5_scan",
    )(slabs, lane_row(a_re), lane_row(a_im), ldt, bt_re, bt_im, ct_re, ct_im, d_row,
      per_channel_rows(a_re), per_channel_rows(a_im), ldt_d, bd_re, bd_im, cd_re, cd_im)


def _glu_kernel(y_ref, w_ref, b_ref, z_ref, o_ref, *, col_tile):
    lb = y_ref.shape[2]
    y = jnp.concatenate([y_ref[k] for k in range(y_ref.shape[0])], axis=1)
    for c0 in range(0, o_ref.shape[1], col_tile):
        cols = slice(c0, c0 + col_tile)
        a = jnp.dot(y, w_ref[:, cols], preferred_element_type=F32) + b_ref[:, cols]
        z = z_ref[:, cols].astype(F32)
        ycol = jnp.concatenate([y_ref[k] for k in range(c0 // lb, (c0 + col_tile) // lb)], axis=1).astype(F32)
        o_ref[:, cols] = (ycol * z / ((1.0 + jnp.exp(-a)) * (1.0 + jnp.exp(-z)))).astype(o_ref.dtype)


def _glu(y_blocks, w_glu, b_glu, proj, z_col0, tm, col_tile):
    nb, n, lb = y_blocks.shape
    width = nb * lb
    tm, col_tile = _tile(n, tm), _tile(width, col_tile)
    assert z_col0 % width == 0
    return pl.pallas_call(
        functools.partial(_glu_kernel, col_tile=col_tile),
        grid=(n // tm,),
        in_specs=[
            pl.BlockSpec((nb, tm, lb), lambda i: (0, i, 0)),
            pl.BlockSpec((width, width), lambda i: (0, 0)),
            pl.BlockSpec((1, width), lambda i: (0, 0)),
            pl.BlockSpec((tm, width), lambda i: (i, z_col0 // width)),
        ],
        out_specs=pl.BlockSpec((tm, width), lambda i: (i, 0)),
        out_shape=jax.ShapeDtypeStruct((n, width), BF16),
        compiler_params=_params([tm * width * 2, width * width * 2, tm * width * 2, tm * width * 2],
                                temp_bytes=tm * width * 2 + 4 * tm * col_tile * 4),
        name="s5_glu",
    )(y_blocks, w_glu, b_glu.reshape(1, width).astype(F32), proj)


def _memattn_kernel(q_ref, z_ref, kv_ref, qg_ref, kg_ref, o_ref, kn_ref, *, width):
    dm = width // MEM_HEADS

    @pl.when(pl.program_id(1) == 0)
    def _():
        for hd in range(MEM_HEADS):
            cols = slice(hd * dm, (hd + 1) * dm)
            k = kv_ref[:, cols].astype(F32)
            kn = k * lax.rsqrt(jnp.mean(k * k, axis=-1, keepdims=True) + EPS) * kg_ref[...]
            kn_ref[:, cols] = kn.astype(BF16)

    for hd in range(MEM_HEADS):
        cols = slice(hd * dm, (hd + 1) * dm)
        q = q_ref[:, cols].astype(F32)
        qn = q * lax.rsqrt(jnp.mean(q * q, axis=-1, keepdims=True) + EPS) * qg_ref[...]
        v = kv_ref[:, width + hd * dm:width + (hd + 1) * dm]
        s = lax.dot_general(qn.astype(BF16), kn_ref[:, cols], (((1,), (1,)), ((), ())),
                            preferred_element_type=F32) * (dm ** -0.5)
        p = jnp.exp(s - jnp.max(s, axis=-1, keepdims=True))
        l = jnp.sum(p, axis=-1, keepdims=True)
        o = jnp.dot(p.astype(BF16), v, preferred_element_type=F32) / l
        o_ref[:, cols] = (o * jax.nn.silu(z_ref[:, cols].astype(F32))).astype(o_ref.dtype)


def _memattn(proj, q_col0, z_col0, kv, q_gain, k_gain, bsz, seq, width, tq):
    assert q_col0 % width == 0 and z_col0 % width == 0
    dm = width // MEM_HEADS
    n_mem = kv.shape[0] // bsz
    tq = _tile(seq, tq)
    nq = seq // tq
    return pl.pallas_call(
        functools.partial(_memattn_kernel, width=width),
        grid=(bsz, nq),
        in_specs=[
            pl.BlockSpec((tq, width), lambda b, i: (b * nq + i, q_col0 // width)),
            pl.BlockSpec((tq, width), lambda b, i: (b * nq + i, z_col0 // width)),
            pl.BlockSpec((n_mem, 2 * width), lambda b, i: (b, 0)),
            pl.BlockSpec((1, dm), lambda b, i: (0, 0)),
            pl.BlockSpec((1, dm), lambda b, i: (0, 0)),
        ],
        out_specs=pl.BlockSpec((tq, width), lambda b, i: (b * nq + i, 0)),
        out_shape=jax.ShapeDtypeStruct((bsz * seq, width), BF16),
        scratch_shapes=[pltpu.VMEM((n_mem, width), BF16)],
        compiler_params=_params([3 * tq * width * 2, n_mem * 2 * width * 2], scratch_bytes=n_mem * width * 2,
                                temp_bytes=8 * tq * dm * 4, n_axes=2),
        name="memory_attention",
    )(proj, proj, kv, q_gain.reshape(1, dm).astype(F32), k_gain.reshape(1, dm).astype(F32))


def _merge_kernel(ya_ref, ys_ref, yc_ref, wa_ref, ws_ref, wc_ref, ga_ref, gs_ref, gc_ref, o_ref):
    def term(y_ref, w_ref, g_ref):
        return jax.nn.sigmoid(g_ref[...].astype(F32)) * jnp.dot(y_ref[...], w_ref[...], preferred_element_type=F32)

    o_ref[...] = (term(ya_ref, wa_ref, ga_ref) + term(ys_ref, ws_ref, gs_ref)
                  + term(yc_ref, wc_ref, gc_ref)).astype(o_ref.dtype)


def _merge(y_a, y_s, y_c, w_a, w_s, w_c, proj, g_col0, tm, tn):
    n, width = y_a.shape
    d = w_a.shape[1]
    tm, tn = _tile(n, tm), _tile(d, tn)
    y_spec = pl.BlockSpec((tm, width), lambda i, j: (i, 0))
    w_spec = pl.BlockSpec((width, tn), lambda i, j: (0, j))

    def g_spec(branch):
        assert (g_col0 + branch * d) % tn == 0
        return pl.BlockSpec((tm, tn), lambda i, j: (i, (g_col0 + branch * d) // tn + j))

    return pl.pallas_call(
        _merge_kernel,
        grid=(n // tm, d // tn),
        in_specs=[y_spec] * 3 + [w_spec] * 3 + [g_spec(0), g_spec(1), g_spec(2)],
        out_specs=pl.BlockSpec((tm, tn), lambda i, j: (i, j)),
        out_shape=jax.ShapeDtypeStruct((n, d), BF16),
        compiler_params=_params([3 * tm * width * 2, 3 * width * tn * 2, 4 * tm * tn * 2], temp_bytes=4 * tm * tn * 4, n_axes=2),
        name="branch_merge",
    )(y_a, y_s, y_c, w_a, w_s, w_c, proj, proj, proj)


def _outproj_kernel(m_ref, w_ref, x_ref, o_ref):
    o_ref[...] = x_ref[...] + jnp.dot(m_ref[...], w_ref[...], preferred_element_type=F32)


def _outproj(merged, w_out, x, tm, tn):
    n, d = merged.shape
    tm, tn = _tile(n, tm), _tile(d, tn)
    return pl.pallas_call(
        _outproj_kernel,
        grid=(n // tm, d // tn),
        in_specs=[
            pl.BlockSpec((tm, d), lambda i, j: (i, 0)),
            pl.BlockSpec((d, tn), lambda i, j: (0, j)),
            pl.BlockSpec((tm, tn), lambda i, j: (i, j)),
        ],
        out_specs=pl.BlockSpec((tm, tn), lambda i, j: (i, j)),
        out_shape=jax.ShapeDtypeStruct((n, d), F32),
        compiler_params=_params([tm * d * 2, d * tn * 2, 2 * tm * tn * 4], temp_bytes=tm * tn * 4, n_axes=2),
        name="out_projection",
    )(merged, w_out, x)


def kernel(x, mem, w_in, g_norm, g_mem, w_mem_kv, q_gain_a, k_gain_a, q_gain_c, k_gain_c, ssm_a_re, ssm_a_im, ssm_log_dt, ssm_b_re, ssm_b_im, ssm_c_re, ssm_c_im, ssm_d, w_glu, b_glu, w_br_a, w_br_s, w_br_c, w_out):
    bsz, seq, d_model = x.shape
    width = w_glu.shape[0]
    n_tok = bsz * seq
    n_mem = mem.shape[1]
    assert seq % MOBA_BLOCK == 0 and seq % SSM_CHUNK == 0 and width % V7X_LANES == 0
    assert w_in.shape == (d_model, 8 * width + 3 * d_model)

    x2 = x.reshape(n_tok, d_model)
    h = _rmsnorm(x2, g_norm, rows=ROWS_RMSNORM)
    proj, slabs = _matmul(h, w_in.astype(BF16), TILE_IN_PROJECTION[0], _tile(width, TILE_IN_PROJECTION[1]),
                          "in_projection", slab_cols=5 * width)
    nb = width // V7X_LANES

    m = _rmsnorm(mem.reshape(bsz * n_mem, d_model), g_mem, rows=ROWS_RMSNORM)
    kv = _matmul(m, w_mem_kv, *TILE_MEMORY_KV, "memory_kv_projection")

    y_a = _moba(slabs, q_gain_a, k_gain_a, bsz, seq, width)

    y_g = _s5(slabs, 4 * nb, nb, ssm_a_re, ssm_a_im, ssm_log_dt, ssm_b_re, ssm_b_im, ssm_c_re, ssm_c_im, ssm_d, bsz)
    y_s = _glu(y_g, w_glu.astype(BF16), b_glu, proj, 0, *TILE_GLU)

    y_c = _memattn(proj, width, 2 * width, kv, q_gain_c, k_gain_c, bsz, seq, width, ROWS_MEMORY_ATTENTION)

    merged = _merge(y_a, y_s, y_c, w_br_a.astype(BF16), w_br_s.astype(BF16), w_br_c.astype(BF16), proj,
                    3 * width, *TILE_BRANCH_MERGE)
    out = _outproj(merged, w_out.astype(BF16), x2, *TILE_OUT_PROJECTION)
    return out.reshape(bsz, seq, d_model)
```

```python
import functools

import jax
import jax.numpy as jnp
from jax import lax
from jax.experimental import pallas as pl
from jax.experimental.pallas import tpu as pltpu

F32 = jnp.float32
BF16 = jnp.bfloat16

V7X_LANES = 128
V7X_F32_SUBLANES = 8
V7X_BF16_SUBLANES = 16
V7X_VMEM_BYTES = 64 * 1024 * 1024
V7X_VMEM_RESERVE_BYTES = 6 * 1024 * 1024

ATTN_HEAD_DIM = 128
MOBA_BLOCK = 256
MOBA_TOP_K = 3
SSM_GROUP = 16
SSM_STATE = 64
MEM_HEADS = 4
EPS = 1e-6
NEG = -1e30
LOG2E = 1.4426950408889634

MOBA_KEY_GROUP = 2
MOBA_QUERY_GROUP = 4
MOBA_QUERY_STRIP = 256
SSM_CHUNK = 16
S5_OUT_COL_TILE = 512
GROUPS_PER_LANE_BLOCK = V7X_LANES // SSM_GROUP
STATE_LANES = GROUPS_PER_LANE_BLOCK * SSM_STATE

TILE_IN_PROJECTION = (1024, 1024)
TILE_MEMORY_KV = (512, 512)
TILE_BRANCH_MERGE = (1024, 512)
TILE_OUT_PROJECTION = (1024, 512)
TILE_GLU = (512, 512)
ROWS_RMSNORM = 512
ROWS_MEMORY_ATTENTION = 512


def _tile(n, pref):
    t = min(n, pref)
    while n % t:
        t -= V7X_LANES
    assert t > 0
    return t


def _params(block_bytes, scratch_bytes=0, temp_bytes=0, n_axes=1):
    need = 2 * sum(block_bytes) + scratch_bytes + temp_bytes
    limit = min(max(need, 16 * 1024 * 1024), V7X_VMEM_BYTES - V7X_VMEM_RESERVE_BYTES)
    return pltpu.CompilerParams(dimension_semantics=("arbitrary",) * n_axes, vmem_limit_bytes=int(limit))


def _rmsnorm_kernel(x_ref, g_ref, o_ref):
    x = x_ref[...].astype(F32)
    ms = jnp.mean(x * x, axis=-1, keepdims=True)
    o_ref[...] = (x * lax.rsqrt(ms + EPS) * g_ref[...]).astype(o_ref.dtype)


def _rmsnorm(x, gain, rows):
    n, d = x.shape
    tm = _tile(n, rows)
    return pl.pallas_call(
        _rmsnorm_kernel,
        grid=(n // tm,),
        in_specs=[pl.BlockSpec((tm, d), lambda i: (i, 0)), pl.BlockSpec((1, d), lambda i: (0, 0))],
        out_specs=pl.BlockSpec((tm, d), lambda i: (i, 0)),
        out_shape=jax.ShapeDtypeStruct((n, d), BF16),
        compiler_params=_params([tm * d * 4, tm * d * 2], temp_bytes=2 * tm * d * 4),
        name="rmsnorm",
    )(x, gain.reshape(1, d).astype(F32))


def _mm_kernel(a_ref, b_ref, o_ref):
    o_ref[...] = jnp.dot(a_ref[...], b_ref[...].astype(BF16), preferred_element_type=F32).astype(o_ref.dtype)


def _mm_slabs_kernel(a_ref, b_ref, o_ref, slab_ref):
    res = jnp.dot(a_ref[...], b_ref[...], preferred_element_type=F32).astype(o_ref.dtype)
    o_ref[...] = res
    for k in range(slab_ref.shape[0]):
        slab_ref[k] = res[:, k * V7X_LANES:(k + 1) * V7X_LANES]


def _matmul(a, b, tm, tn, name, slab_cols=0):
    m, k = a.shape
    _, n = b.shape
    tm, tn = _tile(m, tm), _tile(n, tn)
    in_specs = [pl.BlockSpec((tm, k), lambda i, j: (i, 0)), pl.BlockSpec((k, tn), lambda i, j: (0, j))]
    b_cast_bytes = k * tn * 2 if b.dtype != BF16 else 0
    params = _params([tm * k * 2, k * tn * b.dtype.itemsize, 2 * tm * tn * 2],
                     temp_bytes=tm * tn * (4 + 4 + 2) + b_cast_bytes, n_axes=2)
    if not slab_cols:
        return pl.pallas_call(_mm_kernel, grid=(m // tm, n // tn), in_specs=in_specs,
                              out_specs=pl.BlockSpec((tm, tn), lambda i, j: (i, j)),
                              out_shape=jax.ShapeDtypeStruct((m, n), BF16), compiler_params=params, name=name)(a, b)
    assert slab_cols % tn == 0 and 0 < slab_cols < n
    nj, per = slab_cols // tn, tn // V7X_LANES
    out_spec = pl.BlockSpec((tm, tn), lambda i, j: (i, jnp.maximum(j - nj, 0)))
    slab_spec = pl.BlockSpec((per, tm, V7X_LANES), lambda i, j: (jnp.minimum(j, nj), i, 0))
    return pl.pallas_call(
        _mm_slabs_kernel,
        grid=(m // tm, n // tn), in_specs=in_specs, out_specs=[out_spec, slab_spec],
        out_shape=[jax.ShapeDtypeStruct((m, n - slab_cols), BF16),
                   jax.ShapeDtypeStruct(((nj + 1) * per, m, V7X_LANES), BF16)],
        compiler_params=params, name=name)(a, b)


MOBA_ADD_ROW = ATTN_HEAD_DIM + V7X_BF16_SUBLANES
MOBA_PIECES = 3


def _split_bf16(x):
    hi = x.astype(BF16).astype(F32)
    mid = (x - hi).astype(BF16).astype(F32)
    return hi, mid, x - hi - mid


def _moba_kernel(q_ref, q_next_ref, k_ref, v_ref, z_ref, qg_ref, kg_ref, slope_ref, o_ref,
                 kn_ref, vt_ref, kmean_ref, rhs_ref, s_ref, acc_ref, *, n_blocks, kgroup, qgroup):
    it = pl.program_id(2)
    bs, hd, sub = MOBA_BLOCK, ATTN_HEAD_DIM, V7X_BF16_SUBLANES
    kt, qt = kgroup * bs, qgroup * bs
    n_kgroups = n_blocks // kgroup
    strip = min(qt, MOBA_QUERY_STRIP)
    slope2 = slope_ref[0][:, :1] * LOG2E

    def prepare_keys(g):
        key_off = lax.broadcasted_iota(jnp.int32, (bs, hd), 0).astype(F32)
        lane = lax.broadcasted_iota(jnp.int32, (bs, hd), 1)
        add_lane = lane - (MOBA_ADD_ROW - hd)
        in_add = jnp.logical_and(add_lane >= 0, add_lane < MOBA_PIECES * n_blocks)
        for bi in range(kgroup):
            c = g * kgroup + bi
            rows = pl.ds(pl.multiple_of(c * bs, bs), bs)
            kb = k_ref[rows, :].astype(F32)
            ms = jnp.mean(kb * kb, axis=-1, keepdims=True)
            kn = kb * lax.rsqrt(ms + EPS) * kg_ref[...]
            k_aug = jnp.where(lane < MOBA_PIECES, key_off,
                              jnp.where(jnp.logical_and(in_add, add_lane % n_blocks == c), 1.0, 0.0))
            kn_ref[g, bi * bs:(bi + 1) * bs, :] = jnp.concatenate([kn.astype(BF16), k_aug.astype(BF16)], axis=1)
            kmean_ref[pl.ds(c, 1), :] = jnp.mean(kn, axis=0, keepdims=True)
            vt_ref[g, :hd, bi * bs:(bi + 1) * bs] = v_ref[rows, :].astype(F32).T.astype(BF16)
        vt_ref[g, hd:, :] = jnp.where(lax.broadcasted_iota(jnp.int32, (sub, kt), 0) == 0, 1.0, 0.0).astype(BF16)

    def prepare_queries(q_block_ref, tile, rhs_slot):
        q = q_block_ref[...].astype(F32)
        ms = jnp.mean(q * q, axis=-1, keepdims=True)
        qn_t = (q * lax.rsqrt(ms + EPS) * qg_ref[...]).T

        gate = jnp.dot(kmean_ref[...], qn_t, preferred_element_type=F32, precision=lax.Precision.HIGHEST)
        blk = lax.broadcasted_iota(jnp.int32, gate.shape, 0)
        own = tile * qgroup + lax.broadcasted_iota(jnp.int32, gate.shape, 1) // bs
        blk_f = blk.astype(F32)
        past = blk < own
        g = jnp.where(past, gate, NEG)
        sel = jnp.zeros(gate.shape, jnp.bool_)
        for _ in range(min(MOBA_TOP_K, n_blocks)):
            top = jnp.max(g, axis=0, keepdims=True)
            first = jnp.min(jnp.where(g == top, blk_f, float(n_blocks)), axis=0, keepdims=True)
            pick = blk_f == first
            sel = jnp.logical_or(sel, pick)
            g = jnp.where(pick, -jnp.inf, g)
        sel = jnp.logical_and(sel, past)
        add = jnp.where(sel, slope2 * (bs * (blk - own)).astype(F32), jnp.where(blk == own, 0.0, NEG))

        row = lax.broadcasted_iota(jnp.int32, (sub, qt), 0)
        s_hi, s_mid, s_lo = _split_bf16(slope2)
        slope_rows = jnp.where(row == 0, s_hi, jnp.where(row == 1, s_mid, jnp.where(row == 2, s_lo, 0.0)))
        pad = jnp.zeros((2 * hd - MOBA_ADD_ROW - MOBA_PIECES * n_blocks, qt), F32)
        rhs_ref[rhs_slot] = jnp.concatenate([qn_t * (hd ** -0.5 * LOG2E), slope_rows, *_split_bf16(add), pad],
                                            axis=0).astype(BF16)

    own_groups = qgroup // kgroup

    @pl.when(it == 0)
    def _():
        kmean_ref[...] = jnp.zeros(kmean_ref.shape, F32)
        for t in range(own_groups):
            prepare_keys(t)
        prepare_queries(q_ref, 0, 0)

    def scores_to(slot, gi, rhs_slot, causal_group=None):
        s = jnp.dot(kn_ref[gi], rhs_ref[rhs_slot], preferred_element_type=F32)
        top = None
        for bi in range(kgroup):
            part = s[bi * bs:(bi + 1) * bs]
            if causal_group is not None:
                d0 = (causal_group * kgroup + bi) * bs
                tri = lax.broadcasted_iota(jnp.int32, (bs, bs), 0) <= lax.broadcasted_iota(jnp.int32, (bs, bs), 1)
                pieces = [jnp.where(tri, part[:, d0:d0 + bs], NEG)]
                if d0 > 0:
                    pieces.insert(0, part[:, :d0])
                if d0 + bs < qt:
                    pieces.append(part[:, d0 + bs:])
                part = jnp.concatenate(pieces, axis=1)
            s_ref[slot, bi * bs:(bi + 1) * bs, :] = part
            top = part if top is None else jnp.maximum(top, part)
        return jnp.max(top, axis=0, keepdims=True)

    def accumulate(slot, gi, m, top, causal_group=None):
        m_new = jnp.maximum(m, top)
        alpha = jnp.exp2(m - m_new)
        for c0 in range(0, qt, strip):
            cols = slice(c0, c0 + strip)
            live = kgroup
            if causal_group is not None:
                live = min(max(c0 // bs - causal_group * kgroup + 1, 0), kgroup)
            if live == 0:
                continue
            p = jnp.exp2(s_ref[slot, :live * bs, cols] - m_new[:, cols]).astype(BF16)
            pv = jnp.dot(vt_ref[gi, :, :live * bs], p, preferred_element_type=F32)
            acc_ref[:, cols] = alpha[:, cols] * acc_ref[:, cols] + pv
        return m_new

    def tile_body(tile):
        rhs_slot = tile % 2
        acc_ref[...] = jnp.zeros(acc_ref.shape, F32)
        m = jnp.full((1, qt), 0.1 * NEG, F32)
        for k in range(tile):
            top0 = scores_to(0, 2 * k, rhs_slot)
            top1 = scores_to(1, 2 * k + 1, rhs_slot)
            m = accumulate(1, 2 * k + 1, accumulate(0, 2 * k, m, top0), top1)
        top0 = scores_to(0, 2 * tile, rhs_slot, causal_group=0)
        top1 = scores_to(1, 2 * tile + 1, rhs_slot, causal_group=1)
        accumulate(1, 2 * tile + 1, accumulate(0, 2 * tile, m, top0, causal_group=0), top1, causal_group=1)
        for t in range(own_groups):
            prepare_keys(min((tile + 1) * own_groups + t, n_kgroups - 1))
        prepare_queries(q_next_ref, tile + 1, 1 - rhs_slot)
        acc = acc_ref[...]
        o = (acc[:hd] / acc[hd:hd + 1]).T
        o_ref[...] = (o * jax.nn.silu(z_ref[...].astype(F32))).astype(o_ref.dtype)

    for tile in range(n_blocks // qgroup):
        pl.when(it == tile)(functools.partial(tile_body, tile))


def _moba(slabs, q_gain, k_gain, bsz, seq, width):
    n_heads = width // ATTN_HEAD_DIM
    n_blocks = seq // MOBA_BLOCK
    hd, bs, sub = ATTN_HEAD_DIM, MOBA_BLOCK, V7X_BF16_SUBLANES
    kgroup, qgroup = MOBA_KEY_GROUP, MOBA_QUERY_GROUP
    assert qgroup == 2 * kgroup and n_blocks % qgroup == 0
    assert MOBA_ADD_ROW + MOBA_PIECES * n_blocks <= 2 * hd
    n_kgroups, n_qtiles, kt, qt = n_blocks // kgroup, n_blocks // qgroup, kgroup * bs, qgroup * bs
    slopes = jnp.asarray([[[2.0 ** (-8.0 * (h + 1) / n_heads)] * V7X_LANES] for h in range(n_heads)], F32)
    scratch = [
        pltpu.VMEM((n_kgroups, kt, 2 * hd), BF16),
        pltpu.VMEM((n_kgroups, hd + sub, kt), BF16),
        pltpu.VMEM((n_blocks, hd), F32),
        pltpu.VMEM((2, 2 * hd, qt), BF16),
        pltpu.VMEM((2, kt, qt), F32),
        pltpu.VMEM((hd + sub, qt), F32),
    ]
    scratch_bytes = (n_kgroups * (kt * 2 * hd + (hd + sub) * kt) * 2 + n_blocks * hd * 4
                     + 2 * 2 * hd * qt * 2 + 2 * kt * qt * 4 + (hd + sub) * qt * 4)
    return pl.pallas_call(
        functools.partial(_moba_kernel, n_blocks=n_blocks, kgroup=kgroup, qgroup=qgroup),
        grid=(bsz, n_heads, n_qtiles),
        in_specs=[
            pl.BlockSpec((None, qt, hd), lambda b, h, i: (h, b * n_qtiles + i, 0)),
            pl.BlockSpec((None, qt, hd), lambda b, h, i: (h, b * n_qtiles + jnp.minimum(i + 1, n_qtiles - 1), 0)),
            pl.BlockSpec((None, seq, hd), lambda b, h, i: (n_heads + h, b, 0)),
            pl.BlockSpec((None, seq, hd), lambda b, h, i: (2 * n_heads + h, b, 0)),
            pl.BlockSpec((None, qt, hd), lambda b, h, i: (3 * n_heads + h, b * n_qtiles + i, 0)),
            pl.BlockSpec((1, hd), lambda b, h, i: (0, 0)),
            pl.BlockSpec((1, hd), lambda b, h, i: (0, 0)),
            pl.BlockSpec((1, 1, V7X_LANES), lambda b, h, i: (h, 0, 0)),
        ],
        out_specs=pl.BlockSpec((qt, hd), lambda b, h, i: (b * n_qtiles + i, h)),
        out_shape=jax.ShapeDtypeStruct((bsz * seq, width), BF16),
        scratch_shapes=scratch,
        compiler_params=_params([2 * seq * hd * 2, 4 * qt * hd * 2], scratch_bytes=scratch_bytes,
                                temp_bytes=4 * kt * qt * 4, n_axes=3),
        name="moba_attention",
    )(slabs, slabs, slabs, slabs, slabs, q_gain.reshape(1, hd).astype(F32), k_gain.reshape(1, hd).astype(F32), slopes)


def _zoh(ar, ai, log_dt):
    dt = jnp.exp(log_dt)
    mag = jnp.exp(dt * ar)
    abr = mag * jnp.cos(dt * ai)
    abi = mag * jnp.sin(dt * ai)
    den = ar * ar + ai * ai
    nr = abr - 1.0
    return abr, abi, (nr * ar + abi * ai) / den, (abi * ar - nr * ai) / den


def _s5_kernel(u_ref, ar_ref, ai_ref, ldt_ref, bbr_ref, bbi_ref, ccr_ref, cci_ref, d_ref,
               ard_ref, aid_ref, ldtd_ref, bdr_ref, bdi_ref, cdr_ref, cdi_ref, y_ref,
               p_ref, qt_ref, t_ref, dk_ref, stage_ref, uc_ref, xs_ref, *, n_batch, rows_per_batch, row_tile):
    lb, gn, chunk = V7X_LANES, STATE_LANES, SSM_CHUNK
    kdim = chunk * lb
    col_tile = min(kdim, S5_OUT_COL_TILE)
    scan_rows = V7X_F32_SUBLANES
    assert rows_per_batch % scan_rows == 0
    abr, abi, f_re, f_im = _zoh(ar_ref[0], ai_ref[0], ldt_ref[0])

    abr_d, abi_d, f_re_d, f_im_d = _zoh(ard_ref[0], aid_ref[0], ldtd_ref[0])
    bd_re = f_re_d * bdr_ref[0] - f_im_d * bdi_ref[0]
    bd_im = f_re_d * bdi_ref[0] + f_im_d * bdr_ref[0]
    cd = jnp.concatenate([cdr_ref[0], -cdi_ref[0]], axis=1)
    same_group = (lax.broadcasted_iota(jnp.int32, (lb, lb), 0) // SSM_GROUP
                  == lax.broadcasted_iota(jnp.int32, (lb, lb), 1) // SSM_GROUP)
    pr_d = jnp.ones(abr_d.shape, F32)
    pi_d = jnp.zeros(abr_d.shape, F32)
    for tau in range(chunk):
        lag = jnp.concatenate([bd_re * pr_d - bd_im * pi_d, bd_re * pi_d + bd_im * pr_d], axis=1)
        blocks = lax.dot_general(lag, cd, (((1,), (1,)), ((), ())), preferred_element_type=F32,
                                 precision=lax.Precision.HIGHEST)
        dk_ref[tau] = jnp.where(same_group, blocks, 0.0).astype(BF16)
        pr_d, pi_d = pr_d * abr_d - pi_d * abi_d, pr_d * abi_d + pi_d * abr_d

    row_g = lax.broadcasted_iota(jnp.int32, (lb, gn), 0) // SSM_GROUP
    col_g = lax.broadcasted_iota(jnp.int32, (lb, gn), 1) // SSM_STATE
    same = row_g == col_g
    bb_re, bb_im = bbr_ref[0], bbi_ref[0]
    bbar_re = jnp.where(same, f_re * bb_re - f_im * bb_im, 0.0)
    bbar_im = jnp.where(same, f_re * bb_im + f_im * bb_re, 0.0)
    cc_re = jnp.where(same, ccr_ref[0], 0.0)
    cc_im = jnp.where(same, cci_ref[0], 0.0)

    pr = jnp.ones((1, gn), F32)
    pi = jnp.zeros((1, gn), F32)
    for tau in range(chunk + 1):
        if tau < chunk:
            s = chunk - 1 - tau
            p_ref[s * lb:(s + 1) * lb, :] = jnp.concatenate(
                [bbar_re * pr - bbar_im * pi, bbar_re * pi + bbar_im * pr], axis=1).astype(BF16)
        if tau >= 1:
            t = tau - 1
            qt_ref[t * lb:(t + 1) * lb, :] = jnp.concatenate(
                [cc_re * pr - cc_im * pi, -(cc_re * pi + cc_im * pr)], axis=1).astype(BF16)
        if tau < chunk:
            pr, pi = pr * abr - pi * abi, pr * abi + pi * abr
    al_re, al_im = pr, pi

    zero = jnp.zeros((lb, lb), BF16)
    for s in range(chunk):
        for t in range(chunk):
            t_ref[s * lb:(s + 1) * lb, t * lb:(t + 1) * lb] = dk_ref[t - s] if t >= s else zero

    n_rows = n_batch * rows_per_batch
    steps_per_batch = rows_per_batch * chunk
    for b in range(n_batch):
        stage_ref[...] = u_ref[0, b * steps_per_batch:(b + 1) * steps_per_batch, :].astype(F32)
        for s in range(chunk):
            uc_ref[b * rows_per_batch:(b + 1) * rows_per_batch, s * lb:(s + 1) * lb] = (
                stage_ref[pl.ds(s, rows_per_batch, stride=chunk), :].astype(BF16))

    for r0 in range(0, n_rows, row_tile):
        xs_ref[r0:r0 + row_tile, :] = jnp.dot(uc_ref[r0:r0 + row_tile, :], p_ref[...], preferred_element_type=F32)

    def step(c, carry):
        out = []
        for b in range(n_batch):
            xr, xi = carry[b]
            rows = pl.ds(pl.multiple_of(b * rows_per_batch + c * scan_rows, scan_rows), scan_rows)
            inc = xs_ref[rows, :]
            starts_re, starts_im = [], []
            for k in range(scan_rows):
                starts_re.append(xr)
                starts_im.append(xi)
                xr, xi = (al_re * xr - al_im * xi + inc[k:k + 1, :gn], al_re * xi + al_im * xr + inc[k:k + 1, gn:])
            xs_ref[rows, :gn] = jnp.concatenate(starts_re, axis=0)
            xs_ref[rows, gn:] = jnp.concatenate(starts_im, axis=0)
            out.append((xr, xi))
        return tuple(out)

    x0 = jnp.zeros((1, gn), F32)
    carry = tuple((x0, x0) for _ in range(n_batch))
    for c in range(rows_per_batch // scan_rows):
        carry = step(c, carry)

    for b in range(n_batch):
        for r0 in range(0, rows_per_batch, row_tile):
            rows = slice(b * rows_per_batch + r0, b * rows_per_batch + r0 + row_tile)
            x_start = xs_ref[rows, :].astype(BF16)
            for c0 in range(0, kdim, col_tile):
                cols = slice(c0, c0 + col_tile)
                y = jnp.dot(uc_ref[rows, :c0 + col_tile], t_ref[:c0 + col_tile, cols], preferred_element_type=F32)
                y = y + lax.dot_general(x_start, qt_ref[cols, :], (((1,), (1,)), ((), ())),
                                        preferred_element_type=F32)
                y = jax.nn.gelu(y + d_ref[0][:, cols] * uc_ref[rows, cols].astype(F32))
                for s in range(c0 // lb, (c0 + col_tile) // lb):
                    stage_ref[pl.ds(r0 * chunk + s, row_tile, stride=chunk), :] = y[:, s * lb - c0:(s + 1) * lb - c0]
        y_ref[0, b * steps_per_batch:(b + 1) * steps_per_batch, :] = stage_ref[...].astype(y_ref.dtype)


def _s5(slabs, slab0, nb, a_re, a_im, log_dt, b_re, b_im, c_re, c_im, d_skip, n_batch):
    _, n_steps, lb = slabs.shape
    gpb, gn, chunk = GROUPS_PER_LANE_BLOCK, STATE_LANES, SSM_CHUNK
    n_rows, kdim = n_steps // chunk, chunk * lb
    rows_per_batch = n_rows // n_batch
    row_tile = _tile(rows_per_batch, 256)

    def lane_row(v):
        return v.astype(F32).reshape(nb, 1, gn)

    ldt = jnp.repeat(log_dt.astype(F32), SSM_STATE).reshape(nb, 1, gn)
    bt_re = jnp.tile(b_re.astype(F32).reshape(nb, gpb, SSM_STATE, SSM_GROUP).transpose(0, 3, 1, 2).reshape(nb, SSM_GROUP, gn), (1, gpb, 1))
    bt_im = jnp.tile(b_im.astype(F32).reshape(nb, gpb, SSM_STATE, SSM_GROUP).transpose(0, 3, 1, 2).reshape(nb, SSM_GROUP, gn), (1, gpb, 1))
    ct_re = jnp.tile(c_re.astype(F32).reshape(nb, lb, SSM_STATE), (1, 1, gpb))
    ct_im = jnp.tile(c_im.astype(F32).reshape(nb, lb, SSM_STATE), (1, 1, gpb))
    d_row = jnp.tile(d_skip.astype(F32).reshape(nb, 1, lb), (1, 1, chunk))

    def per_channel_rows(v):
        return jnp.repeat(v.astype(F32).reshape(nb, gpb, SSM_STATE), SSM_GROUP, axis=1)

    ldt_d = per_channel_rows(jnp.broadcast_to(log_dt[:, None], a_re.shape))
    bd_re = b_re.astype(F32).reshape(nb, gpb, SSM_STATE, SSM_GROUP).transpose(0, 1, 3, 2).reshape(nb, lb, SSM_STATE)
    bd_im = b_im.astype(F32).reshape(nb, gpb, SSM_STATE, SSM_GROUP).transpose(0, 1, 3, 2).reshape(nb, lb, SSM_STATE)
    cd_re = c_re.astype(F32).reshape(nb, lb, SSM_STATE)
    cd_im = c_im.astype(F32).reshape(nb, lb, SSM_STATE)
    dia = pl.BlockSpec((1, lb, SSM_STATE), lambda j: (j, 0, 0))

    vec = pl.BlockSpec((1, 1, gn), lambda j: (j, 0, 0))
    mat = pl.BlockSpec((1, lb, gn), lambda j: (j, 0, 0))
    blk = pl.BlockSpec((1, n_steps, lb), lambda j: (j, 0, 0))
    blk_in = pl.BlockSpec((1, n_steps, lb), lambda j: (slab0 + j, 0, 0))
    scratch = [
        pltpu.VMEM((kdim, 2 * gn), BF16),
        pltpu.VMEM((kdim, 2 * gn), BF16),
        pltpu.VMEM((kdim, kdim), BF16),
        pltpu.VMEM((chunk, lb, lb), BF16),
        pltpu.VMEM((n_steps // n_batch, lb), F32),
        pltpu.VMEM((n_rows, kdim), BF16),
        pltpu.VMEM((n_rows, 2 * gn), F32),
    ]
    scratch_bytes = (2 * kdim * 2 * gn * 2 + kdim * kdim * 2 + chunk * lb * lb * 2 + n_steps // n_batch * lb * 4
                     + n_rows * kdim * 2 + n_rows * 2 * gn * 4)
    return pl.pallas_call(
        functools.partial(_s5_kernel, n_batch=n_batch, rows_per_batch=rows_per_batch, row_tile=row_tile),
        grid=(nb,),
        in_specs=[blk_in, vec, vec, vec, mat, mat, mat, mat, pl.BlockSpec((1, 1, kdim), lambda j: (j, 0, 0))] + [dia] * 7,
        out_specs=blk,
        out_shape=jax.ShapeDtypeStruct((nb, n_steps, lb), BF16),
        scratch_shapes=scratch,
        compiler_params=_params([n_steps * lb * 2, n_steps * lb * 2, 4 * lb * gn * 4], scratch_bytes=scratch_bytes,
                                temp_bytes=4 * row_tile * kdim * 4),
        name="s5_scan",
    )(slabs, lane_row(a_re), lane_row(a_im), ldt, bt_re, bt_im, ct_re, ct_im, d_row,
      per_channel_rows(a_re), per_channel_rows(a_im), ldt_d, bd_re, bd_im, cd_re, cd_im)


def _glu_kernel(y_ref, w_ref, b_ref, z_ref, o_ref, *, col_tile):
    lb = y_ref.shape[2]
    y = jnp.concatenate([y_ref[k] for k in range(y_ref.shape[0])], axis=1)
    for c0 in range(0, o_ref.shape[1], col_tile):
        cols = slice(c0, c0 + col_tile)
        a = jnp.dot(y, w_ref[:, cols], preferred_element_type=F32) + b_ref[:, cols]
        z = z_ref[:, cols].astype(F32)
        ycol = jnp.concatenate([y_ref[k] for k in range(c0 // lb, (c0 + col_tile) // lb)], axis=1).astype(F32)
        o_ref[:, cols] = (ycol * z / ((1.0 + jnp.exp(-a)) * (1.0 + jnp.exp(-z)))).astype(o_ref.dtype)


def _glu(y_blocks, w_glu, b_glu, proj, z_col0, tm, col_tile):
    nb, n, lb = y_blocks.shape
    width = nb * lb
    tm, col_tile = _tile(n, tm), _tile(width, col_tile)
    assert z_col0 % width == 0
    return pl.pallas_call(
        functools.partial(_glu_kernel, col_tile=col_tile),
        grid=(n // tm,),
        in_specs=[
            pl.BlockSpec((nb, tm, lb), lambda i: (0, i, 0)),
            pl.BlockSpec((width, width), lambda i: (0, 0)),
            pl.BlockSpec((1, width), lambda i: (0, 0)),
            pl.BlockSpec((tm, width), lambda i: (i, z_col0 // width)),
        ],
        out_specs=pl.BlockSpec((tm, width), lambda i: (i, 0)),
        out_shape=jax.ShapeDtypeStruct((n, width), BF16),
        compiler_params=_params([tm * width * 2, width * width * 2, tm * width * 2, tm * width * 2],
                                temp_bytes=tm * width * 2 + 4 * tm * col_tile * 4),
        name="s5_glu",
    )(y_blocks, w_glu, b_glu.reshape(1, width).astype(F32), proj)


def _memattn_kernel(q_ref, z_ref, kv_ref, qg_ref, kg_ref, o_ref, kn_ref, *, width):
    dm = width // MEM_HEADS

    @pl.when(pl.program_id(1) == 0)
    def _():
        for hd in range(MEM_HEADS):
            cols = slice(hd * dm, (hd + 1) * dm)
            k = kv_ref[:, cols].astype(F32)
            kn = k * lax.rsqrt(jnp.mean(k * k, axis=-1, keepdims=True) + EPS) * kg_ref[...]
            kn_ref[:, cols] = kn.astype(BF16)

    for hd in range(MEM_HEADS):
        cols = slice(hd * dm, (hd + 1) * dm)
        q = q_ref[:, cols].astype(F32)
        qn = q * lax.rsqrt(jnp.mean(q * q, axis=-1, keepdims=True) + EPS) * qg_ref[...]
        v = kv_ref[:, width + hd * dm:width + (hd + 1) * dm]
        s = lax.dot_general(qn.astype(BF16), kn_ref[:, cols], (((1,), (1,)), ((), ())),
                            preferred_element_type=F32) * (dm ** -0.5)
        p = jnp.exp(s - jnp.max(s, axis=-1, keepdims=True))
        l = jnp.sum(p, axis=-1, keepdims=True)
        o = jnp.dot(p.astype(BF16), v, preferred_element_type=F32) / l
        o_ref[:, cols] = (o * jax.nn.silu(z_ref[:, cols].astype(F32))).astype(o_ref.dtype)


def _memattn(proj, q_col0, z_col0, kv, q_gain, k_gain, bsz, seq, width, tq):
    assert q_col0 % width == 0 and z_col0 % width == 0
    dm = width // MEM_HEADS
    n_mem = kv.shape[0] // bsz
    tq = _tile(seq, tq)
    nq = seq // tq
    return pl.pallas_call(
        functools.partial(_memattn_kernel, width=width),
        grid=(bsz, nq),
        in_specs=[
            pl.BlockSpec((tq, width), lambda b, i: (b * nq + i, q_col0 // width)),
            pl.BlockSpec((tq, width), lambda b, i: (b * nq + i, z_col0 // width)),
            pl.BlockSpec((n_mem, 2 * width), lambda b, i: (b, 0)),
            pl.BlockSpec((1, dm), lambda b, i: (0, 0)),
            pl.BlockSpec((1, dm), lambda b, i: (0, 0)),
        ],
        out_specs=pl.BlockSpec((tq, width), lambda b, i: (b * nq + i, 0)),
        out_shape=jax.ShapeDtypeStruct((bsz * seq, width), BF16),
        scratch_shapes=[pltpu.VMEM((n_mem, width), BF16)],
        compiler_params=_params([3 * tq * width * 2, n_mem * 2 * width * 2], scratch_bytes=n_mem * width * 2,
                                temp_bytes=8 * tq * dm * 4, n_axes=2),
        name="memory_attention",
    )(proj, proj, kv, q_gain.reshape(1, dm).astype(F32), k_gain.reshape(1, dm).astype(F32))


def _merge_kernel(ya_ref, ys_ref, yc_ref, wa_ref, ws_ref, wc_ref, ga_ref, gs_ref, gc_ref, o_ref):
    def term(y_ref, w_ref, g_ref):
        return jax.nn.sigmoid(g_ref[...].astype(F32)) * jnp.dot(y_ref[...], w_ref[...], preferred_element_type=F32)

    o_ref[...] = (term(ya_ref, wa_ref, ga_ref) + term(ys_ref, ws_ref, gs_ref)
                  + term(yc_ref, wc_ref, gc_ref)).astype(o_ref.dtype)


def _merge(y_a, y_s, y_c, w_a, w_s, w_c, proj, g_col0, tm, tn):
    n, width = y_a.shape
    d = w_a.shape[1]
    tm, tn = _tile(n, tm), _tile(d, tn)
    y_spec = pl.BlockSpec((tm, width), lambda i, j: (i, 0))
    w_spec = pl.BlockSpec((width, tn), lambda i, j: (0, j))

    def g_spec(branch):
        assert (g_col0 + branch * d) % tn == 0
        return pl.BlockSpec((tm, tn), lambda i, j: (i, (g_col0 + branch * d) // tn + j))

    return pl.pallas_call(
        _merge_kernel,
        grid=(n // tm, d // tn),
        in_specs=[y_spec] * 3 + [w_spec] * 3 + [g_spec(0), g_spec(1), g_spec(2)],
        out_specs=pl.BlockSpec((tm, tn), lambda i, j: (i, j)),
        out_shape=jax.ShapeDtypeStruct((n, d), BF16),
        compiler_params=_params([3 * tm * width * 2, 3 * width * tn * 2, 4 * tm * tn * 2], temp_bytes=4 * tm * tn * 4, n_axes=2),
        name="branch_merge",
    )(y_a, y_s, y_c, w_a, w_s, w_c, proj, proj, proj)


def _outproj_kernel(m_ref, w_ref, x_ref, o_ref):
    o_ref[...] = x_ref[...] + jnp.dot(m_ref[...], w_ref[...], preferred_element_type=F32)


def _outproj(merged, w_out, x, tm, tn):
    n, d = merged.shape
    tm, tn = _tile(n, tm), _tile(d, tn)
    return pl.pallas_call(
        _outproj_kernel,
        grid=(n // tm, d // tn),
        in_specs=[
            pl.BlockSpec((tm, d), lambda i, j: (i, 0)),
            pl.BlockSpec((d, tn), lambda i, j: (0, j)),
            pl.BlockSpec((tm, tn), lambda i, j: (i, j)),
        ],
        out_specs=pl.BlockSpec((tm, tn), lambda i, j: (i, j)),
        out_shape=jax.ShapeDtypeStruct((n, d), F32),
        compiler_params=_params([tm * d * 2, d * tn * 2, 2 * tm * tn * 4], temp_bytes=tm * tn * 4, n_axes=2),
        name="out_projection",
    )(merged, w_out, x)


def kernel(x, mem, w_in, g_norm, g_mem, w_mem_kv, q_gain_a, k_gain_a, q_gain_c, k_gain_c, ssm_a_re, ssm_a_im, ssm_log_dt, ssm_b_re, ssm_b_im, ssm_c_re, ssm_c_im, ssm_d, w_glu, b_glu, w_br_a, w_br_s, w_br_c, w_out):
    bsz, seq, d_model = x.shape
    width = w_glu.shape[0]
    n_tok = bsz * seq
    n_mem = mem.shape[1]
    assert seq % MOBA_BLOCK == 0 and seq % SSM_CHUNK == 0 and width % V7X_LANES == 0
    assert w_in.shape == (d_model, 8 * width + 3 * d_model)

    x2 = x.reshape(n_tok, d_model)
    h = _rmsnorm(x2, g_norm, rows=ROWS_RMSNORM)
    proj, slabs = _matmul(h, w_in.astype(BF16), TILE_IN_PROJECTION[0], _tile(width, TILE_IN_PROJECTION[1]),
                          "in_projection", slab_cols=5 * width)
    nb = width // V7X_LANES

    m = _rmsnorm(mem.reshape(bsz * n_mem, d_model), g_mem, rows=ROWS_RMSNORM)
    kv = _matmul(m, w_mem_kv, *TILE_MEMORY_KV, "memory_kv_projection")

    y_a = _moba(slabs, q_gain_a, k_gain_a, bsz, seq, width)

    y_g = _s5(slabs, 4 * nb, nb, ssm_a_re, ssm_a_im, ssm_log_dt, ssm_b_re, ssm_b_im, ssm_c_re, ssm_c_im, ssm_d, bsz)
    y_s = _glu(y_g, w_glu.astype(BF16), b_glu, proj, 0, *TILE_GLU)

    y_c = _memattn(proj, width, 2 * width, kv, q_gain_c, k_gain_c, bsz, seq, width, ROWS_MEMORY_ATTENTION)

    merged = _merge(y_a, y_s, y_c, w_br_a.astype(BF16), w_br_s.astype(BF16), w_br_c.astype(BF16), proj,
                    3 * width, *TILE_BRANCH_MERGE)
    out = _outproj(merged, w_out.astype(BF16), x2, *TILE_OUT_PROJECTION)
    return out.reshape(bsz, seq, d_model)
```

```python
import functools

import jax
import jax.numpy as jnp
from jax import lax
from jax.experimental import pallas as pl
from jax.experimental.pallas import tpu as pltpu

F32 = jnp.float32
BF16 = jnp.bfloat16

V7X_LANES = 128
V7X_F32_SUBLANES = 8
V7X_BF16_SUBLANES = 16
V7X_VMEM_BYTES = 64 * 1024 * 1024
V7X_VMEM_RESERVE_BYTES = 6 * 1024 * 1024

ATTN_HEAD_DIM = 128
MOBA_BLOCK = 256
MOBA_TOP_K = 3
SSM_GROUP = 16
SSM_STATE = 64
MEM_HEADS = 4
EPS = 1e-6
NEG = -1e30
LOG2E = 1.4426950408889634

MOBA_KEY_GROUP = 2
MOBA_QUERY_GROUP = 4
MOBA_QUERY_STRIP = 256
SSM_CHUNK = 16
S5_OUT_COL_TILE = 512
GROUPS_PER_LANE_BLOCK = V7X_LANES // SSM_GROUP
STATE_LANES = GROUPS_PER_LANE_BLOCK * SSM_STATE

TILE_IN_PROJECTION = (1024, 1024)
TILE_MEMORY_KV = (512, 512)
TILE_BRANCH_MERGE = (1024, 512)
TILE_OUT_PROJECTION = (1024, 512)
TILE_GLU = (512, 256)
ROWS_RMSNORM = 512
ROWS_MEMORY_ATTENTION = 512


def _tile(n, pref):
    t = min(n, pref)
    while n % t:
        t -= V7X_LANES
    assert t > 0
    return t


def _params(block_bytes, scratch_bytes=0, temp_bytes=0, n_axes=1):
    need = 2 * sum(block_bytes) + scratch_bytes + temp_bytes
    limit = min(max(need, 16 * 1024 * 1024), V7X_VMEM_BYTES - V7X_VMEM_RESERVE_BYTES)
    return pltpu.CompilerParams(dimension_semantics=("arbitrary",) * n_axes, vmem_limit_bytes=int(limit))


def _rmsnorm_kernel(x_ref, g_ref, o_ref):
    x = x_ref[...].astype(F32)
    ms = jnp.mean(x * x, axis=-1, keepdims=True)
    o_ref[...] = (x * lax.rsqrt(ms + EPS) * g_ref[...]).astype(o_ref.dtype)


def _rmsnorm(x, gain, rows):
    n, d = x.shape
    tm = _tile(n, rows)
    return pl.pallas_call(
        _rmsnorm_kernel,
        grid=(n // tm,),
        in_specs=[pl.BlockSpec((tm, d), lambda i: (i, 0)), pl.BlockSpec((1, d), lambda i: (0, 0))],
        out_specs=pl.BlockSpec((tm, d), lambda i: (i, 0)),
        out_shape=jax.ShapeDtypeStruct((n, d), BF16),
        compiler_params=_params([tm * d * 4, tm * d * 2], temp_bytes=2 * tm * d * 4),
        name="rmsnorm",
    )(x, gain.reshape(1, d).astype(F32))


def _mm_kernel(a_ref, b_ref, o_ref):
    o_ref[...] = jnp.dot(a_ref[...], b_ref[...].astype(BF16), preferred_element_type=F32).astype(o_ref.dtype)


def _mm_slabs_kernel(a_ref, b_ref, o_ref, slab_ref):
    res = jnp.dot(a_ref[...], b_ref[...], preferred_element_type=F32).astype(o_ref.dtype)
    o_ref[...] = res
    for k in range(slab_ref.shape[0]):
        slab_ref[k] = res[:, k * V7X_LANES:(k + 1) * V7X_LANES]


def _matmul(a, b, tm, tn, name, slab_cols=0):
    m, k = a.shape
    _, n = b.shape
    tm, tn = _tile(m, tm), _tile(n, tn)
    in_specs = [pl.BlockSpec((tm, k), lambda i, j: (i, 0)), pl.BlockSpec((k, tn), lambda i, j: (0, j))]
    b_cast_bytes = k * tn * 2 if b.dtype != BF16 else 0
    params = _params([tm * k * 2, k * tn * b.dtype.itemsize, 2 * tm * tn * 2],
                     temp_bytes=tm * tn * (4 + 4 + 2) + b_cast_bytes, n_axes=2)
    if not slab_cols:
        return pl.pallas_call(_mm_kernel, grid=(m // tm, n // tn), in_specs=in_specs,
                              out_specs=pl.BlockSpec((tm, tn), lambda i, j: (i, j)),
                              out_shape=jax.ShapeDtypeStruct((m, n), BF16), compiler_params=params, name=name)(a, b)
    assert slab_cols % tn == 0 and 0 < slab_cols < n
    nj, per = slab_cols // tn, tn // V7X_LANES
    out_spec = pl.BlockSpec((tm, tn), lambda i, j: (i, jnp.maximum(j - nj, 0)))
    slab_spec = pl.BlockSpec((per, tm, V7X_LANES), lambda i, j: (jnp.minimum(j, nj), i, 0))
    return pl.pallas_call(
        _mm_slabs_kernel,
        grid=(m // tm, n // tn), in_specs=in_specs, out_specs=[out_spec, slab_spec],
        out_shape=[jax.ShapeDtypeStruct((m, n - slab_cols), BF16),
                   jax.ShapeDtypeStruct(((nj + 1) * per, m, V7X_LANES), BF16)],
        compiler_params=params, name=name)(a, b)


MOBA_ADD_ROW = ATTN_HEAD_DIM + V7X_BF16_SUBLANES
MOBA_PIECES = 3


def _split_bf16(x):
    hi = x.astype(BF16).astype(F32)
    mid = (x - hi).astype(BF16).astype(F32)
    return hi, mid, x - hi - mid


def _moba_kernel(q_ref, q_next_ref, k_ref, v_ref, z_ref, qg_ref, kg_ref, slope_ref, o_ref,
                 kn_ref, vt_ref, kmean_ref, rhs_ref, s_ref, acc_ref, *, n_blocks, kgroup, qgroup):
    it = pl.program_id(2)
    bs, hd, sub = MOBA_BLOCK, ATTN_HEAD_DIM, V7X_BF16_SUBLANES
    kt, qt = kgroup * bs, qgroup * bs
    n_kgroups = n_blocks // kgroup
    strip = min(qt, MOBA_QUERY_STRIP)
    slope2 = slope_ref[0][:, :1] * LOG2E

    def prepare_keys(g):
        key_off = lax.broadcasted_iota(jnp.int32, (bs, hd), 0).astype(F32)
        lane = lax.broadcasted_iota(jnp.int32, (bs, hd), 1)
        add_lane = lane - (MOBA_ADD_ROW - hd)
        in_add = jnp.logical_and(add_lane >= 0, add_lane < MOBA_PIECES * n_blocks)
        for bi in range(kgroup):
            c = g * kgroup + bi
            rows = pl.ds(pl.multiple_of(c * bs, bs), bs)
            kb = k_ref[rows, :].astype(F32)
            ms = jnp.mean(kb * kb, axis=-1, keepdims=True)
            kn = kb * lax.rsqrt(ms + EPS) * kg_ref[...]
            k_aug = jnp.where(lane < MOBA_PIECES, key_off,
                              jnp.where(jnp.logical_and(in_add, add_lane % n_blocks == c), 1.0, 0.0))
            kn_ref[g, bi * bs:(bi + 1) * bs, :] = jnp.concatenate([kn.astype(BF16), k_aug.astype(BF16)], axis=1)
            kmean_ref[pl.ds(c, 1), :] = jnp.mean(kn, axis=0, keepdims=True)
            vt_ref[g, :hd, bi * bs:(bi + 1) * bs] = v_ref[rows, :].astype(F32).T.astype(BF16)
        vt_ref[g, hd:, :] = jnp.where(lax.broadcasted_iota(jnp.int32, (sub, kt), 0) == 0, 1.0, 0.0).astype(BF16)

    def prepare_queries(q_block_ref, tile, rhs_slot):
        q = q_block_ref[...].astype(F32)
        ms = jnp.mean(q * q, axis=-1, keepdims=True)
        qn_t = (q * lax.rsqrt(ms + EPS) * qg_ref[...]).T

        gate = jnp.dot(kmean_ref[...], qn_t, preferred_element_type=F32, precision=lax.Precision.HIGHEST)
        blk = lax.broadcasted_iota(jnp.int32, gate.shape, 0)
        own = tile * qgroup + lax.broadcasted_iota(jnp.int32, gate.shape, 1) // bs
        blk_f = blk.astype(F32)
        past = blk < own
        g = jnp.where(past, gate, NEG)
        sel = jnp.zeros(gate.shape, jnp.bool_)
        for _ in range(min(MOBA_TOP_K, n_blocks)):
            top = jnp.max(g, axis=0, keepdims=True)
            first = jnp.min(jnp.where(g == top, blk_f, float(n_blocks)), axis=0, keepdims=True)
            pick = blk_f == first
            sel = jnp.logical_or(sel, pick)
            g = jnp.where(pick, -jnp.inf, g)
        sel = jnp.logical_and(sel, past)
        add = jnp.where(sel, slope2 * (bs * (blk - own)).astype(F32), jnp.where(blk == own, 0.0, NEG))

        row = lax.broadcasted_iota(jnp.int32, (sub, qt), 0)
        s_hi, s_mid, s_lo = _split_bf16(slope2)
        slope_rows = jnp.where(row == 0, s_hi, jnp.where(row == 1, s_mid, jnp.where(row == 2, s_lo, 0.0)))
        pad = jnp.zeros((2 * hd - MOBA_ADD_ROW - MOBA_PIECES * n_blocks, qt), F32)
        rhs_ref[rhs_slot] = jnp.concatenate([qn_t * (hd ** -0.5 * LOG2E), slope_rows, *_split_bf16(add), pad],
                                            axis=0).astype(BF16)

    own_groups = qgroup // kgroup

    @pl.when(it == 0)
    def _():
        kmean_ref[...] = jnp.zeros(kmean_ref.shape, F32)
        for t in range(own_groups):
            prepare_keys(t)
        prepare_queries(q_ref, 0, 0)

    rhs_now = it % 2

    def scores_to(slot, gi, causal_group=None):
        s = jnp.dot(kn_ref[gi], rhs_ref[rhs_now], preferred_element_type=F32)
        top = None
        for bi in range(kgroup):
            part = s[bi * bs:(bi + 1) * bs]
            if causal_group is not None:
                d0 = (causal_group * kgroup + bi) * bs
                tri = lax.broadcasted_iota(jnp.int32, (bs, bs), 0) <= lax.broadcasted_iota(jnp.int32, (bs, bs), 1)
                pieces = [jnp.where(tri, part[:, d0:d0 + bs], NEG)]
                if d0 > 0:
                    pieces.insert(0, part[:, :d0])
                if d0 + bs < qt:
                    pieces.append(part[:, d0 + bs:])
                part = jnp.concatenate(pieces, axis=1)
            s_ref[slot, bi * bs:(bi + 1) * bs, :] = part
            top = part if top is None else jnp.maximum(top, part)
        return jnp.max(top, axis=0, keepdims=True)

    def accumulate(slot, gi, m, top, causal_group=None):
        m_new = jnp.maximum(m, top)
        alpha = jnp.exp2(m - m_new)
        for c0 in range(0, qt, strip):
            cols = slice(c0, c0 + strip)
            live = kgroup
            if causal_group is not None:
                live = min(max(c0 // bs - causal_group * kgroup + 1, 0), kgroup)
            if live == 0:
                continue
            p = jnp.exp2(s_ref[slot, :live * bs, cols] - m_new[:, cols]).astype(BF16)
            pv = jnp.dot(vt_ref[gi, :, :live * bs], p, preferred_element_type=F32)
            acc_ref[:, cols] = alpha[:, cols] * acc_ref[:, cols] + pv
        return m_new

    def pair(k, m):
        top0 = scores_to(0, 2 * k)
        top1 = scores_to(1, 2 * k + 1)
        return accumulate(1, 2 * k + 1, accumulate(0, 2 * k, m, top0), top1)

    acc_ref[...] = jnp.zeros(acc_ref.shape, F32)
    m = lax.fori_loop(0, it, pair, jnp.full((1, qt), 0.1 * NEG, F32))
    top0 = scores_to(0, 2 * it, causal_group=0)
    top1 = scores_to(1, 2 * it + 1, causal_group=1)
    accumulate(1, 2 * it + 1, accumulate(0, 2 * it, m, top0, causal_group=0), top1, causal_group=1)
    for t in range(own_groups):
        prepare_keys(jnp.minimum((it + 1) * own_groups + t, n_kgroups - 1))
    prepare_queries(q_next_ref, it + 1, 1 - rhs_now)
    acc = acc_ref[...]
    o = (acc[:hd] / acc[hd:hd + 1]).T
    o_ref[...] = (o * jax.nn.silu(z_ref[...].astype(F32))).astype(o_ref.dtype)


def _moba(slabs, q_gain, k_gain, bsz, seq, width):
    n_heads = width // ATTN_HEAD_DIM
    n_blocks = seq // MOBA_BLOCK
    hd, bs, sub = ATTN_HEAD_DIM, MOBA_BLOCK, V7X_BF16_SUBLANES
    kgroup, qgroup = MOBA_KEY_GROUP, MOBA_QUERY_GROUP
    assert qgroup == 2 * kgroup and n_blocks % qgroup == 0
    assert MOBA_ADD_ROW + MOBA_PIECES * n_blocks <= 2 * hd
    n_kgroups, n_qtiles, kt, qt = n_blocks // kgroup, n_blocks // qgroup, kgroup * bs, qgroup * bs
    slopes = jnp.asarray([[[2.0 ** (-8.0 * (h + 1) / n_heads)] * V7X_LANES] for h in range(n_heads)], F32)
    scratch = [
        pltpu.VMEM((n_kgroups, kt, 2 * hd), BF16),
        pltpu.VMEM((n_kgroups, hd + sub, kt), BF16),
        pltpu.VMEM((n_blocks, hd), F32),
        pltpu.VMEM((2, 2 * hd, qt), BF16),
        pltpu.VMEM((2, kt, qt), F32),
        pltpu.VMEM((hd + sub, qt), F32),
    ]
    scratch_bytes = (n_kgroups * (kt * 2 * hd + (hd + sub) * kt) * 2 + n_blocks * hd * 4
                     + 2 * 2 * hd * qt * 2 + 2 * kt * qt * 4 + (hd + sub) * qt * 4)
    return pl.pallas_call(
        functools.partial(_moba_kernel, n_blocks=n_blocks, kgroup=kgroup, qgroup=qgroup),
        grid=(bsz, n_heads, n_qtiles),
        in_specs=[
            pl.BlockSpec((None, qt, hd), lambda b, h, i: (h, b * n_qtiles + i, 0)),
            pl.BlockSpec((None, qt, hd), lambda b, h, i: (h, b * n_qtiles + jnp.minimum(i + 1, n_qtiles - 1), 0)),
            pl.BlockSpec((None, seq, hd), lambda b, h, i: (n_heads + h, b, 0)),
            pl.BlockSpec((None, seq, hd), lambda b, h, i: (2 * n_heads + h, b, 0)),
            pl.BlockSpec((None, qt, hd), lambda b, h, i: (3 * n_heads + h, b * n_qtiles + i, 0)),
            pl.BlockSpec((1, hd), lambda b, h, i: (0, 0)),
            pl.BlockSpec((1, hd), lambda b, h, i: (0, 0)),
            pl.BlockSpec((1, 1, V7X_LANES), lambda b, h, i: (h, 0, 0)),
        ],
        out_specs=pl.BlockSpec((qt, hd), lambda b, h, i: (b * n_qtiles + i, h)),
        out_shape=jax.ShapeDtypeStruct((bsz * seq, width), BF16),
        scratch_shapes=scratch,
        compiler_params=_params([2 * seq * hd * 2, 4 * qt * hd * 2], scratch_bytes=scratch_bytes,
                                temp_bytes=4 * kt * qt * 4, n_axes=3),
        name="moba_attention",
    )(slabs, slabs, slabs, slabs, slabs, q_gain.reshape(1, hd).astype(F32), k_gain.reshape(1, hd).astype(F32), slopes)


def _zoh(ar, ai, log_dt):
    dt = jnp.exp(log_dt)
    mag = jnp.exp(dt * ar)
    abr = mag * jnp.cos(dt * ai)
    abi = mag * jnp.sin(dt * ai)
    den = ar * ar + ai * ai
    nr = abr - 1.0
    return abr, abi, (nr * ar + abi * ai) / den, (abi * ar - nr * ai) / den


def _s5_kernel(u_ref, ar_ref, ai_ref, ldt_ref, bbr_ref, bbi_ref, ccr_ref, cci_ref, d_ref,
               ard_ref, aid_ref, ldtd_ref, bdr_ref, bdi_ref, cdr_ref, cdi_ref, y_ref,
               p_ref, qt_ref, t_ref, dk_ref, stage_ref, uc_ref, xs_ref, *, n_batch, rows_per_batch, row_tile):
    lb, gn, chunk = V7X_LANES, STATE_LANES, SSM_CHUNK
    kdim = chunk * lb
    col_tile = min(kdim, S5_OUT_COL_TILE)
    scan_rows = V7X_F32_SUBLANES
    assert rows_per_batch % scan_rows == 0
    abr, abi, f_re, f_im = _zoh(ar_ref[0], ai_ref[0], ldt_ref[0])

    abr_d, abi_d, f_re_d, f_im_d = _zoh(ard_ref[0], aid_ref[0], ldtd_ref[0])
    bd_re = f_re_d * bdr_ref[0] - f_im_d * bdi_ref[0]
    bd_im = f_re_d * bdi_ref[0] + f_im_d * bdr_ref[0]
    cd = jnp.concatenate([cdr_ref[0], -cdi_ref[0]], axis=1)
    same_group = (lax.broadcasted_iota(jnp.int32, (lb, lb), 0) // SSM_GROUP
                  == lax.broadcasted_iota(jnp.int32, (lb, lb), 1) // SSM_GROUP)
    pr_d = jnp.ones(abr_d.shape, F32)
    pi_d = jnp.zeros(abr_d.shape, F32)
    for tau in range(chunk):
        lag = jnp.concatenate([bd_re * pr_d - bd_im * pi_d, bd_re * pi_d + bd_im * pr_d], axis=1)
        blocks = lax.dot_general(lag, cd, (((1,), (1,)), ((), ())), preferred_element_type=F32,
                                 precision=lax.Precision.HIGHEST)
        dk_ref[tau] = jnp.where(same_group, blocks, 0.0).astype(BF16)
        pr_d, pi_d = pr_d * abr_d - pi_d * abi_d, pr_d * abi_d + pi_d * abr_d

    row_g = lax.broadcasted_iota(jnp.int32, (lb, gn), 0) // SSM_GROUP
    col_g = lax.broadcasted_iota(jnp.int32, (lb, gn), 1) // SSM_STATE
    same = row_g == col_g
    bb_re, bb_im = bbr_ref[0], bbi_ref[0]
    bbar_re = jnp.where(same, f_re * bb_re - f_im * bb_im, 0.0)
    bbar_im = jnp.where(same, f_re * bb_im + f_im * bb_re, 0.0)
    cc_re = jnp.where(same, ccr_ref[0], 0.0)
    cc_im = jnp.where(same, cci_ref[0], 0.0)

    pr = jnp.ones((1, gn), F32)
    pi = jnp.zeros((1, gn), F32)
    for tau in range(chunk + 1):
        if tau < chunk:
            s = chunk - 1 - tau
            p_ref[s * lb:(s + 1) * lb, :] = jnp.concatenate(
                [bbar_re * pr - bbar_im * pi, bbar_re * pi + bbar_im * pr], axis=1).astype(BF16)
        if tau >= 1:
            t = tau - 1
            qt_ref[t * lb:(t + 1) * lb, :] = jnp.concatenate(
                [cc_re * pr - cc_im * pi, -(cc_re * pi + cc_im * pr)], axis=1).astype(BF16)
        if tau < chunk:
            pr, pi = pr * abr - pi * abi, pr * abi + pi * abr
    al_re, al_im = pr, pi

    zero = jnp.zeros((lb, lb), BF16)
    for s in range(chunk):
        for t in range(chunk):
            t_ref[s * lb:(s + 1) * lb, t * lb:(t + 1) * lb] = dk_ref[t - s] if t >= s else zero

    n_rows = n_batch * rows_per_batch
    steps_per_batch = rows_per_batch * chunk
    for b in range(n_batch):
        stage_ref[...] = u_ref[0, b * steps_per_batch:(b + 1) * steps_per_batch, :].astype(F32)
        for s in range(chunk):
            uc_ref[b * rows_per_batch:(b + 1) * rows_per_batch, s * lb:(s + 1) * lb] = (
                stage_ref[pl.ds(s, rows_per_batch, stride=chunk), :].astype(BF16))

    for r0 in range(0, n_rows, row_tile):
        xs_ref[r0:r0 + row_tile, :] = jnp.dot(uc_ref[r0:r0 + row_tile, :], p_ref[...], preferred_element_type=F32)

    def step(c, carry):
        out = []
        for b in range(n_batch):
            xr, xi = carry[b]
            rows = pl.ds(pl.multiple_of(b * rows_per_batch + c * scan_rows, scan_rows), scan_rows)
            inc = xs_ref[rows, :]
            starts_re, starts_im = [], []
            for k in range(scan_rows):
                starts_re.append(xr)
                starts_im.append(xi)
                xr, xi = (al_re * xr - al_im * xi + inc[k:k + 1, :gn], al_re * xi + al_im * xr + inc[k:k + 1, gn:])
            xs_ref[rows, :gn] = jnp.concatenate(starts_re, axis=0)
            xs_ref[rows, gn:] = jnp.concatenate(starts_im, axis=0)
            out.append((xr, xi))
        return tuple(out)

    x0 = jnp.zeros((1, gn), F32)
    carry = tuple((x0, x0) for _ in range(n_batch))
    for c in range(rows_per_batch // scan_rows):
        carry = step(c, carry)

    for b in range(n_batch):
        for r0 in range(0, rows_per_batch, row_tile):
            rows = slice(b * rows_per_batch + r0, b * rows_per_batch + r0 + row_tile)
            x_start = xs_ref[rows, :].astype(BF16)
            for c0 in range(0, kdim, col_tile):
                cols = slice(c0, c0 + col_tile)
                y = jnp.dot(uc_ref[rows, :c0 + col_tile], t_ref[:c0 + col_tile, cols], preferred_element_type=F32)
                y = y + lax.dot_general(x_start, qt_ref[cols, :], (((1,), (1,)), ((), ())),
                                        preferred_element_type=F32)
                y = jax.nn.gelu(y + d_ref[0][:, cols] * uc_ref[rows, cols].astype(F32))
                for s in range(c0 // lb, (c0 + col_tile) // lb):
                    stage_ref[pl.ds(r0 * chunk + s, row_tile, stride=chunk), :] = y[:, s * lb - c0:(s + 1) * lb - c0]
        y_ref[0, b * steps_per_batch:(b + 1) * steps_per_batch, :] = stage_ref[...].astype(y_ref.dtype)


def _s5(slabs, slab0, nb, a_re, a_im, log_dt, b_re, b_im, c_re, c_im, d_skip, n_batch):
    _, n_steps, lb = slabs.shape
    gpb, gn, chunk = GROUPS_PER_LANE_BLOCK, STATE_LANES, SSM_CHUNK
    n_rows, kdim = n_steps // chunk, chunk * lb
    rows_per_batch = n_rows // n_batch
    row_tile = _tile(rows_per_batch, 256)

    def lane_row(v):
        return v.astype(F32).reshape(nb, 1, gn)

    ldt = jnp.repeat(log_dt.astype(F32), SSM_STATE).reshape(nb, 1, gn)
    bt_re = jnp.tile(b_re.astype(F32).reshape(nb, gpb, SSM_STATE, SSM_GROUP).transpose(0, 3, 1, 2).reshape(nb, SSM_GROUP, gn), (1, gpb, 1))
    bt_im = jnp.tile(b_im.astype(F32).reshape(nb, gpb, SSM_STATE, SSM_GROUP).transpose(0, 3, 1, 2).reshape(nb, SSM_GROUP, gn), (1, gpb, 1))
    ct_re = jnp.tile(c_re.astype(F32).reshape(nb, lb, SSM_STATE), (1, 1, gpb))
    ct_im = jnp.tile(c_im.astype(F32).reshape(nb, lb, SSM_STATE), (1, 1, gpb))
    d_row = jnp.tile(d_skip.astype(F32).reshape(nb, 1, lb), (1, 1, chunk))

    def per_channel_rows(v):
        return jnp.repeat(v.astype(F32).reshape(nb, gpb, SSM_STATE), SSM_GROUP, axis=1)

    ldt_d = per_channel_rows(jnp.broadcast_to(log_dt[:, None], a_re.shape))
    bd_re = b_re.astype(F32).reshape(nb, gpb, SSM_STATE, SSM_GROUP).transpose(0, 1, 3, 2).reshape(nb, lb, SSM_STATE)
    bd_im = b_im.astype(F32).reshape(nb, gpb, SSM_STATE, SSM_GROUP).transpose(0, 1, 3, 2).reshape(nb, lb, SSM_STATE)
    cd_re = c_re.astype(F32).reshape(nb, lb, SSM_STATE)
    cd_im = c_im.astype(F32).reshape(nb, lb, SSM_STATE)
    dia = pl.BlockSpec((1, lb, SSM_STATE), lambda j: (j, 0, 0))

    vec = pl.BlockSpec((1, 1, gn), lambda j: (j, 0, 0))
    mat = pl.BlockSpec((1, lb, gn), lambda j: (j, 0, 0))
    blk = pl.BlockSpec((1, n_steps, lb), lambda j: (j, 0, 0))
    blk_in = pl.BlockSpec((1, n_steps, lb), lambda j: (slab0 + j, 0, 0))
    scratch = [
        pltpu.VMEM((kdim, 2 * gn), BF16),
        pltpu.VMEM((kdim, 2 * gn), BF16),
        pltpu.VMEM((kdim, kdim), BF16),
        pltpu.VMEM((chunk, lb, lb), BF16),
        pltpu.VMEM((n_steps // n_batch, lb), F32),
        pltpu.VMEM((n_rows, kdim), BF16),
        pltpu.VMEM((n_rows, 2 * gn), F32),
    ]
    scratch_bytes = (2 * kdim * 2 * gn * 2 + kdim * kdim * 2 + chunk * lb * lb * 2 + n_steps // n_batch * lb * 4
                     + n_rows * kdim * 2 + n_rows * 2 * gn * 4)
    return pl.pallas_call(
        functools.partial(_s5_kernel, n_batch=n_batch, rows_per_batch=rows_per_batch, row_tile=row_tile),
        grid=(nb,),
        in_specs=[blk_in, vec, vec, vec, mat, mat, mat, mat, pl.BlockSpec((1, 1, kdim), lambda j: (j, 0, 0))] + [dia] * 7,
        out_specs=blk,
        out_shape=jax.ShapeDtypeStruct((nb, n_steps, lb), BF16),
        scratch_shapes=scratch,
        compiler_params=_params([n_steps * lb * 2, n_steps * lb * 2, 4 * lb * gn * 4], scratch_bytes=scratch_bytes,
                                temp_bytes=4 * row_tile * kdim * 4),
        name="s5_scan",
    )(slabs, lane_row(a_re), lane_row(a_im), ldt, bt_re, bt_im, ct_re, ct_im, d_row,
      per_channel_rows(a_re), per_channel_rows(a_im), ldt_d, bd_re, bd_im, cd_re, cd_im)


def _glu_kernel(y_ref, w_ref, b_ref, z_ref, o_ref, *, col_tile):
    lb = y_ref.shape[2]
    y = jnp.concatenate([y_ref[k] for k in range(y_ref.shape[0])], axis=1)
    for c0 in range(0, o_ref.shape[1], col_tile):
        cols = slice(c0, c0 + col_tile)
        a = jnp.dot(y, w_ref[:, cols], preferred_element_type=F32) + b_ref[:, cols]
        z = z_ref[:, cols].astype(F32)
        ycol = jnp.concatenate([y_ref[k] for k in range(c0 // lb, (c0 + col_tile) // lb)], axis=1).astype(F32)
        o_ref[:, cols] = (ycol * z / ((1.0 + jnp.exp(-a)) * (1.0 + jnp.exp(-z)))).astype(o_ref.dtype)


def _glu(y_blocks, w_glu, b_glu, proj, z_col0, tm, col_tile):
    nb, n, lb = y_blocks.shape
    width = nb * lb
    tm, col_tile = _tile(n, tm), _tile(width, col_tile)
    assert z_col0 % width == 0
    return pl.pallas_call(
        functools.partial(_glu_kernel, col_tile=col_tile),
        grid=(n // tm,),
        in_specs=[
            pl.BlockSpec((nb, tm, lb), lambda i: (0, i, 0)),
            pl.BlockSpec((width, width), lambda i: (0, 0)),
            pl.BlockSpec((1, width), lambda i: (0, 0)),
            pl.BlockSpec((tm, width), lambda i: (i, z_col0 // width)),
        ],
        out_specs=pl.BlockSpec((tm, width), lambda i: (i, 0)),
        out_shape=jax.ShapeDtypeStruct((n, width), BF16),
        compiler_params=_params([tm * width * 2, width * width * 2, tm * width * 2, tm * width * 2],
                                temp_bytes=tm * width * 2 + 4 * tm * col_tile * 4),
        name="s5_glu",
    )(y_blocks, w_glu, b_glu.reshape(1, width).astype(F32), proj)


def _memattn_kernel(q_ref, z_ref, kv_ref, qg_ref, kg_ref, o_ref, kn_ref, *, width):
    dm = width // MEM_HEADS

    @pl.when(pl.program_id(1) == 0)
    def _():
        for hd in range(MEM_HEADS):
            cols = slice(hd * dm, (hd + 1) * dm)
            k = kv_ref[:, cols].astype(F32)
            kn = k * lax.rsqrt(jnp.mean(k * k, axis=-1, keepdims=True) + EPS) * kg_ref[...]
            kn_ref[:, cols] = kn.astype(BF16)

    for hd in range(MEM_HEADS):
        cols = slice(hd * dm, (hd + 1) * dm)
        q = q_ref[:, cols].astype(F32)
        qn = q * lax.rsqrt(jnp.mean(q * q, axis=-1, keepdims=True) + EPS) * qg_ref[...]
        v = kv_ref[:, width + hd * dm:width + (hd + 1) * dm]
        s = lax.dot_general(qn.astype(BF16), kn_ref[:, cols], (((1,), (1,)), ((), ())),
                            preferred_element_type=F32) * (dm ** -0.5)
        p = jnp.exp(s - jnp.max(s, axis=-1, keepdims=True))
        l = jnp.sum(p, axis=-1, keepdims=True)
        o = jnp.dot(p.astype(BF16), v, preferred_element_type=F32) / l
        o_ref[:, cols] = (o * jax.nn.silu(z_ref[:, cols].astype(F32))).astype(o_ref.dtype)


def _memattn(proj, q_col0, z_col0, kv, q_gain, k_gain, bsz, seq, width, tq):
    assert q_col0 % width == 0 and z_col0 % width == 0
    dm = width // MEM_HEADS
    n_mem = kv.shape[0] // bsz
    tq = _tile(seq, tq)
    nq = seq // tq
    return pl.pallas_call(
        functools.partial(_memattn_kernel, width=width),
        grid=(bsz, nq),
        in_specs=[
            pl.BlockSpec((tq, width), lambda b, i: (b * nq + i, q_col0 // width)),
            pl.BlockSpec((tq, width), lambda b, i: (b * nq + i, z_col0 // width)),
            pl.BlockSpec((n_mem, 2 * width), lambda b, i: (b, 0)),
            pl.BlockSpec((1, dm), lambda b, i: (0, 0)),
            pl.BlockSpec((1, dm), lambda b, i: (0, 0)),
        ],
        out_specs=pl.BlockSpec((tq, width), lambda b, i: (b * nq + i, 0)),
        out_shape=jax.ShapeDtypeStruct((bsz * seq, width), BF16),
        scratch_shapes=[pltpu.VMEM((n_mem, width), BF16)],
        compiler_params=_params([3 * tq * width * 2, n_mem * 2 * width * 2], scratch_bytes=n_mem * width * 2,
                                temp_bytes=8 * tq * dm * 4, n_axes=2),
        name="memory_attention",
    )(proj, proj, kv, q_gain.reshape(1, dm).astype(F32), k_gain.reshape(1, dm).astype(F32))


def _merge_kernel(ya_ref, ys_ref, yc_ref, wa_ref, ws_ref, wc_ref, ga_ref, gs_ref, gc_ref, o_ref):
    def term(y_ref, w_ref, g_ref):
        return jax.nn.sigmoid(g_ref[...].astype(F32)) * jnp.dot(y_ref[...], w_ref[...], preferred_element_type=F32)

    o_ref[...] = (term(ya_ref, wa_ref, ga_ref) + term(ys_ref, ws_ref, gs_ref)
                  + term(yc_ref, wc_ref, gc_ref)).astype(o_ref.dtype)


def _merge(y_a, y_s, y_c, w_a, w_s, w_c, proj, g_col0, tm, tn):
    n, width = y_a.shape
    d = w_a.shape[1]
    tm, tn = _tile(n, tm), _tile(d, tn)
    y_spec = pl.BlockSpec((tm, width), lambda i, j: (i, 0))
    w_spec = pl.BlockSpec((width, tn), lambda i, j: (0, j))

    def g_spec(branch):
        assert (g_col0 + branch * d) % tn == 0
        return pl.BlockSpec((tm, tn), lambda i, j: (i, (g_col0 + branch * d) // tn + j))

    return pl.pallas_call(
        _merge_kernel,
        grid=(n // tm, d // tn),
        in_specs=[y_spec] * 3 + [w_spec] * 3 + [g_spec(0), g_spec(1), g_spec(2)],
        out_specs=pl.BlockSpec((tm, tn), lambda i, j: (i, j)),
        out_shape=jax.ShapeDtypeStruct((n, d), BF16),
        compiler_params=_params([3 * tm * width * 2, 3 * width * tn * 2, 4 * tm * tn * 2], temp_bytes=4 * tm * tn * 4, n_axes=2),
        name="branch_merge",
    )(y_a, y_s, y_c, w_a, w_s, w_c, proj, proj, proj)


def _outproj_kernel(m_ref, w_ref, x_ref, o_ref):
    o_ref[...] = x_ref[...] + jnp.dot(m_ref[...], w_ref[...], preferred_element_type=F32)


def _outproj(merged, w_out, x, tm, tn):
    n, d = merged.shape
    tm, tn = _tile(n, tm), _tile(d, tn)
    return pl.pallas_call(
        _outproj_kernel,
        grid=(n // tm, d // tn),
        in_specs=[
            pl.BlockSpec((tm, d), lambda i, j: (i, 0)),
            pl.BlockSpec((d, tn), lambda i, j: (0, j)),
            pl.BlockSpec((tm, tn), lambda i, j: (i, j)),
        ],
        out_specs=pl.BlockSpec((tm, tn), lambda i, j: (i, j)),
        out_shape=jax.ShapeDtypeStruct((n, d), F32),
        compiler_params=_params([tm * d * 2, d * tn * 2, 2 * tm * tn * 4], temp_bytes=tm * tn * 4, n_axes=2),
        name="out_projection",
    )(merged, w_out, x)


def kernel(x, mem, w_in, g_norm, g_mem, w_mem_kv, q_gain_a, k_gain_a, q_gain_c, k_gain_c, ssm_a_re, ssm_a_im, ssm_log_dt, ssm_b_re, ssm_b_im, ssm_c_re, ssm_c_im, ssm_d, w_glu, b_glu, w_br_a, w_br_s, w_br_c, w_out):
    bsz, seq, d_model = x.shape
    width = w_glu.shape[0]
    n_tok = bsz * seq
    n_mem = mem.shape[1]
    assert seq % MOBA_BLOCK == 0 and seq % SSM_CHUNK == 0 and width % V7X_LANES == 0
    assert w_in.shape == (d_model, 8 * width + 3 * d_model)

    x2 = x.reshape(n_tok, d_model)
    h = _rmsnorm(x2, g_norm, rows=ROWS_RMSNORM)
    proj, slabs = _matmul(h, w_in.astype(BF16), TILE_IN_PROJECTION[0], _tile(width, TILE_IN_PROJECTION[1]),
                          "in_projection", slab_cols=5 * width)
    nb = width // V7X_LANES

    m = _rmsnorm(mem.reshape(bsz * n_mem, d_model), g_mem, rows=ROWS_RMSNORM)
    kv = _matmul(m, w_mem_kv, *TILE_MEMORY_KV, "memory_kv_projection")

    y_a = _moba(slabs, q_gain_a, k_gain_a, bsz, seq, width)

    y_g = _s5(slabs, 4 * nb, nb, ssm_a_re, ssm_a_im, ssm_log_dt, ssm_b_re, ssm_b_im, ssm_c_re, ssm_c_im, ssm_d, bsz)
    y_s = _glu(y_g, w_glu.astype(BF16), b_glu, proj, 0, *TILE_GLU)

    y_c = _memattn(proj, width, 2 * width, kv, q_gain_c, k_gain_c, bsz, seq, width, ROWS_MEMORY_ATTENTION)

    merged = _merge(y_a, y_s, y_c, w_br_a.astype(BF16), w_br_s.astype(BF16), w_br_c.astype(BF16), proj,
                    3 * width, *TILE_BRANCH_MERGE)
    out = _outproj(merged, w_out.astype(BF16), x2, *TILE_OUT_PROJECTION)
    return out.reshape(bsz, seq, d_model)
```
